```python
import math
import jax, jax.numpy as jnp
from jax import lax
import numpy as np

D_MODEL = 2048
BATCH = 16
SEQ = 2048
DEPTH = 2

CHUNK = 64
EPS = 1e-6
NEG_INF = -1e30
ROPE_THETA = 500000.0

N_HEADS_A = 8
HEAD_DIM = 128
ROT_DIM = HEAD_DIM // 4
N_IDX_HEADS = 8
IDX_DIM = 64
IDX_ROT_DIM = IDX_DIM // 4
TOPK_MAX = 256
Q_BLOCK = 128
WIDTH_A = N_HEADS_A * HEAD_DIM

WIDTH_B = D_MODEL // 4
SSM_GROUP = 16
N_SSM_GROUPS = WIDTH_B // SSM_GROUP
SSM_STATE = 64

WIDTH_C = D_MODEL // 4
POOL_WINDOWS = (2, 4, 8, 16)
POOL_GROUP = WIDTH_C // 4

N_BRANCH = 3

D_FF = 5504
CONV_WIDTH = 3

Q_W = WIDTH_A
K_W = HEAD_DIM
V_W = HEAD_DIM
QI_W = N_IDX_HEADS * IDX_DIM
KI_W = IDX_DIM
WI_W = N_IDX_HEADS
U_W = WIDTH_B
P_W = WIDTH_C
IN_SPLITS = (Q_W, Q_W + K_W, Q_W + K_W + V_W, Q_W + K_W + V_W + QI_W,
             Q_W + K_W + V_W + QI_W + KI_W, Q_W + K_W + V_W + QI_W + KI_W + WI_W,
             Q_W + K_W + V_W + QI_W + KI_W + WI_W + U_W)
D_IN = Q_W + K_W + V_W + QI_W + KI_W + WI_W + U_W + P_W

kernel_name = "chunk_causal_hybrid_dsa_s5_pool_convffn"


def rms_norm(x, g):
    xf = x.astype(jnp.float32)
    y = xf * lax.rsqrt(jnp.mean(xf * xf, axis=-1, keepdims=True) + EPS)
    return (y * g.astype(jnp.float32)).astype(x.dtype)


def partial_rope(x, pos, rot_dim):
    half = rot_dim // 2
    inv_freq = ROPE_THETA ** (-jnp.arange(half, dtype=jnp.float32) * (2.0 / rot_dim))
    ang = pos.astype(jnp.float32)[..., None] * inv_freq
    cos = jnp.cos(ang)[:, :, None, :]
    sin = jnp.sin(ang)[:, :, None, :]
    xf = x.astype(jnp.float32)
    x1 = xf[..., :half]
    x2 = xf[..., half:rot_dim]
    out = jnp.concatenate([x1 * cos - x2 * sin, x2 * cos + x1 * sin, xf[..., rot_dim:]], axis=-1)
    return out.astype(x.dtype)


def dsa_attention(q, k, v, qi, ki, wi, n_topk):
    B, L, H, dh = q.shape
    nb = L // Q_BLOCK
    key_pos = jnp.arange(L)
    kf = k.astype(jnp.float32)
    vf = v.astype(jnp.float32)
    kif = ki.astype(jnp.float32)

    def block(args):
        bi, qb, qib, wib = args
        qpos = bi * Q_BLOCK + jnp.arange(Q_BLOCK)
        limit = (qpos // CHUNK + 1) * CHUNK
        admissible = key_pos[None, :] < limit[:, None]
        s = jnp.einsum('bqhd,bsd->bqhs', qib.astype(jnp.float32), kif) * (IDX_DIM ** -0.5)
        idx_score = jnp.einsum('bqhs,bqh->bqs', jax.nn.relu(s), wib.astype(jnp.float32))
        idx_score = jnp.where(admissible[None], idx_score, NEG_INF)
        top_val, top_idx = lax.top_k(idx_score, n_topk)
        valid = top_val > NEG_INF * 0.5
        kg = jax.vmap(lambda kk, ii: kk[ii])(kf, top_idx)
        vg = jax.vmap(lambda vv, ii: vv[ii])(vf, top_idx)
        logits = jnp.einsum('bqhd,bqkd->bqhk', qb.astype(jnp.float32), kg) * (dh ** -0.5)
        logits = jnp.where(valid[:, :, None, :], logits, NEG_INF)
        probs = jax.nn.softmax(logits, axis=-1)
        o = jnp.einsum('bqhk,bqkd->bqhd', probs, vg)
        return o.astype(q.dtype)

    qs = q.reshape(B, nb, Q_BLOCK, H, dh).transpose(1, 0, 2, 3, 4)
    qis = qi.reshape(B, nb, Q_BLOCK, N_IDX_HEADS, IDX_DIM).transpose(1, 0, 2, 3, 4)
    wis = wi.reshape(B, nb, Q_BLOCK, N_IDX_HEADS).transpose(1, 0, 2, 3)
    out = lax.map(block, (jnp.arange(nb), qs, qis, wis))
    return out.transpose(1, 0, 2, 3, 4).reshape(B, L, H * dh)


def s5_mixer(u, a_re, a_im, b_re, b_im, c_re, c_im, d_skip, log_dt, w_glu):
    B, L, _ = u.shape
    G, P, I = N_SSM_GROUPS, SSM_STATE, SSM_GROUP
    f32 = jnp.float32
    uf = u.astype(f32).reshape(B, L, G, I)
    lam = lax.complex(a_re.astype(f32), a_im.astype(f32))
    dt = jnp.exp(log_dt.astype(f32))[:, None]
    lam_bar = jnp.exp(lam * dt)
    b_bar = ((lam_bar - 1.0) / lam)[..., None] * lax.complex(b_re.astype(f32), b_im.astype(f32))
    bu = jnp.einsum('gpi,blgi->blgp', b_bar, uf.astype(jnp.complex64))
    a_elems = jnp.broadcast_to(lam_bar, (1, L, G, P))

    def combine(left, right):
        a_l, b_l = left
        a_r, b_r = right
        return a_r * a_l, a_r * b_l + b_r

    _, states = lax.associative_scan(combine, (a_elems, bu), axis=1)
    c_t = lax.complex(c_re.astype(f32), c_im.astype(f32))
    y = jnp.einsum('gip,blgp->blgi', c_t, states).real + d_skip.astype(f32).reshape(G, I) * uf
    y = jax.nn.gelu(y.reshape(B, L, WIDTH_B))
    y = y * jax.nn.sigmoid(y @ w_glu.astype(f32))
    return y.astype(u.dtype)


def pool_mixer(p, w_pool, pool_scale):
    B, L, _ = p.shape
    pf = p.astype(jnp.float32)
    cs = jnp.pad(jnp.cumsum(pf, axis=1), ((0, 0), (1, 0), (0, 0)))
    t1 = jnp.arange(1, L + 1, dtype=jnp.float32)
    outs = []
    for g, w in enumerate(POOL_WINDOWS):
        sl = slice(g * POOL_GROUP, (g + 1) * POOL_GROUP)
        upper = cs[:, 1:, sl]
        lower = jnp.pad(cs[:, :L + 1 - w, sl], ((0, 0), (w - 1, 0), (0, 0)))
        mean = (upper - lower) / jnp.minimum(t1, float(w))[None, :, None]
        outs.append(mean - pf[:, :, sl])
    pooled = jnp.stack(outs, axis=2)
    y = jnp.einsum('blgi,gio->blgo', pooled, w_pool.astype(jnp.float32)).reshape(B, L, WIDTH_C)
    return (y * pool_scale.astype(jnp.float32)).astype(p.dtype)


def hybrid_mixer(h, positions, n_topk, w_in, g_q, g_k, a_re, a_im, b_re, b_im, c_re, c_im,
                 d_skip, log_dt, w_glu, w_pool, pool_scale, p_a, p_b, p_c, w_gate, b_gate, w_out):
    B, L, _ = h.shape
    proj = h @ w_in
    q, k, v, qi, ki, wi, u, p = jnp.split(proj, IN_SPLITS, axis=-1)
    q = partial_rope(rms_norm(q.reshape(B, L, N_HEADS_A, HEAD_DIM), g_q), positions, ROT_DIM)
    k = partial_rope(rms_norm(k, g_k)[:, :, None, :], positions, ROT_DIM)[:, :, 0]
    qi = partial_rope(qi.reshape(B, L, N_IDX_HEADS, IDX_DIM), positions, IDX_ROT_DIM)
    ki = partial_rope(ki[:, :, None, :], positions, IDX_ROT_DIM)[:, :, 0]
    o_a = dsa_attention(q, k, v, qi, ki, wi * (N_IDX_HEADS ** -0.5), n_topk)
    o_b = s5_mixer(u, a_re, a_im, b_re, b_im, c_re, c_im, d_skip, log_dt, w_glu)
    o_c = pool_mixer(p, w_pool, pool_scale)
    merged = jax.nn.sigmoid(h @ w_gate[0] + b_gate[0]) * (o_a @ p_a)
    merged = merged + jax.nn.sigmoid(h @ w_gate[1] + b_gate[1]) * (o_b @ p_b)
    merged = merged + jax.nn.sigmoid(h @ w_gate[2] + b_gate[2]) * (o_c @ p_c)
    return merged @ w_out


def conv_ffn(h, w_up, conv_w, conv_b, w_down):
    L = h.shape[1]
    a, b = jnp.split(h @ w_up, 2, axis=-1)
    a_pad = jnp.pad(a, ((0, 0), (CONV_WIDTH - 1, 0), (0, 0)))
    a_conv = conv_b + a_pad[:, 0:L] * conv_w[0]
    for j in range(1, CONV_WIDTH):
        a_conv = a_conv + a_pad[:, j:j + L] * conv_w[j]
    return (jax.nn.silu(a_conv) * b) @ w_down


def setup_inputs(seed: int = 0) -> dict:
    key = jax.random.key(seed)
    ks = jax.random.split(key, 40)
    f32 = jnp.float32
    G, P, I = N_SSM_GROUPS, SSM_STATE, SSM_GROUP

    def nrm(k, shape, scale):
        return jax.random.normal(k, shape, f32) * scale

    x = nrm(ks[0], (BATCH, SEQ, D_MODEL), 1.0)
    c = nrm(ks[1], (BATCH, D_MODEL), 1.0)
    offsets = jax.random.randint(ks[2], (BATCH, 1), 0, 64) * CHUNK
    positions = (offsets + jnp.arange(SEQ, dtype=jnp.int32)[None, :]).astype(jnp.int32)
    return {
        "x": x,
        "c": c,
        "positions": positions,
        "w_ada": nrm(ks[3], (DEPTH, D_MODEL, 6 * D_MODEL), 0.5 * D_MODEL ** -0.5),
        "b_ada": nrm(ks[4], (DEPTH, 6 * D_MODEL), 0.02),
        "g_norm1": 1.0 + nrm(ks[5], (DEPTH, D_MODEL), 0.02),
        "g_norm2": 1.0 + nrm(ks[6], (DEPTH, D_MODEL), 0.02),
        "w_in": nrm(ks[7], (DEPTH, D_MODEL, D_IN), D_MODEL ** -0.5),
        "g_q": 1.0 + nrm(ks[8], (DEPTH, HEAD_DIM), 0.02),
        "g_k": 1.0 + nrm(ks[9], (DEPTH, HEAD_DIM), 0.02),
        "a_re": -0.5 * (1.0 + nrm(ks[10], (DEPTH, G, P), 0.01)),
        "a_im": math.pi * jnp.arange(P, dtype=f32)[None, None, :] + nrm(ks[11], (DEPTH, G, P), 0.01),
        "b_re": nrm(ks[12], (DEPTH, G, P, I), (2.0 * I) ** -0.5),
        "b_im": nrm(ks[13], (DEPTH, G, P, I), (2.0 * I) ** -0.5),
        "c_re": nrm(ks[14], (DEPTH, G, I, P), (2.0 * P) ** -0.5),
        "c_im": nrm(ks[15], (DEPTH, G, I, P), (2.0 * P) ** -0.5),
        "d_skip": nrm(ks[16], (DEPTH, WIDTH_B), 1.0),
        "log_dt": jax.random.uniform(ks[17], (DEPTH, G), f32, math.log(1e-3), math.log(1e-1)),
        "w_glu": nrm(ks[18], (DEPTH, WIDTH_B, WIDTH_B), WIDTH_B ** -0.5),
        "w_pool": nrm(ks[19], (DEPTH, 4, POOL_GROUP, POOL_GROUP), POOL_GROUP ** -0.5),
        "pool_scale": 1.0 + nrm(ks[20], (DEPTH, WIDTH_C), 0.02),
        "p_a": nrm(ks[21], (DEPTH, WIDTH_A, D_MODEL), WIDTH_A ** -0.5),
        "p_b": nrm(ks[22], (DEPTH, WIDTH_B, D_MODEL), WIDTH_B ** -0.5),
        "p_c": nrm(ks[23], (DEPTH, WIDTH_C, D_MODEL), WIDTH_C ** -0.5),
        "w_gate": nrm(ks[24], (DEPTH, N_BRANCH, D_MODEL, D_MODEL), D_MODEL ** -0.5),
        "b_gate": nrm(ks[25], (DEPTH, N_BRANCH, D_MODEL), 0.02),
        "w_out": nrm(ks[26], (DEPTH, D_MODEL, D_MODEL), D_MODEL ** -0.5),
        "w_up": nrm(ks[27], (DEPTH, D_MODEL, 2 * D_FF), D_MODEL ** -0.5),
        "conv_w": nrm(ks[28], (DEPTH, CONV_WIDTH, D_FF), CONV_WIDTH ** -0.5),
        "conv_b": nrm(ks[29], (DEPTH, D_FF), 0.02),
        "w_down": nrm(ks[30], (DEPTH, D_FF, D_MODEL), D_FF ** -0.5),
    }


def reference(x, c, positions, w_ada, b_ada, g_norm1, g_norm2, w_in, g_q, g_k, a_re, a_im,
              b_re, b_im, c_re, c_im, d_skip, log_dt, w_glu, w_pool, pool_scale, p_a, p_b, p_c,
              w_gate, b_gate, w_out, w_up, conv_w, conv_b, w_down):
    L = x.shape[1]
    n_topk = min(TOPK_MAX, L // 4)
    c_act = jax.nn.silu(c)
    for l in range(DEPTH):
        mod = c_act @ w_ada[l] + b_ada[l]
        sh1, sc1, gt1, sh2, sc2, gt2 = [m[:, None, :] for m in jnp.split(mod, 6, axis=-1)]
        h = rms_norm(x, g_norm1[l]) * (1.0 + sc1) + sh1
        mix = hybrid_mixer(h, positions, n_topk, w_in[l], g_q[l], g_k[l], a_re[l], a_im[l],
                           b_re[l], b_im[l], c_re[l], c_im[l], d_skip[l], log_dt[l], w_glu[l],
                           w_pool[l], pool_scale[l], p_a[l], p_b[l], p_c[l], w_gate[l],
                           b_gate[l], w_out[l])
        x = x + gt1 * mix
        h = rms_norm(x, g_norm2[l]) * (1.0 + sc2) + sh2
        x = x + gt2 * conv_ffn(h, w_up[l], conv_w[l], conv_b[l], w_down[l])
    return x
```

```python
import functools
import math

import jax
import jax.numpy as jnp
from jax import lax
from jax.experimental import pallas as pl
from jax.experimental.pallas import tpu as pltpu

F32 = jnp.float32
BF16 = jnp.bfloat16
I32 = jnp.int32

D_MODEL = 2048
DEPTH = 2
CHUNK = 64
EPS = 1e-6
NEG_INF = -1e30
ROPE_THETA = 500000.0

N_HEADS_A = 8
HEAD_DIM = 128
ROT_DIM = HEAD_DIM // 4
N_IDX_HEADS = 8
IDX_DIM = 64
IDX_ROT_DIM = IDX_DIM // 4
TOPK_MAX = 256
Q_BLOCK = 128
WIDTH_A = N_HEADS_A * HEAD_DIM

WIDTH_B = D_MODEL // 4
SSM_GROUP = 16
N_SSM_GROUPS = WIDTH_B // SSM_GROUP
SSM_STATE = 64
SSM_BUNDLE = 8
N_SSM_BUNDLES = N_SSM_GROUPS // SSM_BUNDLE
BUNDLE_CH = SSM_BUNDLE * SSM_GROUP
BUNDLE_ST = SSM_BUNDLE * SSM_STATE

WIDTH_C = D_MODEL // 4
POOL_WINDOWS = (2, 4, 8, 16)
POOL_GROUP = WIDTH_C // 4

D_FF = 5504
CONV_WIDTH = 3

LANES = 128
INT_MIN = -(2 ** 31)

COL_K = WIDTH_A
COL_V = COL_K + HEAD_DIM
COL_QI = COL_V + HEAD_DIM
COL_KW = COL_QI + N_IDX_HEADS * IDX_DIM
COL_U = COL_KW + LANES
COL_P = COL_U + WIDTH_B
D_IN_PAD = COL_P + WIDTH_C
KW_USED = IDX_DIM + N_IDX_HEADS

VMEM_LIMIT = 56 * 1024 * 1024


def _cparams(sem):
    return pltpu.CompilerParams(dimension_semantics=sem, vmem_limit_bytes=VMEM_LIMIT)


def _norm_mod(x, g, sc, sh):
    ms = jnp.mean(x * x, axis=-1, keepdims=True)
    y = x * lax.rsqrt(ms + EPS)
    return (y * g) * (1.0 + sc) + sh


def _rope(x, c, sa, sb, half):
    return x * c + pltpu.roll(x, LANES - half, 1) * sa + pltpu.roll(x, half, 1) * sb


def _ada_kernel(c_ref, w_ref, b_ref, o_ref):
    c = c_ref[...]
    ca = c * jax.nn.sigmoid(c)
    o_ref[0] = jnp.dot(ca, w_ref[0], preferred_element_type=F32) + b_ref[0]


def _ada(c, w_ada, b_ada, tn=1024):
    depth, d, n = w_ada.shape
    b = c.shape[0]
    return pl.pallas_call(
        _ada_kernel,
        grid=(depth, n // tn),
        in_specs=[
            pl.BlockSpec((b, d), lambda l, j: (0, 0)),
            pl.BlockSpec((1, d, tn), lambda l, j: (l, 0, j)),
            pl.BlockSpec((1, 1, tn), lambda l, j: (l, 0, j)),
        ],
        out_specs=pl.BlockSpec((1, b, tn), lambda l, j: (l, 0, j)),
        out_shape=jax.ShapeDtypeStruct((depth, b, n), F32),
        compiler_params=_cparams(("arbitrary", "arbitrary")),
        name="ada",
    )(c, w_ada, b_ada.reshape(depth, 1, n))


def _proj_kernel(x_ref, mod_ref, g_ref, w_ref, gq_ref, gk_ref, cq_ref, saq_ref, sbq_ref,
                 ci_ref, sai_ref, sbi_ref,
                 q_ref, k_ref, v_ref, qi_ref, ki2_ref, kw_ref, u_ref, p_ref):
    h = _norm_mod(x_ref[...], g_ref[...], mod_ref[0, 1:2, :], mod_ref[0, 0:1, :]).astype(BF16)

    def mm(c0, width):
        return jnp.dot(h, w_ref[:, c0:c0 + width], preferred_element_type=F32)

    def qk_norm_rope(xh, g):
        ms = jnp.mean(xh * xh, axis=-1, keepdims=True)
        y = xh * lax.rsqrt(ms + EPS) * g
        return _rope(y, cq_ref[...], saq_ref[...], sbq_ref[...], ROT_DIM // 2)

    for hd in range(N_HEADS_A):
        sl = slice(hd * HEAD_DIM, (hd + 1) * HEAD_DIM)
        q_ref[:, sl] = qk_norm_rope(mm(hd * HEAD_DIM, HEAD_DIM), gq_ref[...]).astype(BF16)
    k_ref[...] = qk_norm_rope(mm(COL_K, HEAD_DIM), gk_ref[...]).astype(BF16)
    v_ref[...] = mm(COL_V, HEAD_DIM).astype(BF16)
    for s in range(N_IDX_HEADS * IDX_DIM // LANES):
        sl = slice(s * LANES, (s + 1) * LANES)
        qi = mm(COL_QI + s * LANES, LANES)
        qi_ref[:, sl] = _rope(qi, ci_ref[...], sai_ref[...], sbi_ref[...], IDX_ROT_DIM // 2).astype(BF16)
    kw = mm(COL_KW, LANES)
    ki = _rope(kw, ci_ref[...], sai_ref[...], sbi_ref[...], IDX_ROT_DIM // 2)
    lane = lax.broadcasted_iota(I32, ki.shape, 1)
    ki_lo = jnp.where(lane < IDX_DIM, ki, 0.0)
    ki2_ref[:, 0:LANES] = ki_lo.astype(BF16)
    ki2_ref[:, LANES:2 * LANES] = pltpu.roll(ki_lo, IDX_DIM, 1).astype(BF16)
    kw_ref[...] = kw * (N_IDX_HEADS ** -0.5 * IDX_DIM ** -0.5)
    u_ref[...] = mm(COL_U, WIDTH_B)
    p_ref[...] = mm(COL_P, WIDTH_C)


def _proj(x2, mod, g1, w_in_pad, g_q, g_k, tabs_q, tabs_i, batch, seq, tm):
    n, d = x2.shape
    tpb = seq // tm
    row = lambda i: (i, 0)
    const = lambda i: (0, 0)
    tab_spec = pl.BlockSpec((tm, LANES), row)
    return pl.pallas_call(
        _proj_kernel,
        grid=(n // tm,),
        in_specs=[
            pl.BlockSpec((tm, d), row),
            pl.BlockSpec((1, 6, d), lambda i: (i // tpb, 0, 0)),
            pl.BlockSpec((1, d), const),
            pl.BlockSpec((d, D_IN_PAD), const),
            pl.BlockSpec((1, HEAD_DIM), const),
            pl.BlockSpec((1, HEAD_DIM), const),
            tab_spec, tab_spec, tab_spec, tab_spec, tab_spec, tab_spec,
        ],
        out_specs=[
            pl.BlockSpec((tm, WIDTH_A), row),
            pl.BlockSpec((tm, HEAD_DIM), row),
            pl.BlockSpec((tm, HEAD_DIM), row),
            pl.BlockSpec((tm, N_IDX_HEADS * IDX_DIM), row),
            pl.BlockSpec((tm, 2 * LANES), row),
            pl.BlockSpec((tm, LANES), row),
            pl.BlockSpec((tm, WIDTH_B), lambda i: (i % tpb, i // tpb)),
            pl.BlockSpec((tm, WIDTH_C), row),
        ],
        out_shape=[
            jax.ShapeDtypeStruct((n, WIDTH_A), BF16),
            jax.ShapeDtypeStruct((n, HEAD_DIM), BF16),
            jax.ShapeDtypeStruct((n, HEAD_DIM), BF16),
            jax.ShapeDtypeStruct((n, N_IDX_HEADS * IDX_DIM), BF16),
            jax.ShapeDtypeStruct((n, 2 * LANES), BF16),
            jax.ShapeDtypeStruct((n, LANES), F32),
            jax.ShapeDtypeStruct((seq, batch * WIDTH_B), F32),
            jax.ShapeDtypeStruct((n, WIDTH_C), F32),
        ],
        compiler_params=_cparams(("arbitrary",)),
        name="proj",
    )(x2, mod, g1, w_in_pad, g_q, g_k, *tabs_q, *tabs_i)


def _dsa_kernel(q_ref, k_ref, v_ref, qi_ref, ki2_ref, kw_ref, o_ref,
                s_ref, m_ref, l_ref, acc_ref, *, topk, kb):
    qb = pl.program_id(1)
    nkb = (qb * Q_BLOCK + Q_BLOCK + kb - 1) // kb
    nslab = kb // LANES
    row = lax.broadcasted_iota(I32, (Q_BLOCK, kb), 0)
    lane = lax.broadcasted_iota(I32, (Q_BLOCK, kb), 1)
    limit_row = qb * Q_BLOCK + (row // CHUNK + 1) * CHUNK
    w = kw_ref[...]
    wb = [jnp.broadcast_to(w[:, IDX_DIM + hd:IDX_DIM + hd + 1], (Q_BLOCK, kb)) for hd in range(N_IDX_HEADS)]
    trans_b = (((1,), (1,)), ((), ()))

    def score_body(j, carry):
        k0 = pl.multiple_of(j * kb, kb)
        kia = ki2_ref[pl.ds(k0, kb), 0:LANES]
        kib = ki2_ref[pl.ds(k0, kb), LANES:2 * LANES]
        sc = jnp.zeros((Q_BLOCK, kb), F32)
        for pr in range(N_IDX_HEADS // 2):
            qp = qi_ref[:, pr * LANES:(pr + 1) * LANES]
            sa = lax.dot_general(qp, kia, trans_b, preferred_element_type=F32)
            sb = lax.dot_general(qp, kib, trans_b, preferred_element_type=F32)
            sc = sc + jnp.maximum(sa, 0.0) * wb[2 * pr] + jnp.maximum(sb, 0.0) * wb[2 * pr + 1]
        adm = ((k0 + lane) < limit_row) & (sc > NEG_INF * 0.5)
        bits = pltpu.bitcast(sc, I32)
        skey = bits ^ ((bits >> 31) & 0x7FFFFFFF)
        s_ref[:, pl.ds(k0, kb)] = jnp.where(adm, skey, INT_MIN)
        return carry

    lax.fori_loop(0, nkb, score_body, 0)

    def bit_body(i, thr):
        cand = thr + jnp.left_shift(jnp.int32(1), 31 - i)

        def cnt_body(j, acc):
            k0 = pl.multiple_of(j * kb, kb)
            blk = s_ref[:, pl.ds(k0, kb)]
            for s in range(nslab):
                acc = acc + jnp.where(blk[:, s * LANES:(s + 1) * LANES] >= cand, 1.0, 0.0)
            return acc

        acc = lax.fori_loop(0, nkb, cnt_body, jnp.zeros((Q_BLOCK, LANES), F32))
        cnt = jnp.sum(acc, axis=1, keepdims=True)
        return jnp.where(cnt >= float(topk), cand, thr)

    thr = lax.fori_loop(0, 32, bit_body, jnp.full((Q_BLOCK, LANES), INT_MIN, I32))
    thr_kb = jnp.concatenate([thr] * nslab, axis=1) if nslab > 1 else thr

    m_ref[...] = jnp.full(m_ref.shape, NEG_INF, F32)
    l_ref[...] = jnp.zeros(l_ref.shape, F32)
    acc_ref[...] = jnp.zeros(acc_ref.shape, F32)
    scale = HEAD_DIM ** -0.5

    def att_body(j, carry):
        k0 = pl.multiple_of(j * kb, kb)
        kj = k_ref[pl.ds(k0, kb), :]
        vj = v_ref[pl.ds(k0, kb), :]
        sk = s_ref[:, pl.ds(k0, kb)]
        mask = (sk >= thr_kb) & (sk > INT_MIN)
        for hd in range(N_HEADS_A):
            rows = slice(hd * Q_BLOCK, (hd + 1) * Q_BLOCK)
            qh = q_ref[:, hd * HEAD_DIM:(hd + 1) * HEAD_DIM]
            s = lax.dot_general(qh, kj, trans_b, preferred_element_type=F32) * scale
            s = jnp.where(mask, s, NEG_INF)
            m_old = m_ref[rows, :]
            m_new = jnp.maximum(m_old, jnp.max(s, axis=1, keepdims=True))
            alpha = jnp.exp(m_old - m_new)
            p = jnp.where(mask, jnp.exp(s - m_new[:, 0:1]), 0.0)
            l_ref[rows, :] = alpha * l_ref[rows, :] + jnp.sum(p, axis=1, keepdims=True)
            acc_ref[rows, :] = alpha * acc_ref[rows, :] + jnp.dot(p.astype(BF16), vj, preferred_element_type=F32)
            m_ref[rows, :] = m_new
        return carry

    lax.fori_loop(0, nkb, att_body, 0)
    for hd in range(N_HEADS_A):
        rows = slice(hd * Q_BLOCK, (hd + 1) * Q_BLOCK)
        o_ref[:, hd * HEAD_DIM:(hd + 1) * HEAD_DIM] = (acc_ref[rows, :] / l_ref[rows, :]).astype(BF16)


def _dsa(q, k, v, qi, ki2, kw, batch, seq, topk, kb):
    n = q.shape[0]
    nqb = seq // Q_BLOCK
    kb = min(kb, seq)
    qrow = lambda b, i: (b * nqb + i, 0)
    brow = lambda b, i: (b, 0)
    return pl.pallas_call(
        functools.partial(_dsa_kernel, topk=topk, kb=kb),
        grid=(batch, nqb),
        in_specs=[
            pl.BlockSpec((Q_BLOCK, WIDTH_A), qrow),
            pl.BlockSpec((seq, HEAD_DIM), brow),
            pl.BlockSpec((seq, HEAD_DIM), brow),
            pl.BlockSpec((Q_BLOCK, N_IDX_HEADS * IDX_DIM), qrow),
            pl.BlockSpec((seq, 2 * LANES), brow),
            pl.BlockSpec((Q_BLOCK, LANES), qrow),
        ],
        out_specs=pl.BlockSpec((Q_BLOCK, WIDTH_A), qrow),
        out_shape=jax.ShapeDtypeStruct((n, WIDTH_A), BF16),
        scratch_shapes=[
            pltpu.VMEM((Q_BLOCK, seq), I32),
            pltpu.VMEM((N_HEADS_A * Q_BLOCK, LANES), F32),
            pltpu.VMEM((N_HEADS_A * Q_BLOCK, LANES), F32),
            pltpu.VMEM((N_HEADS_A * Q_BLOCK, HEAD_DIM), F32),
        ],
        compiler_params=_cparams(("arbitrary", "arbitrary")),
        name="dsa",
    )(q, k, v, qi, ki2, kw)


def _s5_kernel(u_ref, bm_ref, cm_ref, lre_ref, lim_ref, dsk_ref, wglu_ref, o_ref,
               bu_ref, st_ref, y_ref, *, batch, tl):
    @pl.when(pl.program_id(0) == 0)
    def _():
        st_ref[...] = jnp.zeros(st_ref.shape, F32)

    for gb in range(N_SSM_BUNDLES):
        ch = slice(gb * BUNDLE_CH, (gb + 1) * BUNDLE_CH)
        u_g = u_ref[:, ch]
        bu_ref[...] = jnp.dot(u_g.astype(BF16), bm_ref[gb], preferred_element_type=F32)
        lam_re = jnp.broadcast_to(lre_ref[gb], (batch, BUNDLE_ST))
        lam_im = jnp.broadcast_to(lim_ref[gb], (batch, BUNDLE_ST))

        def step(t, carry):
            x_re, x_im = carry
            r0 = pl.multiple_of(t * batch, batch)
            n_re = lam_re * x_re - lam_im * x_im + bu_ref[pl.ds(r0, batch), 0:BUNDLE_ST]
            n_im = lam_re * x_im + lam_im * x_re + bu_ref[pl.ds(r0, batch), BUNDLE_ST:2 * BUNDLE_ST]
            bu_ref[pl.ds(r0, batch), 0:BUNDLE_ST] = n_re
            bu_ref[pl.ds(r0, batch), BUNDLE_ST:2 * BUNDLE_ST] = n_im
            return n_re, n_im

        x_re, x_im = lax.fori_loop(
            0, tl, step, (st_ref[gb, :, 0:BUNDLE_ST], st_ref[gb, :, BUNDLE_ST:2 * BUNDLE_ST]))
        st_ref[gb, :, 0:BUNDLE_ST] = x_re
        st_ref[gb, :, BUNDLE_ST:2 * BUNDLE_ST] = x_im
        y = jnp.dot(bu_ref[...].astype(BF16), cm_ref[gb], preferred_element_type=F32)
        y_ref[:, ch] = y + dsk_ref[:, ch] * u_g

    y = jax.nn.gelu(y_ref[...])
    gl = jnp.dot(y.astype(BF16), wglu_ref[...], preferred_element_type=F32)
    o_ref[...] = (y * jax.nn.sigmoid(gl)).astype(BF16)


def _s5_params(a_re, a_im, b_re, b_im, c_re, c_im, log_dt):
    dt = jnp.exp(log_dt)[:, None]
    mag = jnp.exp(a_re * dt)
    lb_re = mag * jnp.cos(a_im * dt)
    lb_im = mag * jnp.sin(a_im * dt)
    nr, ni = lb_re - 1.0, lb_im
    den = a_re * a_re + a_im * a_im
    q_re = (nr * a_re + ni * a_im) / den
    q_im = (ni * a_re - nr * a_im) / den
    bb_re = q_re[..., None] * b_re - q_im[..., None] * b_im
    bb_im = q_re[..., None] * b_im + q_im[..., None] * b_re
    eye = jnp.eye(SSM_BUNDLE, dtype=F32)
    nb = N_SSM_BUNDLES

    def pack_b(m):
        m = m.reshape(nb, SSM_BUNDLE, SSM_STATE, SSM_GROUP)
        return jnp.einsum('bgpi,gh->bgihp', m, eye).reshape(nb, BUNDLE_CH, BUNDLE_ST)

    def pack_c(m):
        m = m.reshape(nb, SSM_BUNDLE, SSM_GROUP, SSM_STATE)
        return jnp.einsum('bgop,gh->bgpho', m, eye).reshape(nb, BUNDLE_ST, BUNDLE_CH)

    bm = jnp.concatenate([pack_b(bb_re), pack_b(bb_im)], axis=2).astype(BF16)
    cm = jnp.concatenate([pack_c(c_re), -pack_c(c_im)], axis=1).astype(BF16)
    return bm, cm, lb_re.reshape(nb, 1, BUNDLE_ST), lb_im.reshape(nb, 1, BUNDLE_ST)


def _s5(u_tm, bm, cm, lre, lim, d_skip, w_glu, batch, seq, tl):
    rows = tl * batch
    const3 = lambda t: (0, 0, 0)
    const2 = lambda t: (0, 0)
    return pl.pallas_call(
        functools.partial(_s5_kernel, batch=batch, tl=tl),
        grid=(seq // tl,),
        in_specs=[
            pl.BlockSpec((rows, WIDTH_B), lambda t: (t, 0)),
            pl.BlockSpec(bm.shape, const3),
            pl.BlockSpec(cm.shape, const3),
            pl.BlockSpec(lre.shape, const3),
            pl.BlockSpec(lim.shape, const3),
            pl.BlockSpec((1, WIDTH_B), const2),
            pl.BlockSpec((WIDTH_B, WIDTH_B), const2),
        ],
        out_specs=pl.BlockSpec((rows, WIDTH_B), lambda t: (t, 0)),
        out_shape=jax.ShapeDtypeStruct((seq * batch, WIDTH_B), BF16),
        scratch_shapes=[
            pltpu.VMEM((rows, 2 * BUNDLE_ST), F32),
            pltpu.VMEM((N_SSM_BUNDLES, batch, 2 * BUNDLE_ST), F32),
            pltpu.VMEM((rows, WIDTH_B), F32),
        ],
        compiler_params=_cparams(("arbitrary",)),
        name="s5",
    )(u_tm, bm, cm, lre, lim, d_skip, w_glu)


def _pool_kernel(p_ref, w_ref, sc_ref, o_ref):
    seq = p_ref.shape[0]
    t = lax.broadcasted_iota(I32, (seq, POOL_GROUP), 0)
    t1 = (t + 1).astype(F32)
    for g, win in enumerate(POOL_WINDOWS):
        ch = slice(g * POOL_GROUP, (g + 1) * POOL_GROUP)
        x = p_ref[:, ch]
        s = x
        sh = 1
        while sh < win:
            s = s + jnp.where(t >= sh, pltpu.roll(s, sh, 0), 0.0)
            sh *= 2
        pooled = s / jnp.minimum(t1, float(win)) - x
        y = jnp.dot(pooled.astype(BF16), w_ref[g], preferred_element_type=F32)
        o_ref[:, ch] = (y * sc_ref[:, ch]).astype(BF16)


def _pool(p, w_pool, pool_scale, batch, seq):
    return pl.pallas_call(
        _pool_kernel,
        grid=(batch,),
        in_specs=[
            pl.BlockSpec((seq, WIDTH_C), lambda b: (b, 0)),
            pl.BlockSpec(w_pool.shape, lambda b: (0, 0, 0)),
            pl.BlockSpec((1, WIDTH_C), lambda b: (0, 0)),
        ],
        out_specs=pl.BlockSpec((seq, WIDTH_C), lambda b: (b, 0)),
        out_shape=jax.ShapeDtypeStruct(p.shape, BF16),
        compiler_params=_cparams(("arbitrary",)),
        name="pool",
    )(p, w_pool, pool_scale)


def _merge_kernel(x_ref, xc_ref, mod_ref, g_ref, oa_ref, ob_ref, oc_ref, wg_ref, bg_ref,
                  pa_ref, pb_ref, pc_ref, wo_ref, gt_ref, o_ref, h_ref, mg_ref, *, nj, tn):
    j = pl.program_id(1)

    @pl.when(j == 0)
    def _():
        h_ref[...] = _norm_mod(x_ref[...], g_ref[...], mod_ref[0, 1:2, :], mod_ref[0, 0:1, :]).astype(BF16)

    @pl.when(j < nj)
    def _():
        h = h_ref[...]
        branches = ((oa_ref, pa_ref), (ob_ref, pb_ref), (oc_ref, pc_ref))
        merged = None
        for i, (o_r, p_r) in enumerate(branches):
            gate = jax.nn.sigmoid(jnp.dot(h, wg_ref[i], preferred_element_type=F32) + bg_ref[i:i + 1, :])
            term = gate * jnp.dot(o_r[...], p_r[...], preferred_element_type=F32)
            merged = term if merged is None else merged + term
        c0 = pl.multiple_of(j * tn, tn)
        mg_ref[:, pl.ds(c0, tn)] = merged.astype(BF16)

    @pl.when(j >= nj)
    def _():
        mix = jnp.dot(mg_ref[...], wo_ref[...], preferred_element_type=F32)
        o_ref[...] = xc_ref[...] + gt_ref[0] * mix


def _merge(x2, mod, gt, g1, o_a, o_b_tm, o_c, wg, bg, p_a, p_b, p_c, w_out, seq, tm, tn):
    n, d = x2.shape
    nj = d // tn
    tpb = seq // tm
    row = lambda i, j: (i, 0)
    const = lambda i, j: (0, 0)
    ph1 = lambda i, j: (0, jnp.minimum(j, nj - 1))
    ph2 = lambda i, j: (0, jnp.maximum(j - nj, 0))
    return pl.pallas_call(
        functools.partial(_merge_kernel, nj=nj, tn=tn),
        grid=(n // tm, 2 * nj),
        in_specs=[
            pl.BlockSpec((tm, d), row),
            pl.BlockSpec((tm, tn), lambda i, j: (i, jnp.maximum(j - nj, 0))),
            pl.BlockSpec((1, 6, d), lambda i, j: (i // tpb, 0, 0)),
            pl.BlockSpec((1, d), const),
            pl.BlockSpec((tm, WIDTH_A), row),
            pl.BlockSpec((tm, WIDTH_B), lambda i, j: (i % tpb, i // tpb)),
            pl.BlockSpec((tm, WIDTH_C), row),
            pl.BlockSpec((3, d, tn), lambda i, j: (0, 0, jnp.minimum(j, nj - 1))),
            pl.BlockSpec((3, tn), ph1),
            pl.BlockSpec((WIDTH_A, tn), ph1),
            pl.BlockSpec((WIDTH_B, tn), ph1),
            pl.BlockSpec((WIDTH_C, tn), ph1),
            pl.BlockSpec((d, tn), ph2),
            pl.BlockSpec((1, 1, tn), lambda i, j: (i // tpb, 0, jnp.maximum(j - nj, 0))),
        ],
        out_specs=pl.BlockSpec((tm, tn), lambda i, j: (i, jnp.maximum(j - nj, 0))),
        out_shape=jax.ShapeDtypeStruct((n, d), F32),
        scratch_shapes=[pltpu.VMEM((tm, d), BF16), pltpu.VMEM((tm, d), BF16)],
        compiler_params=_cparams(("arbitrary", "arbitrary")),
        name="merge",
    )(x2, x2, mod, g1, o_a, o_b_tm, o_c, wg, bg, p_a, p_b, p_c, w_out, gt)


FFN_HALO = 16


def _ffn_kernel(x_ref, xh_ref, mod_ref, g_ref, wa_ref, wb_ref, cw_ref, cb_ref, wd_ref, o_ref,
                h_ref, acc_ref, *, tm, tpb):
    i = pl.program_id(0)
    j = pl.program_id(1)
    nj = pl.num_programs(1)

    @pl.when(j == 0)
    def _():
        g, sc, sh = g_ref[...], mod_ref[0, 4:5, :], mod_ref[0, 3:4, :]
        halo = _norm_mod(xh_ref[...], g, sc, sh)
        h_ref[0:FFN_HALO, :] = jnp.where(i % tpb == 0, 0.0, halo).astype(BF16)
        h_ref[FFN_HALO:, :] = _norm_mod(x_ref[...], g, sc, sh).astype(BF16)
        acc_ref[...] = jnp.zeros(acc_ref.shape, F32)

    a = jnp.dot(h_ref[...], wa_ref[...], preferred_element_type=F32)
    b = jnp.dot(h_ref[FFN_HALO:, :], wb_ref[...], preferred_element_type=F32)
    a_conv = cb_ref[...] + a[FFN_HALO - 2:FFN_HALO - 2 + tm] * cw_ref[0:1, :]
    a_conv = a_conv + a[FFN_HALO - 1:FFN_HALO - 1 + tm] * cw_ref[1:2, :]
    a_conv = a_conv + a[FFN_HALO:] * cw_ref[2:3, :]
    act = (a_conv * jax.nn.sigmoid(a_conv)) * b
    acc_ref[...] += jnp.dot(act.astype(BF16), wd_ref[...], preferred_element_type=F32)

    @pl.when(j == nj - 1)
    def _():
        o_ref[...] = x_ref[...] + mod_ref[0, 5:6, :] * acc_ref[...]


def _ffn(x2, mod, g2, w_a, w_b, conv_w, conv_b, w_down, seq, tm, tn):
    n, d = x2.shape
    dff = w_a.shape[1]
    tpb = seq // tm
    hb = tm // FFN_HALO
    return pl.pallas_call(
        functools.partial(_ffn_kernel, tm=tm, tpb=tpb),
        grid=(n // tm, dff // tn),
        in_specs=[
            pl.BlockSpec((tm, d), lambda i, j: (i, 0)),
            pl.BlockSpec((FFN_HALO, d), lambda i, j: (jnp.maximum(i * hb - 1, 0), 0)),
            pl.BlockSpec((1, 6, d), lambda i, j: (i // tpb, 0, 0)),
            pl.BlockSpec((1, d), lambda i, j: (0, 0)),
            pl.BlockSpec((d, tn), lambda i, j: (0, j)),
            pl.BlockSpec((d, tn), lambda i, j: (0, j)),
            pl.BlockSpec((CONV_WIDTH, tn), lambda i, j: (0, j)),
            pl.BlockSpec((1, tn), lambda i, j: (0, j)),
            pl.BlockSpec((tn, d), lambda i, j: (j, 0)),
        ],
        out_specs=pl.BlockSpec((tm, d), lambda i, j: (i, 0)),
        out_shape=jax.ShapeDtypeStruct((n, d), F32),
        scratch_shapes=[pltpu.VMEM((tm + FFN_HALO, d), BF16), pltpu.VMEM((tm, d), F32)],
        compiler_params=_cparams(("arbitrary", "arbitrary")),
        name="ffn",
    )(x2, x2, mod, g2, w_a, w_b, conv_w, conv_b, w_down)


def _rope_tables(positions, rot_dim, period):
    half = rot_dim // 2
    inv_freq = ROPE_THETA ** (-jnp.arange(half, dtype=F32) * (2.0 / rot_dim))
    ang = positions.astype(F32)[..., None] * inv_freq
    cos, sin = jnp.cos(ang), jnp.sin(ang)
    rest = period - 2 * half
    ones = jnp.ones(cos.shape[:-1] + (rest,), F32)
    zh = jnp.zeros_like(sin)
    zr = jnp.zeros_like(ones)
    reps = LANES // period
    out = []
    for parts in ((cos, cos, ones), (-sin, zh, zr), (zh, sin, zr)):
        t = jnp.concatenate(parts, axis=-1)
        out.append(jnp.tile(t, (1, 1, reps)).reshape(-1, LANES))
    return out


def _pad_cols(w, new):
    return jnp.pad(w, ((0, 0), (0, new - w.shape[1])))


def kernel(x, c, positions, w_ada, b_ada, g_norm1, g_norm2, w_in, g_q, g_k, a_re, a_im, b_re, b_im,
           c_re, c_im, d_skip, log_dt, w_glu, w_pool, pool_scale, p_a, p_b, p_c, w_gate, b_gate,
           w_out, w_up, conv_w, conv_b, w_down, *, tm=512, tn_merge=512, tn_ffn=512, kb=512, tl=32):
    batch, seq, d = x.shape
    depth = w_ada.shape[0]
    n = batch * seq
    tm = min(tm, seq)
    tl = min(tl, seq)
    topk = min(TOPK_MAX, seq // 4)
    dff_pad = -(-D_FF // tn_ffn) * tn_ffn

    mod_all = _ada(c, w_ada, b_ada).reshape(depth, batch, 6, d)
    tabs_q = _rope_tables(positions, ROT_DIM, HEAD_DIM)
    tabs_i = _rope_tables(positions, IDX_ROT_DIM, IDX_DIM)

    x2 = x.reshape(n, d)
    for l in range(depth):
        mod = mod_all[l]
        gt1 = mod[:, 2:3, :]
        w_in_pad = jnp.concatenate(
            [w_in[l][:, :COL_KW + KW_USED],
             jnp.zeros((d, LANES - KW_USED), F32),
             w_in[l][:, COL_KW + KW_USED:]], axis=1).astype(BF16)
        q, k, v, qi, ki2, kw, u_tm, p = _proj(
            x2, mod, g_norm1[l][None], w_in_pad, g_q[l][None], g_k[l][None], tabs_q, tabs_i,
            batch, seq, tm)
        o_a = _dsa(q, k, v, qi, ki2, kw, batch, seq, topk, kb)
        bm, cm, lre, lim = _s5_params(a_re[l], a_im[l], b_re[l], b_im[l], c_re[l], c_im[l], log_dt[l])
        o_b = _s5(u_tm.reshape(seq * batch, WIDTH_B), bm, cm, lre, lim, d_skip[l][None],
                  w_glu[l].astype(BF16), batch, seq, tl)
        o_c = _pool(p, w_pool[l].astype(BF16), pool_scale[l][None], batch, seq)
        x2 = _merge(x2, mod, gt1, g_norm1[l][None], o_a, o_b.reshape(seq, batch * WIDTH_B), o_c,
                    w_gate[l].astype(BF16), b_gate[l], p_a[l].astype(BF16), p_b[l].astype(BF16),
                    p_c[l].astype(BF16), w_out[l].astype(BF16), seq, tm, tn_merge)
        w_a = _pad_cols(w_up[l][:, :D_FF], dff_pad).astype(BF16)
        w_b = _pad_cols(w_up[l][:, D_FF:], dff_pad).astype(BF16)
        w_d = jnp.pad(w_down[l], ((0, dff_pad - D_FF), (0, 0))).astype(BF16)
        x2 = _ffn(x2, mod, g_norm2[l][None], w_a, w_b, _pad_cols(conv_w[l], dff_pad),
                  _pad_cols(conv_b[l][None], dff_pad), w_d, seq, tm, tn_ffn)
    return x2.reshape(batch, seq, d)
```

```python
import functools
import math

import jax
import jax.numpy as jnp
from jax import lax
from jax.experimental import pallas as pl
from jax.experimental.pallas import tpu as pltpu

F32 = jnp.float32
BF16 = jnp.bfloat16
I32 = jnp.int32

D_MODEL = 2048
DEPTH = 2
CHUNK = 64
EPS = 1e-6
NEG_INF = -1e30
ROPE_THETA = 500000.0

N_HEADS_A = 8
HEAD_DIM = 128
ROT_DIM = HEAD_DIM // 4
N_IDX_HEADS = 8
IDX_DIM = 64
IDX_ROT_DIM = IDX_DIM // 4
N_IDX_PAIRS = N_IDX_HEADS * IDX_DIM // 128
TOPK_MAX = 256
Q_BLOCK = 128
WIDTH_A = N_HEADS_A * HEAD_DIM

WIDTH_B = D_MODEL // 4
SSM_GROUP = 16
N_SSM_GROUPS = WIDTH_B // SSM_GROUP
SSM_STATE = 64
SSM_BUNDLE = 8
N_SSM_BUNDLES = N_SSM_GROUPS // SSM_BUNDLE
BUNDLE_CH = SSM_BUNDLE * SSM_GROUP
BUNDLE_ST = SSM_BUNDLE * SSM_STATE

WIDTH_C = D_MODEL // 4
POOL_WINDOWS = (2, 4, 8, 16)
POOL_GROUP = WIDTH_C // 4

D_FF = 5504
CONV_WIDTH = 3

LANES = 128
INT_MIN = -(2 ** 31)

COL_K = WIDTH_A
COL_V = COL_K + HEAD_DIM
COL_QI = COL_V + HEAD_DIM
COL_KW = COL_QI + N_IDX_HEADS * IDX_DIM
COL_U = COL_KW + LANES
COL_P = COL_U + WIDTH_B
D_IN_PAD = COL_P + WIDTH_C
KW_USED = IDX_DIM + N_IDX_HEADS

VMEM_LIMIT = 56 * 1024 * 1024


def _cparams(sem):
    return pltpu.CompilerParams(dimension_semantics=sem, vmem_limit_bytes=VMEM_LIMIT)


def _norm_mod(x, g, sc, sh):
    ms = jnp.mean(x * x, axis=-1, keepdims=True)
    y = x * lax.rsqrt(ms + EPS)
    return (y * g) * (1.0 + sc) + sh


def _rope(x, c, sa, sb, half):
    return x * c + pltpu.roll(x, LANES - half, 1) * sa + pltpu.roll(x, half, 1) * sb


def _ada_kernel(c_ref, w_ref, b_ref, o_ref):
    c = c_ref[...]
    ca = c * jax.nn.sigmoid(c)
    o_ref[0] = jnp.dot(ca, w_ref[0], preferred_element_type=F32) + b_ref[0]


def _ada(c, w_ada, b_ada, tn=1024):
    depth, d, n = w_ada.shape
    b = c.shape[0]
    return pl.pallas_call(
        _ada_kernel,
        grid=(depth, n // tn),
        in_specs=[
            pl.BlockSpec((b, d), lambda l, j: (0, 0)),
            pl.BlockSpec((1, d, tn), lambda l, j: (l, 0, j)),
            pl.BlockSpec((1, 1, tn), lambda l, j: (l, 0, j)),
        ],
        out_specs=pl.BlockSpec((1, b, tn), lambda l, j: (l, 0, j)),
        out_shape=jax.ShapeDtypeStruct((depth, b, n), F32),
        compiler_params=_cparams(("arbitrary", "arbitrary")),
        name="ada",
    )(c, w_ada, b_ada.reshape(depth, 1, n))


def _proj_kernel(x_ref, mod_ref, g_ref, w_ref, gq_ref, gk_ref, cq_ref, saq_ref, sbq_ref,
                 ci_ref, sai_ref, sbi_ref,
                 h_ref, q_ref, k_ref, v_ref, qi_ref, ki2_ref, kw_ref, u_ref, p_ref):
    h = _norm_mod(x_ref[...], g_ref[...], mod_ref[0, 1:2, :], mod_ref[0, 0:1, :]).astype(BF16)
    h_ref[...] = h

    def mm(c0, width):
        return jnp.dot(h, w_ref[:, c0:c0 + width], preferred_element_type=F32)

    def qk_norm_rope(xh, g):
        ms = jnp.mean(xh * xh, axis=-1, keepdims=True)
        y = xh * lax.rsqrt(ms + EPS) * g
        return _rope(y, cq_ref[...], saq_ref[...], sbq_ref[...], ROT_DIM // 2)

    def store_stacked(ref, slab, idx, nslab):
        for qbl in range(slab.shape[0] // Q_BLOCK):
            r0 = (qbl * nslab + idx) * Q_BLOCK
            ref[r0:r0 + Q_BLOCK, :] = slab[qbl * Q_BLOCK:(qbl + 1) * Q_BLOCK]

    for hd in range(N_HEADS_A):
        qh = qk_norm_rope(mm(hd * HEAD_DIM, HEAD_DIM), gq_ref[...]).astype(BF16)
        store_stacked(q_ref, qh, hd, N_HEADS_A)
    k_ref[...] = qk_norm_rope(mm(COL_K, HEAD_DIM), gk_ref[...]).astype(BF16)
    v_ref[...] = mm(COL_V, HEAD_DIM).astype(BF16)
    for s in range(N_IDX_PAIRS):
        qi = mm(COL_QI + s * LANES, LANES)
        qi = _rope(qi, ci_ref[...], sai_ref[...], sbi_ref[...], IDX_ROT_DIM // 2).astype(BF16)
        store_stacked(qi_ref, qi, s, N_IDX_PAIRS)
    kw = mm(COL_KW, LANES)
    ki = _rope(kw, ci_ref[...], sai_ref[...], sbi_ref[...], IDX_ROT_DIM // 2)
    lane = lax.broadcasted_iota(I32, ki.shape, 1)
    ki_lo = jnp.where(lane < IDX_DIM, ki, 0.0)
    ki2_ref[:, 0:LANES] = ki_lo.astype(BF16)
    ki2_ref[:, LANES:2 * LANES] = pltpu.roll(ki_lo, IDX_DIM, 1).astype(BF16)
    kw_ref[...] = kw * (N_IDX_HEADS ** -0.5 * IDX_DIM ** -0.5)
    u_ref[...] = mm(COL_U, WIDTH_B)
    p_ref[...] = mm(COL_P, WIDTH_C)


def _proj(x2, mod, g1, w_in_pad, g_q, g_k, tabs_q, tabs_i, batch, seq, tm):
    n, d = x2.shape
    tpb = seq // tm
    row = lambda i: (i, 0)
    const = lambda i: (0, 0)
    tab_spec = pl.BlockSpec((tm, LANES), row)
    return pl.pallas_call(
        _proj_kernel,
        grid=(n // tm,),
        in_specs=[
            pl.BlockSpec((tm, d), row),
            pl.BlockSpec((1, 6, d), lambda i: (i // tpb, 0, 0)),
            pl.BlockSpec((1, d), const),
            pl.BlockSpec((d, D_IN_PAD), const),
            pl.BlockSpec((1, HEAD_DIM), const),
            pl.BlockSpec((1, HEAD_DIM), const),
            tab_spec, tab_spec, tab_spec, tab_spec, tab_spec, tab_spec,
        ],
        out_specs=[
            pl.BlockSpec((tm, d), row),
            pl.BlockSpec((tm * N_HEADS_A, HEAD_DIM), row),
            pl.BlockSpec((tm, HEAD_DIM), row),
            pl.BlockSpec((tm, HEAD_DIM), row),
            pl.BlockSpec((tm * N_IDX_PAIRS, LANES), row),
            pl.BlockSpec((tm, 2 * LANES), row),
            pl.BlockSpec((tm, LANES), row),
            pl.BlockSpec((tm, WIDTH_B), lambda i: (i % tpb, i // tpb)),
            pl.BlockSpec((tm, WIDTH_C), row),
        ],
        out_shape=[
            jax.ShapeDtypeStruct((n, d), BF16),
            jax.ShapeDtypeStruct((n * N_HEADS_A, HEAD_DIM), BF16),
            jax.ShapeDtypeStruct((n, HEAD_DIM), BF16),
            jax.ShapeDtypeStruct((n, HEAD_DIM), BF16),
            jax.ShapeDtypeStruct((n * N_IDX_PAIRS, LANES), BF16),
            jax.ShapeDtypeStruct((n, 2 * LANES), BF16),
            jax.ShapeDtypeStruct((n, LANES), F32),
            jax.ShapeDtypeStruct((seq, batch * WIDTH_B), F32),
            jax.ShapeDtypeStruct((n, WIDTH_C), F32),
        ],
        compiler_params=_cparams(("arbitrary",)),
        name="proj",
    )(x2, mod, g1, w_in_pad, g_q, g_k, *tabs_q, *tabs_i)


KEY_BLOCK = 4 * LANES


def _dsa_scores(qi_ref, ki2_ref, kw_ref, s_ref, nqb):
    row = lax.broadcasted_iota(I32, (Q_BLOCK, KEY_BLOCK), 0)
    lane = lax.broadcasted_iota(I32, (Q_BLOCK, KEY_BLOCK), 1)
    chunk_end = (row // CHUNK + 1) * CHUNK
    trans_b = (((1,), (1,)), ((), ()))

    def qb_body(qq, carry):
        r0 = pl.multiple_of(qq * Q_BLOCK, Q_BLOCK)
        w = kw_ref[pl.ds(r0, Q_BLOCK), :]
        wb = [jnp.broadcast_to(w[:, IDX_DIM + hd:IDX_DIM + hd + 1], (Q_BLOCK, KEY_BLOCK))
              for hd in range(N_IDX_HEADS)]
        limit_row = r0 + chunk_end

        def key_body(j, c2):
            k0 = pl.multiple_of(j * KEY_BLOCK, KEY_BLOCK)
            kia = ki2_ref[pl.ds(k0, KEY_BLOCK), 0:LANES]
            kib = ki2_ref[pl.ds(k0, KEY_BLOCK), LANES:2 * LANES]
            qp = qi_ref[pl.ds(pl.multiple_of(r0 * N_IDX_PAIRS, Q_BLOCK), Q_BLOCK * N_IDX_PAIRS), :]
            sa = lax.dot_general(qp, kia, trans_b, preferred_element_type=F32)
            sb = lax.dot_general(qp, kib, trans_b, preferred_element_type=F32)
            sc = jnp.zeros((Q_BLOCK, KEY_BLOCK), F32)
            for pr in range(N_IDX_PAIRS):
                rows = slice(pr * Q_BLOCK, (pr + 1) * Q_BLOCK)
                sc = sc + jnp.maximum(sa[rows], 0.0) * wb[2 * pr] + jnp.maximum(sb[rows], 0.0) * wb[2 * pr + 1]
            adm = ((k0 + lane) < limit_row) & (sc > NEG_INF * 0.5)
            bits = pltpu.bitcast(sc, I32)
            bits = jnp.where(bits == INT_MIN, 0, bits)
            skey = bits ^ ((bits >> 31) & 0x7FFFFFFF)
            s_ref[pl.ds(r0, Q_BLOCK), pl.ds(k0, KEY_BLOCK)] = jnp.where(adm, skey, INT_MIN)
            return c2

        nkb = (r0 + Q_BLOCK + KEY_BLOCK - 1) // KEY_BLOCK
        lax.fori_loop(0, nkb, key_body, 0)
        return carry

    lax.fori_loop(0, nqb, qb_body, 0)


def _dsa_threshold(s_ref, thr_ref, cnt_ref, nqb, topk):
    thr_ref[...] = jnp.full(thr_ref.shape, INT_MIN, I32)

    ones = jnp.ones((LANES, LANES), BF16)

    def bit_body(i, carry):
        bitval = jnp.left_shift(jnp.int32(1), 31 - i)

        def qb_body(qq, c2):
            r0 = pl.multiple_of(qq * Q_BLOCK, Q_BLOCK)
            cand = thr_ref[pl.ds(r0, Q_BLOCK), :] + bitval

            def key_body(j, acc):
                blk = s_ref[pl.ds(r0, Q_BLOCK), pl.ds(pl.multiple_of(j * KEY_BLOCK, KEY_BLOCK), KEY_BLOCK)]
                for s in range(KEY_BLOCK // LANES):
                    acc = acc + jnp.where(blk[:, s * LANES:(s + 1) * LANES] >= cand, 1.0, 0.0)
                return acc

            nkb = (r0 + Q_BLOCK + KEY_BLOCK - 1) // KEY_BLOCK
            acc = lax.fori_loop(0, nkb, key_body, jnp.zeros((Q_BLOCK, LANES), F32))
            cnt_ref[pl.ds(r0, Q_BLOCK), :] = acc.astype(BF16)
            return c2

        lax.fori_loop(0, nqb, qb_body, 0)
        cnt = jnp.dot(cnt_ref[...], ones, preferred_element_type=F32)
        thr = thr_ref[...]
        thr_ref[...] = jnp.where(cnt >= float(topk), thr + bitval, thr)
        return carry

    lax.fori_loop(0, 32, bit_body, 0)


def _dsa_kernel(q_ref, k_ref, v_ref, qi_ref, ki2_ref, kw_ref, o_ref,
                s_ref, thr_ref, cnt_ref, bias_ref, lg_ref, p_ref, acc_ref, *head_refs, topk, nqb):
    m_refs = head_refs[0:N_HEADS_A]
    l_refs = head_refs[N_HEADS_A:2 * N_HEADS_A]
    qb = pl.program_id(1)

    @pl.when(qb == 0)
    def _():
        _dsa_scores(qi_ref, ki2_ref, kw_ref, s_ref, nqb)
        _dsa_threshold(s_ref, thr_ref, cnt_ref, nqb, topk)

    r0 = pl.multiple_of(qb * Q_BLOCK, Q_BLOCK)
    nkb = (r0 + Q_BLOCK + KEY_BLOCK - 1) // KEY_BLOCK
    nslab = KEY_BLOCK // LANES
    trans_b = (((1,), (1,)), ((), ()))
    thr = thr_ref[pl.ds(r0, Q_BLOCK), :]
    thr_kb = jnp.concatenate([thr] * nslab, axis=1)

    def bias_body(j, carry):
        k0 = pl.multiple_of(j * KEY_BLOCK, KEY_BLOCK)
        sk = s_ref[pl.ds(r0, Q_BLOCK), pl.ds(k0, KEY_BLOCK)]
        sel = (sk >= thr_kb) & (sk > INT_MIN)
        bias_ref[:, pl.ds(k0, KEY_BLOCK)] = jnp.where(sel, 0.0, NEG_INF)
        return carry

    lax.fori_loop(0, nkb, bias_body, 0)

    for hd in range(N_HEADS_A):
        m_refs[hd][...] = jnp.full((Q_BLOCK, LANES), NEG_INF, F32)
        l_refs[hd][...] = jnp.zeros((Q_BLOCK, LANES), F32)
    acc_ref[...] = jnp.zeros(acc_ref.shape, F32)

    def max_body(j, carry):
        k0 = pl.multiple_of(j * KEY_BLOCK, KEY_BLOCK)
        s = lax.dot_general(q_ref[...], k_ref[pl.ds(k0, KEY_BLOCK), :], trans_b, preferred_element_type=F32)
        bias = bias_ref[:, pl.ds(k0, KEY_BLOCK)]
        for hd in range(N_HEADS_A):
            rows = slice(hd * Q_BLOCK, (hd + 1) * Q_BLOCK)
            sh = s[rows] + bias
            lg_ref[rows, pl.ds(k0, KEY_BLOCK)] = sh
            m = m_refs[hd][...]
            for sl in range(nslab):
                m = jnp.maximum(m, sh[:, sl * LANES:(sl + 1) * LANES])
            m_refs[hd][...] = m
        return carry

    lax.fori_loop(0, nkb, max_body, 0)
    for hd in range(N_HEADS_A):
        m_refs[hd][...] = jnp.broadcast_to(jnp.max(m_refs[hd][...], axis=1, keepdims=True), (Q_BLOCK, LANES))

    c_exp = HEAD_DIM ** -0.5 * math.log2(math.e)

    def sum_body(j, carry):
        k0 = pl.multiple_of(j * KEY_BLOCK, KEY_BLOCK)
        for hd in range(N_HEADS_A):
            rows = slice(hd * Q_BLOCK, (hd + 1) * Q_BLOCK)
            m = m_refs[hd][...]
            l = l_refs[hd][...]
            for sl in range(nslab):
                p_s = jnp.exp2((lg_ref[rows, pl.ds(k0 + sl * LANES, LANES)] - m) * c_exp)
                l = l + p_s
                p_ref[rows, sl * LANES:(sl + 1) * LANES] = p_s.astype(BF16)
            l_refs[hd][...] = l
        acc_ref[...] += jnp.dot(p_ref[...], v_ref[pl.ds(k0, KEY_BLOCK), :], preferred_element_type=F32)
        return carry

    lax.fori_loop(0, nkb, sum_body, 0)
    for hd in range(N_HEADS_A):
        rows = slice(hd * Q_BLOCK, (hd + 1) * Q_BLOCK)
        l = jnp.sum(l_refs[hd][...], axis=1, keepdims=True)
        o_ref[:, hd * HEAD_DIM:(hd + 1) * HEAD_DIM] = (acc_ref[rows, :] / l).astype(BF16)


def _dsa(q, k, v, qi, ki2, kw, batch, seq, topk):
    assert seq % KEY_BLOCK == 0
    n = k.shape[0]
    nqb = seq // Q_BLOCK
    qrow = lambda b, i: (b * nqb + i, 0)
    brow = lambda b, i: (b, 0)
    head_scratch = [pltpu.VMEM((Q_BLOCK, LANES), F32)] * (2 * N_HEADS_A)
    return pl.pallas_call(
        functools.partial(_dsa_kernel, topk=topk, nqb=nqb),
        grid=(batch, nqb),
        in_specs=[
            pl.BlockSpec((Q_BLOCK * N_HEADS_A, HEAD_DIM), qrow),
            pl.BlockSpec((seq, HEAD_DIM), brow),
            pl.BlockSpec((seq, HEAD_DIM), brow),
            pl.BlockSpec((seq * N_IDX_PAIRS, LANES), brow),
            pl.BlockSpec((seq, 2 * LANES), brow),
            pl.BlockSpec((seq, LANES), brow),
        ],
        out_specs=pl.BlockSpec((Q_BLOCK, WIDTH_A), qrow),
        out_shape=jax.ShapeDtypeStruct((n, WIDTH_A), BF16),
        scratch_shapes=[
            pltpu.VMEM((seq, seq), I32),
            pltpu.VMEM((seq, LANES), I32),
            pltpu.VMEM((seq, LANES), BF16),
            pltpu.VMEM((Q_BLOCK, seq), F32),
            pltpu.VMEM((Q_BLOCK * N_HEADS_A, seq), F32),
            pltpu.VMEM((Q_BLOCK * N_HEADS_A, KEY_BLOCK), BF16),
            pltpu.VMEM((Q_BLOCK * N_HEADS_A, HEAD_DIM), F32),
        ] + head_scratch,
        compiler_params=_cparams(("arbitrary", "arbitrary")),
        name="dsa",
    )(q, k, v, qi, ki2, kw)


def _s5_kernel(u_ref, bm_ref, cm_ref, lre_ref, lim_ref, dsk_ref, wglu_ref, o_ref,
               bu_ref, st_ref, y_ref, *, batch, tl):
    @pl.when(pl.program_id(0) == 0)
    def _():
        st_ref[...] = jnp.zeros(st_ref.shape, F32)

    for gb in range(N_SSM_BUNDLES):
        ch = slice(gb * BUNDLE_CH, (gb + 1) * BUNDLE_CH)
        u_g = u_ref[:, ch]
        bu_ref[...] = jnp.dot(u_g.astype(BF16), bm_ref[gb], preferred_element_type=F32)
        lam_re = jnp.broadcast_to(lre_ref[gb], (batch, BUNDLE_ST))
        lam_im = jnp.broadcast_to(lim_ref[gb], (batch, BUNDLE_ST))

        def step(t, carry):
            x_re, x_im = carry
            r0 = pl.multiple_of(t * batch, batch)
            n_re = lam_re * x_re - lam_im * x_im + bu_ref[pl.ds(r0, batch), 0:BUNDLE_ST]
            n_im = lam_re * x_im + lam_im * x_re + bu_ref[pl.ds(r0, batch), BUNDLE_ST:2 * BUNDLE_ST]
            bu_ref[pl.ds(r0, batch), 0:BUNDLE_ST] = n_re
            bu_ref[pl.ds(r0, batch), BUNDLE_ST:2 * BUNDLE_ST] = n_im
            return n_re, n_im

        x_re, x_im = lax.fori_loop(
            0, tl, step, (st_ref[gb, :, 0:BUNDLE_ST], st_ref[gb, :, BUNDLE_ST:2 * BUNDLE_ST]))
        st_ref[gb, :, 0:BUNDLE_ST] = x_re
        st_ref[gb, :, BUNDLE_ST:2 * BUNDLE_ST] = x_im
        y = jnp.dot(bu_ref[...].astype(BF16), cm_ref[gb], preferred_element_type=F32)
        y_ref[:, ch] = y + dsk_ref[:, ch] * u_g

    y = jax.nn.gelu(y_ref[...])
    gl = jnp.dot(y.astype(BF16), wglu_ref[...], preferred_element_type=F32)
    o_ref[...] = (y * jax.nn.sigmoid(gl)).astype(BF16)


def _s5_params(a_re, a_im, b_re, b_im, c_re, c_im, log_dt):
    dt = jnp.exp(log_dt)[:, None]
    mag = jnp.exp(a_re * dt)
    lb_re = mag * jnp.cos(a_im * dt)
    lb_im = mag * jnp.sin(a_im * dt)
    nr, ni = lb_re - 1.0, lb_im
    den = a_re * a_re + a_im * a_im
    q_re = (nr * a_re + ni * a_im) / den
    q_im = (ni * a_re - nr * a_im) / den
    bb_re = q_re[..., None] * b_re - q_im[..., None] * b_im
    bb_im = q_re[..., None] * b_im + q_im[..., None] * b_re
    eye = jnp.eye(SSM_BUNDLE, dtype=F32)
    nb = N_SSM_BUNDLES

    def pack_b(m):
        m = m.reshape(nb, SSM_BUNDLE, SSM_STATE, SSM_GROUP)
        return jnp.einsum('bgpi,gh->bgihp', m, eye).reshape(nb, BUNDLE_CH, BUNDLE_ST)

    def pack_c(m):
        m = m.reshape(nb, SSM_BUNDLE, SSM_GROUP, SSM_STATE)
        return jnp.einsum('bgop,gh->bgpho', m, eye).reshape(nb, BUNDLE_ST, BUNDLE_CH)

    bm = jnp.concatenate([pack_b(bb_re), pack_b(bb_im)], axis=2).astype(BF16)
    cm = jnp.concatenate([pack_c(c_re), -pack_c(c_im)], axis=1).astype(BF16)
    return bm, cm, lb_re.reshape(nb, 1, BUNDLE_ST), lb_im.reshape(nb, 1, BUNDLE_ST)


def _s5(u_tm, bm, cm, lre, lim, d_skip, w_glu, batch, seq, tl):
    rows = tl * batch
    const3 = lambda t: (0, 0, 0)
    const2 = lambda t: (0, 0)
    return pl.pallas_call(
        functools.partial(_s5_kernel, batch=batch, tl=tl),
        grid=(seq // tl,),
        in_specs=[
            pl.BlockSpec((rows, WIDTH_B), lambda t: (t, 0)),
            pl.BlockSpec(bm.shape, const3),
            pl.BlockSpec(cm.shape, const3),
            pl.BlockSpec(lre.shape, const3),
            pl.BlockSpec(lim.shape, const3),
            pl.BlockSpec((1, WIDTH_B), const2),
            pl.BlockSpec((WIDTH_B, WIDTH_B), const2),
        ],
        out_specs=pl.BlockSpec((rows, WIDTH_B), lambda t: (t, 0)),
        out_shape=jax.ShapeDtypeStruct((seq * batch, WIDTH_B), BF16),
        scratch_shapes=[
            pltpu.VMEM((rows, 2 * BUNDLE_ST), F32),
            pltpu.VMEM((N_SSM_BUNDLES, batch, 2 * BUNDLE_ST), F32),
            pltpu.VMEM((rows, WIDTH_B), F32),
        ],
        compiler_params=_cparams(("arbitrary",)),
        name="s5",
    )(u_tm, bm, cm, lre, lim, d_skip, w_glu)


def _pool_kernel(p_ref, w_ref, sc_ref, o_ref):
    seq = p_ref.shape[0]
    t = lax.broadcasted_iota(I32, (seq, POOL_GROUP), 0)
    t1 = (t + 1).astype(F32)
    for g, win in enumerate(POOL_WINDOWS):
        ch = slice(g * POOL_GROUP, (g + 1) * POOL_GROUP)
        x = p_ref[:, ch]
        s = x
        sh = 1
        while sh < win:
            s = s + jnp.where(t >= sh, pltpu.roll(s, sh, 0), 0.0)
            sh *= 2
        pooled = s / jnp.minimum(t1, float(win)) - x
        y = jnp.dot(pooled.astype(BF16), w_ref[g], preferred_element_type=F32)
        o_ref[:, ch] = (y * sc_ref[:, ch]).astype(BF16)


def _pool(p, w_pool, pool_scale, batch, seq):
    return pl.pallas_call(
        _pool_kernel,
        grid=(batch,),
        in_specs=[
            pl.BlockSpec((seq, WIDTH_C), lambda b: (b, 0)),
            pl.BlockSpec(w_pool.shape, lambda b: (0, 0, 0)),
            pl.BlockSpec((1, WIDTH_C), lambda b: (0, 0)),
        ],
        out_specs=pl.BlockSpec((seq, WIDTH_C), lambda b: (b, 0)),
        out_shape=jax.ShapeDtypeStruct(p.shape, BF16),
        compiler_params=_cparams(("arbitrary",)),
        name="pool",
    )(p, w_pool, pool_scale)


def _merge_kernel(h_ref, xc_ref, oa_ref, ob_ref, oc_ref, wg_ref, bg_ref,
                  pa_ref, pb_ref, pc_ref, wo_ref, gt_ref, o_ref, mg_ref, *, nj, tn):
    j = pl.program_id(1)

    @pl.when(j < nj)
    def _():
        h = h_ref[...]
        branches = ((oa_ref, pa_ref), (ob_ref, pb_ref), (oc_ref, pc_ref))
        merged = None
        for i, (o_r, p_r) in enumerate(branches):
            gate = jax.nn.sigmoid(jnp.dot(h, wg_ref[i], preferred_element_type=F32) + bg_ref[i:i + 1, :])
            term = gate * jnp.dot(o_r[...], p_r[...], preferred_element_type=F32)
            merged = term if merged is None else merged + term
        c0 = pl.multiple_of(j * tn, tn)
        mg_ref[:, pl.ds(c0, tn)] = merged.astype(BF16)

    @pl.when(j >= nj)
    def _():
        mix = jnp.dot(mg_ref[...], wo_ref[...], preferred_element_type=F32)
        o_ref[...] = xc_ref[...] + gt_ref[0] * mix


def _merge(x2, h, gt, o_a, o_b_tm, o_c, wg, bg, p_a, p_b, p_c, w_out, seq, tm, tn):
    n, d = x2.shape
    nj = d // tn
    tpb = seq // tm
    row = lambda i, j: (i, 0)
    ph1 = lambda i, j: (0, jnp.minimum(j, nj - 1))
    ph2 = lambda i, j: (0, jnp.maximum(j - nj, 0))
    return pl.pallas_call(
        functools.partial(_merge_kernel, nj=nj, tn=tn),
        grid=(n // tm, 2 * nj),
        in_specs=[
            pl.BlockSpec((tm, d), row),
            pl.BlockSpec((tm, tn), lambda i, j: (i, jnp.maximum(j - nj, 0))),
            pl.BlockSpec((tm, WIDTH_A), row),
            pl.BlockSpec((tm, WIDTH_B), lambda i, j: (i % tpb, i // tpb)),
            pl.BlockSpec((tm, WIDTH_C), row),
            pl.BlockSpec((3, d, tn), lambda i, j: (0, 0, jnp.minimum(j, nj - 1))),
            pl.BlockSpec((3, tn), ph1),
            pl.BlockSpec((WIDTH_A, tn), ph1),
            pl.BlockSpec((WIDTH_B, tn), ph1),
            pl.BlockSpec((WIDTH_C, tn), ph1),
            pl.BlockSpec((d, tn), ph2),
            pl.BlockSpec((1, 1, tn), lambda i, j: (i // tpb, 0, jnp.maximum(j - nj, 0))),
        ],
        out_specs=pl.BlockSpec((tm, tn), lambda i, j: (i, jnp.maximum(j - nj, 0))),
        out_shape=jax.ShapeDtypeStruct((n, d), F32),
        scratch_shapes=[pltpu.VMEM((tm, d), BF16)],
        compiler_params=_cparams(("arbitrary", "arbitrary")),
        name="merge",
    )(h, x2, o_a, o_b_tm, o_c, wg, bg, p_a, p_b, p_c, w_out, gt)


FFN_HALO = 16


def _ffn_kernel(x_ref, xh_ref, mod_ref, g_ref, wa_ref, wb_ref, cw_ref, cb_ref, wd_ref, o_ref,
                h_ref, acc_ref, *, tm, tpb):
    i = pl.program_id(0)
    j = pl.program_id(1)
    nj = pl.num_programs(1)

    @pl.when(j == 0)
    def _():
        g, sc, sh = g_ref[...], mod_ref[0, 4:5, :], mod_ref[0, 3:4, :]
        halo = _norm_mod(xh_ref[...], g, sc, sh)
        h_ref[0:FFN_HALO, :] = jnp.where(i % tpb == 0, 0.0, halo).astype(BF16)
        h_ref[FFN_HALO:, :] = _norm_mod(x_ref[...], g, sc, sh).astype(BF16)
        acc_ref[...] = jnp.zeros(acc_ref.shape, F32)

    a = jnp.dot(h_ref[...], wa_ref[...], preferred_element_type=F32)
    b = jnp.dot(h_ref[FFN_HALO:, :], wb_ref[...], preferred_element_type=F32)
    a_conv = cb_ref[...] + a[FFN_HALO - 2:FFN_HALO - 2 + tm] * cw_ref[0:1, :]
    a_conv = a_conv + a[FFN_HALO - 1:FFN_HALO - 1 + tm] * cw_ref[1:2, :]
    a_conv = a_conv + a[FFN_HALO:] * cw_ref[2:3, :]
    act = (a_conv * jax.nn.sigmoid(a_conv)) * b
    acc_ref[...] += jnp.dot(act.astype(BF16), wd_ref[...], preferred_element_type=F32)

    @pl.when(j == nj - 1)
    def _():
        o_ref[...] = x_ref[...] + mod_ref[0, 5:6, :] * acc_ref[...]


def _ffn(x2, mod, g2, w_a, w_b, conv_w, conv_b, w_down, seq, tm, tn):
    n, d = x2.shape
    dff = w_a.shape[1]
    tpb = seq // tm
    hb = tm // FFN_HALO
    return pl.pallas_call(
        functools.partial(_ffn_kernel, tm=tm, tpb=tpb),
        grid=(n // tm, dff // tn),
        in_specs=[
            pl.BlockSpec((tm, d), lambda i, j: (i, 0)),
            pl.BlockSpec((FFN_HALO, d), lambda i, j: (jnp.maximum(i * hb - 1, 0), 0)),
            pl.BlockSpec((1, 6, d), lambda i, j: (i // tpb, 0, 0)),
            pl.BlockSpec((1, d), lambda i, j: (0, 0)),
            pl.BlockSpec((d, tn), lambda i, j: (0, j)),
            pl.BlockSpec((d, tn), lambda i, j: (0, j)),
            pl.BlockSpec((CONV_WIDTH, tn), lambda i, j: (0, j)),
            pl.BlockSpec((1, tn), lambda i, j: (0, j)),
            pl.BlockSpec((tn, d), lambda i, j: (j, 0)),
        ],
        out_specs=pl.BlockSpec((tm, d), lambda i, j: (i, 0)),
        out_shape=jax.ShapeDtypeStruct((n, d), F32),
        scratch_shapes=[pltpu.VMEM((tm + FFN_HALO, d), BF16), pltpu.VMEM((tm, d), F32)],
        compiler_params=_cparams(("arbitrary", "arbitrary")),
        name="ffn",
    )(x2, x2, mod, g2, w_a, w_b, conv_w, conv_b, w_down)


def _rope_tables(positions, rot_dim, period):
    half = rot_dim // 2
    inv_freq = ROPE_THETA ** (-jnp.arange(half, dtype=F32) * (2.0 / rot_dim))
    ang = positions.astype(F32)[..., None] * inv_freq
    cos, sin = jnp.cos(ang), jnp.sin(ang)
    rest = period - 2 * half
    ones = jnp.ones(cos.shape[:-1] + (rest,), F32)
    zh = jnp.zeros_like(sin)
    zr = jnp.zeros_like(ones)
    reps = LANES // period
    out = []
    for parts in ((cos, cos, ones), (-sin, zh, zr), (zh, sin, zr)):
        t = jnp.concatenate(parts, axis=-1)
        out.append(jnp.tile(t, (1, 1, reps)).reshape(-1, LANES))
    return out


def _pad_cols(w, new):
    return jnp.pad(w, ((0, 0), (0, new - w.shape[1])))


def kernel(x, c, positions, w_ada, b_ada, g_norm1, g_norm2, w_in, g_q, g_k, a_re, a_im, b_re, b_im,
           c_re, c_im, d_skip, log_dt, w_glu, w_pool, pool_scale, p_a, p_b, p_c, w_gate, b_gate,
           w_out, w_up, conv_w, conv_b, w_down, *, tm=512, tm_merge=1024, tn_merge=256, tn_ffn=512, tl=32):
    batch, seq, d = x.shape
    depth = w_ada.shape[0]
    n = batch * seq
    tm = min(tm, seq)
    tm_merge = min(tm_merge, seq)
    tl = min(tl, seq)
    topk = min(TOPK_MAX, seq // 4)
    dff_pad = -(-D_FF // tn_ffn) * tn_ffn

    mod_all = _ada(c, w_ada, b_ada).reshape(depth, batch, 6, d)
    tabs_q = _rope_tables(positions, ROT_DIM, HEAD_DIM)
    tabs_i = _rope_tables(positions, IDX_ROT_DIM, IDX_DIM)

    x2 = x.reshape(n, d)
    for l in range(depth):
        mod = mod_all[l]
        gt1 = mod[:, 2:3, :]
        w_in_pad = jnp.concatenate(
            [w_in[l][:, :COL_KW + KW_USED],
             jnp.zeros((d, LANES - KW_USED), F32),
             w_in[l][:, COL_KW + KW_USED:]], axis=1).astype(BF16)
        h, q, k, v, qi, ki2, kw, u_tm, p = _proj(
            x2, mod, g_norm1[l][None], w_in_pad, g_q[l][None], g_k[l][None], tabs_q, tabs_i,
            batch, seq, tm)
        o_a = _dsa(q, k, v, qi, ki2, kw, batch, seq, topk)
        bm, cm, lre, lim = _s5_params(a_re[l], a_im[l], b_re[l], b_im[l], c_re[l], c_im[l], log_dt[l])
        o_b = _s5(u_tm.reshape(seq * batch, WIDTH_B), bm, cm, lre, lim, d_skip[l][None],
                  w_glu[l].astype(BF16), batch, seq, tl)
        o_c = _pool(p, w_pool[l].astype(BF16), pool_scale[l][None], batch, seq)
        x2 = _merge(x2, h, gt1, o_a, o_b.reshape(seq, batch * WIDTH_B), o_c,
                    w_gate[l].astype(BF16), b_gate[l], p_a[l].astype(BF16), p_b[l].astype(BF16),
                    p_c[l].astype(BF16), w_out[l].astype(BF16), seq, tm_merge, tn_merge)
        w_a = _pad_cols(w_up[l][:, :D_FF], dff_pad).astype(BF16)
        w_b = _pad_cols(w_up[l][:, D_FF:], dff_pad).astype(BF16)
        w_d = jnp.pad(w_down[l], ((0, dff_pad - D_FF), (0, 0))).astype(BF16)
        x2 = _ffn(x2, mod, g_norm2[l][None], w_a, w_b, _pad_cols(conv_w[l], dff_pad),
                  _pad_cols(conv_b[l][None], dff_pad), w_d, seq, tm, tn_ffn)
    return x2.reshape(batch, seq, d)
```

```python
import functools
import math

import jax
import jax.numpy as jnp
from jax import lax
from jax.experimental import pallas as pl
from jax.experimental.pallas import tpu as pltpu

F32 = jnp.float32
BF16 = jnp.bfloat16
I32 = jnp.int32

D_MODEL = 2048
DEPTH = 2
CHUNK = 64
EPS = 1e-6
NEG_INF = -1e30
ROPE_THETA = 500000.0

N_HEADS_A = 8
HEAD_DIM = 128
ROT_DIM = HEAD_DIM // 4
N_IDX_HEADS = 8
IDX_DIM = 64
IDX_ROT_DIM = IDX_DIM // 4
N_IDX_PAIRS = N_IDX_HEADS * IDX_DIM // 128
TOPK_MAX = 256
Q_BLOCK = 128
WIDTH_A = N_HEADS_A * HEAD_DIM

WIDTH_B = D_MODEL // 4
SSM_GROUP = 16
N_SSM_GROUPS = WIDTH_B // SSM_GROUP
SSM_STATE = 64
SSM_BUNDLE = 8
N_SSM_BUNDLES = N_SSM_GROUPS // SSM_BUNDLE
BUNDLE_CH = SSM_BUNDLE * SSM_GROUP
BUNDLE_ST = SSM_BUNDLE * SSM_STATE

WIDTH_C = D_MODEL // 4
POOL_WINDOWS = (2, 4, 8, 16)
POOL_GROUP = WIDTH_C // 4

D_FF = 5504
CONV_WIDTH = 3

LANES = 128
INT_MIN = -(2 ** 31)

COL_K = WIDTH_A
COL_V = COL_K + HEAD_DIM
COL_QI = COL_V + HEAD_DIM
COL_KW = COL_QI + N_IDX_HEADS * IDX_DIM
COL_U = COL_KW + LANES
COL_P = COL_U + WIDTH_B
D_IN_PAD = COL_P + WIDTH_C
KW_USED = IDX_DIM + N_IDX_HEADS

VMEM_LIMIT = 56 * 1024 * 1024


def _cparams(sem):
    return pltpu.CompilerParams(dimension_semantics=sem, vmem_limit_bytes=VMEM_LIMIT)


def _norm_mod(x, g, sc, sh):
    ms = jnp.mean(x * x, axis=-1, keepdims=True)
    y = x * lax.rsqrt(ms + EPS)
    return (y * g) * (1.0 + sc) + sh


def _rope(x, c, sa, sb, half):
    return x * c + pltpu.roll(x, LANES - half, 1) * sa + pltpu.roll(x, half, 1) * sb


def _ada_kernel(c_ref, w_ref, b_ref, o_ref):
    c = c_ref[...]
    ca = c * jax.nn.sigmoid(c)
    o_ref[0] = jnp.dot(ca, w_ref[0], preferred_element_type=F32) + b_ref[0]


def _ada(c, w_ada, b_ada, tn=1024):
    depth, d, n = w_ada.shape
    b = c.shape[0]
    return pl.pallas_call(
        _ada_kernel,
        grid=(depth, n // tn),
        in_specs=[
            pl.BlockSpec((b, d), lambda l, j: (0, 0)),
            pl.BlockSpec((1, d, tn), lambda l, j: (l, 0, j)),
            pl.BlockSpec((1, 1, tn), lambda l, j: (l, 0, j)),
        ],
        out_specs=pl.BlockSpec((1, b, tn), lambda l, j: (l, 0, j)),
        out_shape=jax.ShapeDtypeStruct((depth, b, n), F32),
        compiler_params=_cparams(("arbitrary", "arbitrary")),
        name="ada",
    )(c, w_ada, b_ada.reshape(depth, 1, n))


def _proj_kernel(x_ref, mod_ref, g_ref, w_ref, gq_ref, gk_ref, cq_ref, saq_ref, sbq_ref,
                 ci_ref, sai_ref, sbi_ref,
                 h_ref, q_ref, k_ref, v_ref, qi_ref, ki2_ref, kw_ref, u_ref, p_ref):
    h = _norm_mod(x_ref[...], g_ref[...], mod_ref[0, 1:2, :], mod_ref[0, 0:1, :]).astype(BF16)
    h_ref[...] = h

    def mm(c0, width):
        return jnp.dot(h, w_ref[:, c0:c0 + width], preferred_element_type=F32)

    def qk_norm_rope(xh, g):
        ms = jnp.mean(xh * xh, axis=-1, keepdims=True)
        y = xh * lax.rsqrt(ms + EPS) * g
        return _rope(y, cq_ref[...], saq_ref[...], sbq_ref[...], ROT_DIM // 2)

    def store_stacked(ref, slab, idx, nslab):
        for qbl in range(slab.shape[0] // Q_BLOCK):
            r0 = (qbl * nslab + idx) * Q_BLOCK
            ref[r0:r0 + Q_BLOCK, :] = slab[qbl * Q_BLOCK:(qbl + 1) * Q_BLOCK]

    q_all = mm(0, WIDTH_A)
    for hd in range(N_HEADS_A):
        qh = qk_norm_rope(q_all[:, hd * HEAD_DIM:(hd + 1) * HEAD_DIM], gq_ref[...]).astype(BF16)
        store_stacked(q_ref, qh, hd, N_HEADS_A)
    kv = mm(COL_K, 2 * HEAD_DIM)
    k_ref[...] = qk_norm_rope(kv[:, 0:HEAD_DIM], gk_ref[...]).astype(BF16)
    v_ref[...] = kv[:, HEAD_DIM:2 * HEAD_DIM].astype(BF16)
    qi_all = mm(COL_QI, N_IDX_PAIRS * LANES)
    for s in range(N_IDX_PAIRS):
        qi = qi_all[:, s * LANES:(s + 1) * LANES]
        qi = _rope(qi, ci_ref[...], sai_ref[...], sbi_ref[...], IDX_ROT_DIM // 2).astype(BF16)
        store_stacked(qi_ref, qi, s, N_IDX_PAIRS)
    rest = mm(COL_KW, LANES + WIDTH_B + WIDTH_C)
    kw = rest[:, 0:LANES]
    ki = _rope(kw, ci_ref[...], sai_ref[...], sbi_ref[...], IDX_ROT_DIM // 2)
    lane = lax.broadcasted_iota(I32, ki.shape, 1)
    ki_lo = jnp.where(lane < IDX_DIM, ki, 0.0)
    ki2_ref[:, 0:LANES] = ki_lo.astype(BF16)
    ki2_ref[:, LANES:2 * LANES] = pltpu.roll(ki_lo, IDX_DIM, 1).astype(BF16)
    kw_ref[...] = kw * (N_IDX_HEADS ** -0.5 * IDX_DIM ** -0.5)
    u_ref[...] = rest[:, LANES:LANES + WIDTH_B]
    p_ref[...] = rest[:, LANES + WIDTH_B:]


def _proj(x2, mod, g1, w_in_pad, g_q, g_k, tabs_q, tabs_i, batch, seq, tm):
    n, d = x2.shape
    tpb = seq // tm
    row = lambda i: (i, 0)
    const = lambda i: (0, 0)
    tab_spec = pl.BlockSpec((tm, LANES), row)
    return pl.pallas_call(
        _proj_kernel,
        grid=(n // tm,),
        in_specs=[
            pl.BlockSpec((tm, d), row),
            pl.BlockSpec((1, 6, d), lambda i: (i // tpb, 0, 0)),
            pl.BlockSpec((1, d), const),
            pl.BlockSpec((d, D_IN_PAD), const),
            pl.BlockSpec((1, HEAD_DIM), const),
            pl.BlockSpec((1, HEAD_DIM), const),
            tab_spec, tab_spec, tab_spec, tab_spec, tab_spec, tab_spec,
        ],
        out_specs=[
            pl.BlockSpec((tm, d), row),
            pl.BlockSpec((tm * N_HEADS_A, HEAD_DIM), row),
            pl.BlockSpec((tm, HEAD_DIM), row),
            pl.BlockSpec((tm, HEAD_DIM), row),
            pl.BlockSpec((tm * N_IDX_PAIRS, LANES), row),
            pl.BlockSpec((tm, 2 * LANES), row),
            pl.BlockSpec((tm, LANES), row),
            pl.BlockSpec((tm, WIDTH_B), lambda i: (i % tpb, i // tpb)),
            pl.BlockSpec((tm, WIDTH_C), row),
        ],
        out_shape=[
            jax.ShapeDtypeStruct((n, d), BF16),
            jax.ShapeDtypeStruct((n * N_HEADS_A, HEAD_DIM), BF16),
            jax.ShapeDtypeStruct((n, HEAD_DIM), BF16),
            jax.ShapeDtypeStruct((n, HEAD_DIM), BF16),
            jax.ShapeDtypeStruct((n * N_IDX_PAIRS, LANES), BF16),
            jax.ShapeDtypeStruct((n, 2 * LANES), BF16),
            jax.ShapeDtypeStruct((n, LANES), F32),
            jax.ShapeDtypeStruct((seq, batch * WIDTH_B), F32),
            jax.ShapeDtypeStruct((n, WIDTH_C), F32),
        ],
        compiler_params=_cparams(("arbitrary",)),
        name="proj",
    )(x2, mod, g1, w_in_pad, g_q, g_k, *tabs_q, *tabs_i)


KEY_BLOCK = 4 * LANES


def _dsa_scores(qi_ref, ki2_ref, kw_ref, s_ref, nqb):
    row = lax.broadcasted_iota(I32, (Q_BLOCK, KEY_BLOCK), 0)
    lane = lax.broadcasted_iota(I32, (Q_BLOCK, KEY_BLOCK), 1)
    chunk_end = (row // CHUNK + 1) * CHUNK
    trans_b = (((1,), (1,)), ((), ()))

    def qb_body(qq, carry):
        r0 = pl.multiple_of(qq * Q_BLOCK, Q_BLOCK)
        w = kw_ref[pl.ds(r0, Q_BLOCK), :]
        wb = [jnp.broadcast_to(w[:, IDX_DIM + hd:IDX_DIM + hd + 1], (Q_BLOCK, KEY_BLOCK))
              for hd in range(N_IDX_HEADS)]
        limit_row = r0 + chunk_end

        def key_body(j, c2):
            k0 = pl.multiple_of(j * KEY_BLOCK, KEY_BLOCK)
            kia = ki2_ref[pl.ds(k0, KEY_BLOCK), 0:LANES]
            kib = ki2_ref[pl.ds(k0, KEY_BLOCK), LANES:2 * LANES]
            qp = qi_ref[pl.ds(pl.multiple_of(r0 * N_IDX_PAIRS, Q_BLOCK), Q_BLOCK * N_IDX_PAIRS), :]
            sa = lax.dot_general(qp, kia, trans_b, preferred_element_type=F32)
            sb = lax.dot_general(qp, kib, trans_b, preferred_element_type=F32)
            sc = jnp.zeros((Q_BLOCK, KEY_BLOCK), F32)
            for pr in range(N_IDX_PAIRS):
                rows = slice(pr * Q_BLOCK, (pr + 1) * Q_BLOCK)
                sc = sc + jnp.maximum(sa[rows], 0.0) * wb[2 * pr] + jnp.maximum(sb[rows], 0.0) * wb[2 * pr + 1]
            adm = ((k0 + lane) < limit_row) & (sc > NEG_INF * 0.5)
            bits = pltpu.bitcast(sc, I32)
            bits = jnp.where(bits == INT_MIN, 0, bits)
            skey = bits ^ ((bits >> 31) & 0x7FFFFFFF)
            s_ref[pl.ds(r0, Q_BLOCK), pl.ds(k0, KEY_BLOCK)] = jnp.where(adm, skey, INT_MIN)
            return c2

        nkb = (r0 + Q_BLOCK + KEY_BLOCK - 1) // KEY_BLOCK
        lax.fori_loop(0, nkb, key_body, 0)
        return carry

    lax.fori_loop(0, nqb, qb_body, 0)


def _dsa_threshold(s_ref, thr_ref, cnt_ref, nqb, topk):
    thr_ref[...] = jnp.full(thr_ref.shape, INT_MIN, I32)

    ones = jnp.ones((LANES, LANES), BF16)

    def bit_body(i, carry):
        bitval = jnp.left_shift(jnp.int32(1), 31 - i)

        for qq in range(nqb):
            rows = slice(qq * Q_BLOCK, (qq + 1) * Q_BLOCK)
            cand = thr_ref[rows, :] + bitval
            acc = jnp.zeros((Q_BLOCK, LANES), F32)
            for s in range(qq + 1):
                acc = acc + jnp.where(s_ref[rows, s * LANES:(s + 1) * LANES] >= cand, 1.0, 0.0)
            cnt_ref[rows, :] = acc.astype(BF16)
        cnt = jnp.dot(cnt_ref[...], ones, preferred_element_type=F32)
        thr = thr_ref[...]
        thr_ref[...] = jnp.where(cnt >= float(topk), thr + bitval, thr)
        return carry

    lax.fori_loop(0, 32, bit_body, 0)


def _dsa_kernel(q_ref, k_ref, v_ref, qi_ref, ki2_ref, kw_ref, o_ref,
                s_ref, thr_ref, cnt_ref, bias_ref, lg_ref, p_ref, acc_ref, *head_refs, topk, nqb):
    m_refs = head_refs[0:N_HEADS_A]
    l_refs = head_refs[N_HEADS_A:2 * N_HEADS_A]
    qb = pl.program_id(1)

    @pl.when(qb == 0)
    def _():
        _dsa_scores(qi_ref, ki2_ref, kw_ref, s_ref, nqb)
        _dsa_threshold(s_ref, thr_ref, cnt_ref, nqb, topk)

    r0 = pl.multiple_of(qb * Q_BLOCK, Q_BLOCK)
    nkb = (r0 + Q_BLOCK + KEY_BLOCK - 1) // KEY_BLOCK
    nslab = KEY_BLOCK // LANES
    trans_b = (((1,), (1,)), ((), ()))
    thr = thr_ref[pl.ds(r0, Q_BLOCK), :]
    thr_kb = jnp.concatenate([thr] * nslab, axis=1)

    def bias_body(j, carry):
        k0 = pl.multiple_of(j * KEY_BLOCK, KEY_BLOCK)
        sk = s_ref[pl.ds(r0, Q_BLOCK), pl.ds(k0, KEY_BLOCK)]
        sel = (sk >= thr_kb) & (sk > INT_MIN)
        bias_ref[:, pl.ds(k0, KEY_BLOCK)] = jnp.where(sel, 0.0, NEG_INF)
        return carry

    lax.fori_loop(0, nkb, bias_body, 0)

    for hd in range(N_HEADS_A):
        m_refs[hd][...] = jnp.full((Q_BLOCK, LANES), NEG_INF, F32)
        l_refs[hd][...] = jnp.zeros((Q_BLOCK, LANES), F32)
    acc_ref[...] = jnp.zeros(acc_ref.shape, F32)

    def max_body(j, carry):
        k0 = pl.multiple_of(j * KEY_BLOCK, KEY_BLOCK)
        s = lax.dot_general(q_ref[...], k_ref[pl.ds(k0, KEY_BLOCK), :], trans_b, preferred_element_type=F32)
        bias = bias_ref[:, pl.ds(k0, KEY_BLOCK)]
        for hd in range(N_HEADS_A):
            rows = slice(hd * Q_BLOCK, (hd + 1) * Q_BLOCK)
            sh = s[rows] + bias
            lg_ref[rows, pl.ds(k0, KEY_BLOCK)] = sh
            m = m_refs[hd][...]
            for sl in range(nslab):
                m = jnp.maximum(m, sh[:, sl * LANES:(sl + 1) * LANES])
            m_refs[hd][...] = m
        return carry

    lax.fori_loop(0, nkb, max_body, 0)
    for hd in range(N_HEADS_A):
        m_refs[hd][...] = jnp.broadcast_to(jnp.max(m_refs[hd][...], axis=1, keepdims=True), (Q_BLOCK, LANES))

    c_exp = HEAD_DIM ** -0.5 * math.log2(math.e)

    def sum_body(j, carry):
        k0 = pl.multiple_of(j * KEY_BLOCK, KEY_BLOCK)
        for hd in range(N_HEADS_A):
            rows = slice(hd * Q_BLOCK, (hd + 1) * Q_BLOCK)
            m = m_refs[hd][...]
            l = l_refs[hd][...]
            for sl in range(nslab):
                p_s = jnp.exp2((lg_ref[rows, pl.ds(k0 + sl * LANES, LANES)] - m) * c_exp)
                l = l + p_s
                p_ref[rows, sl * LANES:(sl + 1) * LANES] = p_s.astype(BF16)
            l_refs[hd][...] = l
        acc_ref[...] += jnp.dot(p_ref[...], v_ref[pl.ds(k0, KEY_BLOCK), :], preferred_element_type=F32)
        return carry

    lax.fori_loop(0, nkb, sum_body, 0)
    for hd in range(N_HEADS_A):
        rows = slice(hd * Q_BLOCK, (hd + 1) * Q_BLOCK)
        l = jnp.sum(l_refs[hd][...], axis=1, keepdims=True)
        o_ref[:, hd * HEAD_DIM:(hd + 1) * HEAD_DIM] = (acc_ref[rows, :] / l).astype(BF16)


def _dsa(q, k, v, qi, ki2, kw, batch, seq, topk):
    assert seq % KEY_BLOCK == 0
    n = k.shape[0]
    nqb = seq // Q_BLOCK
    qrow = lambda b, i: (b * nqb + i, 0)
    brow = lambda b, i: (b, 0)
    head_scratch = [pltpu.VMEM((Q_BLOCK, LANES), F32)] * (2 * N_HEADS_A)
    return pl.pallas_call(
        functools.partial(_dsa_kernel, topk=topk, nqb=nqb),
        grid=(batch, nqb),
        in_specs=[
            pl.BlockSpec((Q_BLOCK * N_HEADS_A, HEAD_DIM), qrow),
            pl.BlockSpec((seq, HEAD_DIM), brow),
            pl.BlockSpec((seq, HEAD_DIM), brow),
            pl.BlockSpec((seq * N_IDX_PAIRS, LANES), brow),
            pl.BlockSpec((seq, 2 * LANES), brow),
            pl.BlockSpec((seq, LANES), brow),
        ],
        out_specs=pl.BlockSpec((Q_BLOCK, WIDTH_A), qrow),
        out_shape=jax.ShapeDtypeStruct((n, WIDTH_A), BF16),
        scratch_shapes=[
            pltpu.VMEM((seq, seq), I32),
            pltpu.VMEM((seq, LANES), I32),
            pltpu.VMEM((seq, LANES), BF16),
            pltpu.VMEM((Q_BLOCK, seq), F32),
            pltpu.VMEM((Q_BLOCK * N_HEADS_A, seq), F32),
            pltpu.VMEM((Q_BLOCK * N_HEADS_A, KEY_BLOCK), BF16),
            pltpu.VMEM((Q_BLOCK * N_HEADS_A, HEAD_DIM), F32),
        ] + head_scratch,
        compiler_params=_cparams(("arbitrary", "arbitrary")),
        name="dsa",
    )(q, k, v, qi, ki2, kw)


def _s5_kernel(u_ref, bm_ref, cm_ref, lre_ref, lim_ref, dsk_ref, wglu_ref, o_ref,
               bu_ref, st_ref, y_ref, *, batch, tl):
    @pl.when(pl.program_id(0) == 0)
    def _():
        st_ref[...] = jnp.zeros(st_ref.shape, F32)

    for gb in range(N_SSM_BUNDLES):
        ch = slice(gb * BUNDLE_CH, (gb + 1) * BUNDLE_CH)
        u_g = u_ref[:, ch]
        bu_ref[...] = jnp.dot(u_g.astype(BF16), bm_ref[gb], preferred_element_type=F32)
        lam_re = jnp.broadcast_to(lre_ref[gb], (batch, BUNDLE_ST))
        lam_im = jnp.broadcast_to(lim_ref[gb], (batch, BUNDLE_ST))

        def step(t, carry):
            x_re, x_im = carry
            r0 = pl.multiple_of(t * batch, batch)
            n_re = lam_re * x_re - lam_im * x_im + bu_ref[pl.ds(r0, batch), 0:BUNDLE_ST]
            n_im = lam_re * x_im + lam_im * x_re + bu_ref[pl.ds(r0, batch), BUNDLE_ST:2 * BUNDLE_ST]
            bu_ref[pl.ds(r0, batch), 0:BUNDLE_ST] = n_re
            bu_ref[pl.ds(r0, batch), BUNDLE_ST:2 * BUNDLE_ST] = n_im
            return n_re, n_im

        x_re, x_im = lax.fori_loop(
            0, tl, step, (st_ref[gb, :, 0:BUNDLE_ST], st_ref[gb, :, BUNDLE_ST:2 * BUNDLE_ST]))
        st_ref[gb, :, 0:BUNDLE_ST] = x_re
        st_ref[gb, :, BUNDLE_ST:2 * BUNDLE_ST] = x_im
        y = jnp.dot(bu_ref[...].astype(BF16), cm_ref[gb], preferred_element_type=F32)
        y_ref[:, ch] = y + dsk_ref[:, ch] * u_g

    y = jax.nn.gelu(y_ref[...])
    gl = jnp.dot(y.astype(BF16), wglu_ref[...], preferred_element_type=F32)
    o_ref[...] = (y * jax.nn.sigmoid(gl)).astype(BF16)


def _s5_params(a_re, a_im, b_re, b_im, c_re, c_im, log_dt):
    dt = jnp.exp(log_dt)[:, None]
    mag = jnp.exp(a_re * dt)
    lb_re = mag * jnp.cos(a_im * dt)
    lb_im = mag * jnp.sin(a_im * dt)
    nr, ni = lb_re - 1.0, lb_im
    den = a_re * a_re + a_im * a_im
    q_re = (nr * a_re + ni * a_im) / den
    q_im = (ni * a_re - nr * a_im) / den
    bb_re = q_re[..., None] * b_re - q_im[..., None] * b_im
    bb_im = q_re[..., None] * b_im + q_im[..., None] * b_re
    eye = jnp.eye(SSM_BUNDLE, dtype=F32)
    nb = N_SSM_BUNDLES

    def pack_b(m):
        m = m.reshape(nb, SSM_BUNDLE, SSM_STATE, SSM_GROUP)
        return jnp.einsum('bgpi,gh->bgihp', m, eye).reshape(nb, BUNDLE_CH, BUNDLE_ST)

    def pack_c(m):
        m = m.reshape(nb, SSM_BUNDLE, SSM_GROUP, SSM_STATE)
        return jnp.einsum('bgop,gh->bgpho', m, eye).reshape(nb, BUNDLE_ST, BUNDLE_CH)

    bm = jnp.concatenate([pack_b(bb_re), pack_b(bb_im)], axis=2).astype(BF16)
    cm = jnp.concatenate([pack_c(c_re), -pack_c(c_im)], axis=1).astype(BF16)
    return bm, cm, lb_re.reshape(nb, 1, BUNDLE_ST), lb_im.reshape(nb, 1, BUNDLE_ST)


def _s5(u_tm, bm, cm, lre, lim, d_skip, w_glu, batch, seq, tl):
    rows = tl * batch
    const3 = lambda t: (0, 0, 0)
    const2 = lambda t: (0, 0)
    return pl.pallas_call(
        functools.partial(_s5_kernel, batch=batch, tl=tl),
        grid=(seq // tl,),
        in_specs=[
            pl.BlockSpec((rows, WIDTH_B), lambda t: (t, 0)),
            pl.BlockSpec(bm.shape, const3),
            pl.BlockSpec(cm.shape, const3),
            pl.BlockSpec(lre.shape, const3),
            pl.BlockSpec(lim.shape, const3),
            pl.BlockSpec((1, WIDTH_B), const2),
            pl.BlockSpec((WIDTH_B, WIDTH_B), const2),
        ],
        out_specs=pl.BlockSpec((rows, WIDTH_B), lambda t: (t, 0)),
        out_shape=jax.ShapeDtypeStruct((seq * batch, WIDTH_B), BF16),
        scratch_shapes=[
            pltpu.VMEM((rows, 2 * BUNDLE_ST), F32),
            pltpu.VMEM((N_SSM_BUNDLES, batch, 2 * BUNDLE_ST), F32),
            pltpu.VMEM((rows, WIDTH_B), F32),
        ],
        compiler_params=_cparams(("arbitrary",)),
        name="s5",
    )(u_tm, bm, cm, lre, lim, d_skip, w_glu)


def _pool_kernel(p_ref, w_ref, sc_ref, o_ref):
    seq = p_ref.shape[0]
    t = lax.broadcasted_iota(I32, (seq, POOL_GROUP), 0)
    t1 = (t + 1).astype(F32)
    for g, win in enumerate(POOL_WINDOWS):
        ch = slice(g * POOL_GROUP, (g + 1) * POOL_GROUP)
        x = p_ref[:, ch]
        s = x
        sh = 1
        while sh < win:
            s = s + jnp.where(t >= sh, pltpu.roll(s, sh, 0), 0.0)
            sh *= 2
        pooled = s / jnp.minimum(t1, float(win)) - x
        y = jnp.dot(pooled.astype(BF16), w_ref[g], preferred_element_type=F32)
        o_ref[:, ch] = (y * sc_ref[:, ch]).astype(BF16)


def _pool(p, w_pool, pool_scale, batch, seq):
    return pl.pallas_call(
        _pool_kernel,
        grid=(batch,),
        in_specs=[
            pl.BlockSpec((seq, WIDTH_C), lambda b: (b, 0)),
            pl.BlockSpec(w_pool.shape, lambda b: (0, 0, 0)),
            pl.BlockSpec((1, WIDTH_C), lambda b: (0, 0)),
        ],
        out_specs=pl.BlockSpec((seq, WIDTH_C), lambda b: (b, 0)),
        out_shape=jax.ShapeDtypeStruct(p.shape, BF16),
        compiler_params=_cparams(("arbitrary",)),
        name="pool",
    )(p, w_pool, pool_scale)


def _col_tiles(w, tn):
    *lead, k, n = w.shape
    nl = len(lead)
    return w.reshape(*lead, k, n // tn, tn).transpose(*range(nl), nl + 1, nl, nl + 2)


def _merge_kernel(h_ref, xc_ref, oa_ref, ob_ref, oc_ref, wg_ref, bg_ref,
                  pa_ref, pb_ref, pc_ref, wo_ref, gt_ref, o_ref, mg_ref, *, nj1, tn1):
    j = pl.program_id(1)

    @pl.when(j < nj1)
    def _():
        h = h_ref[...]
        branches = ((oa_ref, pa_ref), (ob_ref, pb_ref), (oc_ref, pc_ref))
        merged = None
        for i, (o_r, p_r) in enumerate(branches):
            gate = jax.nn.sigmoid(jnp.dot(h, wg_ref[i, 0], preferred_element_type=F32) + bg_ref[0, i:i + 1, :])
            term = gate * jnp.dot(o_r[...], p_r[0], preferred_element_type=F32)
            merged = term if merged is None else merged + term
        c0 = pl.multiple_of(j * tn1, tn1)
        mg_ref[:, pl.ds(c0, tn1)] = merged.astype(BF16)

    @pl.when(j >= nj1)
    def _():
        mix = jnp.dot(mg_ref[...], wo_ref[0], preferred_element_type=F32)
        o_ref[...] = xc_ref[...] + gt_ref[0] * mix


def _merge(x2, h, gt, o_a, o_b_tm, o_c, wg, bg, p_a, p_b, p_c, w_out, seq, tm, tn1, tn2):
    n, d = x2.shape
    nj1, nj2 = d // tn1, d // tn2
    tpb = seq // tm
    row = lambda i, j: (i, 0)
    j1 = lambda j: jnp.minimum(j, nj1 - 1)
    j2 = lambda j: jnp.maximum(j - nj1, 0)
    w1 = lambda i, j: (j1(j), 0, 0)
    return pl.pallas_call(
        functools.partial(_merge_kernel, nj1=nj1, tn1=tn1),
        grid=(n // tm, nj1 + nj2),
        in_specs=[
            pl.BlockSpec((tm, d), row),
            pl.BlockSpec((tm, tn2), lambda i, j: (i, j2(j))),
            pl.BlockSpec((tm, WIDTH_A), row),
            pl.BlockSpec((tm, WIDTH_B), lambda i, j: (i % tpb, i // tpb)),
            pl.BlockSpec((tm, WIDTH_C), row),
            pl.BlockSpec((3, 1, d, tn1), lambda i, j: (0, j1(j), 0, 0)),
            pl.BlockSpec((1, 3, tn1), w1),
            pl.BlockSpec((1, WIDTH_A, tn1), w1),
            pl.BlockSpec((1, WIDTH_B, tn1), w1),
            pl.BlockSpec((1, WIDTH_C, tn1), w1),
            pl.BlockSpec((1, d, tn2), lambda i, j: (j2(j), 0, 0)),
            pl.BlockSpec((1, 1, tn2), lambda i, j: (i // tpb, 0, j2(j))),
        ],
        out_specs=pl.BlockSpec((tm, tn2), lambda i, j: (i, j2(j))),
        out_shape=jax.ShapeDtypeStruct((n, d), F32),
        scratch_shapes=[pltpu.VMEM((tm, d), BF16)],
        compiler_params=_cparams(("arbitrary", "arbitrary")),
        name="merge",
    )(h, x2, o_a, o_b_tm, o_c, _col_tiles(wg, tn1), _col_tiles(bg, tn1), _col_tiles(p_a, tn1),
      _col_tiles(p_b, tn1), _col_tiles(p_c, tn1), _col_tiles(w_out, tn2), gt)


FFN_HALO = 16


def _ffn_kernel(x_ref, xh_ref, mod_ref, g_ref, wa_ref, wb_ref, cw_ref, cb_ref, wd_ref, o_ref,
                h_ref, acc_ref, *, tm, tpb):
    i = pl.program_id(0)
    j = pl.program_id(1)
    nj = pl.num_programs(1)

    @pl.when(j == 0)
    def _():
        g, sc, sh = g_ref[...], mod_ref[0, 4:5, :], mod_ref[0, 3:4, :]
        halo = _norm_mod(xh_ref[...], g, sc, sh)
        h_ref[0:FFN_HALO, :] = jnp.where(i % tpb == 0, 0.0, halo).astype(BF16)
        h_ref[FFN_HALO:, :] = _norm_mod(x_ref[...], g, sc, sh).astype(BF16)
        acc_ref[...] = jnp.zeros(acc_ref.shape, F32)

    a = jnp.dot(h_ref[...], wa_ref[0], preferred_element_type=F32)
    b = jnp.dot(h_ref[FFN_HALO:, :], wb_ref[0], preferred_element_type=F32)
    a_conv = cb_ref[...] + a[FFN_HALO - 2:FFN_HALO - 2 + tm] * cw_ref[0:1, :]
    a_conv = a_conv + a[FFN_HALO - 1:FFN_HALO - 1 + tm] * cw_ref[1:2, :]
    a_conv = a_conv + a[FFN_HALO:] * cw_ref[2:3, :]
    act = (a_conv * jax.nn.sigmoid(a_conv)) * b
    acc_ref[...] += jnp.dot(act.astype(BF16), wd_ref[...], preferred_element_type=F32)

    @pl.when(j == nj - 1)
    def _():
        o_ref[...] = x_ref[...] + mod_ref[0, 5:6, :] * acc_ref[...]


def _ffn(x2, mod, g2, w_a, w_b, conv_w, conv_b, w_down, seq, tm, tn):
    n, d = x2.shape
    dff = w_a.shape[1]
    tpb = seq // tm
    hb = tm // FFN_HALO
    return pl.pallas_call(
        functools.partial(_ffn_kernel, tm=tm, tpb=tpb),
        grid=(n // tm, dff // tn),
        in_specs=[
            pl.BlockSpec((tm, d), lambda i, j: (i, 0)),
            pl.BlockSpec((FFN_HALO, d), lambda i, j: (jnp.maximum(i * hb - 1, 0), 0)),
            pl.BlockSpec((1, 6, d), lambda i, j: (i // tpb, 0, 0)),
            pl.BlockSpec((1, d), lambda i, j: (0, 0)),
            pl.BlockSpec((1, d, tn), lambda i, j: (j, 0, 0)),
            pl.BlockSpec((1, d, tn), lambda i, j: (j, 0, 0)),
            pl.BlockSpec((CONV_WIDTH, tn), lambda i, j: (0, j)),
            pl.BlockSpec((1, tn), lambda i, j: (0, j)),
            pl.BlockSpec((tn, d), lambda i, j: (j, 0)),
        ],
        out_specs=pl.BlockSpec((tm, d), lambda i, j: (i, 0)),
        out_shape=jax.ShapeDtypeStruct((n, d), F32),
        scratch_shapes=[pltpu.VMEM((tm + FFN_HALO, d), BF16), pltpu.VMEM((tm, d), F32)],
        compiler_params=_cparams(("arbitrary", "arbitrary")),
        name="ffn",
    )(x2, x2, mod, g2, _col_tiles(w_a, tn), _col_tiles(w_b, tn), conv_w, conv_b, w_down)


def _rope_tables(positions, rot_dim, period):
    half = rot_dim // 2
    inv_freq = ROPE_THETA ** (-jnp.arange(half, dtype=F32) * (2.0 / rot_dim))
    ang = positions.astype(F32)[..., None] * inv_freq
    cos, sin = jnp.cos(ang), jnp.sin(ang)
    rest = period - 2 * half
    ones = jnp.ones(cos.shape[:-1] + (rest,), F32)
    zh = jnp.zeros_like(sin)
    zr = jnp.zeros_like(ones)
    reps = LANES // period
    out = []
    for parts in ((cos, cos, ones), (-sin, zh, zr), (zh, sin, zr)):
        t = jnp.concatenate(parts, axis=-1)
        out.append(jnp.tile(t, (1, 1, reps)).reshape(-1, LANES))
    return out


def _pad_cols(w, new):
    return jnp.pad(w, ((0, 0), (0, new - w.shape[1])))


def kernel(x, c, positions, w_ada, b_ada, g_norm1, g_norm2, w_in, g_q, g_k, a_re, a_im, b_re, b_im,
           c_re, c_im, d_skip, log_dt, w_glu, w_pool, pool_scale, p_a, p_b, p_c, w_gate, b_gate,
           w_out, w_up, conv_w, conv_b, w_down, *, tm=512, tm_merge=1024, tn_merge=256, tn_out=512, tn_ffn=512, tl=32):
    batch, seq, d = x.shape
    depth = w_ada.shape[0]
    n = batch * seq
    tm = min(tm, seq)
    tm_merge = min(tm_merge, seq)
    tl = min(tl, seq)
    topk = min(TOPK_MAX, seq // 4)
    dff_pad = -(-D_FF // tn_ffn) * tn_ffn

    mod_all = _ada(c, w_ada, b_ada).reshape(depth, batch, 6, d)
    tabs_q = _rope_tables(positions, ROT_DIM, HEAD_DIM)
    tabs_i = _rope_tables(positions, IDX_ROT_DIM, IDX_DIM)

    x2 = x.reshape(n, d)
    for l in range(depth):
        mod = mod_all[l]
        gt1 = mod[:, 2:3, :]
        w_in_pad = jnp.concatenate(
            [w_in[l][:, :COL_KW + KW_USED],
             jnp.zeros((d, LANES - KW_USED), F32),
             w_in[l][:, COL_KW + KW_USED:]], axis=1).astype(BF16)
        h, q, k, v, qi, ki2, kw, u_tm, p = _proj(
            x2, mod, g_norm1[l][None], w_in_pad, g_q[l][None], g_k[l][None], tabs_q, tabs_i,
            batch, seq, tm)
        o_a = _dsa(q, k, v, qi, ki2, kw, batch, seq, topk)
        bm, cm, lre, lim = _s5_params(a_re[l], a_im[l], b_re[l], b_im[l], c_re[l], c_im[l], log_dt[l])
        o_b = _s5(u_tm.reshape(seq * batch, WIDTH_B), bm, cm, lre, lim, d_skip[l][None],
                  w_glu[l].astype(BF16), batch, seq, tl)
        o_c = _pool(p, w_pool[l].astype(BF16), pool_scale[l][None], batch, seq)
        x2 = _merge(x2, h, gt1, o_a, o_b.reshape(seq, batch * WIDTH_B), o_c,
                    w_gate[l].astype(BF16), b_gate[l], p_a[l].astype(BF16), p_b[l].astype(BF16),
                    p_c[l].astype(BF16), w_out[l].astype(BF16), seq, tm_merge, tn_merge, tn_out)
        w_a = _pad_cols(w_up[l][:, :D_FF], dff_pad).astype(BF16)
        w_b = _pad_cols(w_up[l][:, D_FF:], dff_pad).astype(BF16)
        w_d = jnp.pad(w_down[l], ((0, dff_pad - D_FF), (0, 0))).astype(BF16)
        x2 = _ffn(x2, mod, g_norm2[l][None], w_a, w_b, _pad_cols(conv_w[l], dff_pad),
                  _pad_cols(conv_b[l][None], dff_pad), w_d, seq, tm, tn_ffn)
    return x2.reshape(batch, seq, d)
```

```python
import functools
import math

import jax
import jax.numpy as jnp
from jax import lax
from jax.experimental import pallas as pl
from jax.experimental.pallas import tpu as pltpu

F32 = jnp.float32
BF16 = jnp.bfloat16
I32 = jnp.int32

D_MODEL = 2048
DEPTH = 2
CHUNK = 64
EPS = 1e-6
NEG_INF = -1e30
ROPE_THETA = 500000.0

N_HEADS_A = 8
HEAD_DIM = 128
ROT_DIM = HEAD_DIM // 4
N_IDX_HEADS = 8
IDX_DIM = 64
IDX_ROT_DIM = IDX_DIM // 4
N_IDX_PAIRS = N_IDX_HEADS * IDX_DIM // 128
TOPK_MAX = 256
Q_BLOCK = 128
WIDTH_A = N_HEADS_A * HEAD_DIM

WIDTH_B = D_MODEL // 4
SSM_GROUP = 16
N_SSM_GROUPS = WIDTH_B // SSM_GROUP
SSM_STATE = 64
SSM_BUNDLE = 8
N_SSM_BUNDLES = N_SSM_GROUPS // SSM_BUNDLE
BUNDLE_CH = SSM_BUNDLE * SSM_GROUP
BUNDLE_ST = SSM_BUNDLE * SSM_STATE

WIDTH_C = D_MODEL // 4
POOL_WINDOWS = (2, 4, 8, 16)
POOL_GROUP = WIDTH_C // 4

D_FF = 5504
CONV_WIDTH = 3

LANES = 128
INT_MIN = -(2 ** 31)

COL_K = WIDTH_A
COL_V = COL_K + HEAD_DIM
COL_QI = COL_V + HEAD_DIM
COL_KW = COL_QI + N_IDX_HEADS * IDX_DIM
COL_U = COL_KW + LANES
COL_P = COL_U + WIDTH_B
D_IN_PAD = COL_P + WIDTH_C
KW_USED = IDX_DIM + N_IDX_HEADS

VMEM_LIMIT = 56 * 1024 * 1024


def _cparams(sem):
    return pltpu.CompilerParams(dimension_semantics=sem, vmem_limit_bytes=VMEM_LIMIT)


def _norm_mod(x, g, sc, sh):
    ms = jnp.mean(x * x, axis=-1, keepdims=True)
    y = x * lax.rsqrt(ms + EPS)
    return (y * g) * (1.0 + sc) + sh


def _rope(x, c, sa, sb, half):
    return x * c + pltpu.roll(x, LANES - half, 1) * sa + pltpu.roll(x, half, 1) * sb


def _ada_kernel(c_ref, w_ref, b_ref, o_ref):
    c = c_ref[...]
    ca = c * jax.nn.sigmoid(c)
    o_ref[0] = jnp.dot(ca, w_ref[0], preferred_element_type=F32) + b_ref[0]


def _ada(c, w_ada, b_ada, tn=1024):
    depth, d, n = w_ada.shape
    b = c.shape[0]
    return pl.pallas_call(
        _ada_kernel,
        grid=(depth, n // tn),
        in_specs=[
            pl.BlockSpec((b, d), lambda l, j: (0, 0)),
            pl.BlockSpec((1, d, tn), lambda l, j: (l, 0, j)),
            pl.BlockSpec((1, 1, tn), lambda l, j: (l, 0, j)),
        ],
        out_specs=pl.BlockSpec((1, b, tn), lambda l, j: (l, 0, j)),
        out_shape=jax.ShapeDtypeStruct((depth, b, n), F32),
        compiler_params=_cparams(("arbitrary", "arbitrary")),
        name="ada",
    )(c, w_ada, b_ada.reshape(depth, 1, n))


def _proj_kernel(x_ref, mod_ref, g_ref, w_ref, gq_ref, gk_ref, cq_ref, saq_ref, sbq_ref,
                 ci_ref, sai_ref, sbi_ref,
                 h_ref, q_ref, k_ref, v_ref, qi_ref, ki2_ref, kw_ref, u_ref, p_ref):
    h = _norm_mod(x_ref[...], g_ref[...], mod_ref[0, 1:2, :], mod_ref[0, 0:1, :]).astype(BF16)
    h_ref[...] = h

    def mm(c0, width):
        return jnp.dot(h, w_ref[:, c0:c0 + width], preferred_element_type=F32)

    def qk_norm_rope(xh, g):
        ms = jnp.mean(xh * xh, axis=-1, keepdims=True)
        y = xh * lax.rsqrt(ms + EPS) * g
        return _rope(y, cq_ref[...], saq_ref[...], sbq_ref[...], ROT_DIM // 2)

    def store_stacked(ref, slab, idx, nslab):
        for qbl in range(slab.shape[0] // Q_BLOCK):
            r0 = (qbl * nslab + idx) * Q_BLOCK
            ref[r0:r0 + Q_BLOCK, :] = slab[qbl * Q_BLOCK:(qbl + 1) * Q_BLOCK]

    q_all = mm(0, WIDTH_A)
    for hd in range(N_HEADS_A):
        qh = qk_norm_rope(q_all[:, hd * HEAD_DIM:(hd + 1) * HEAD_DIM], gq_ref[...]).astype(BF16)
        store_stacked(q_ref, qh, hd, N_HEADS_A)
    kv = mm(COL_K, 2 * HEAD_DIM)
    k_ref[...] = qk_norm_rope(kv[:, 0:HEAD_DIM], gk_ref[...]).astype(BF16)
    v_ref[...] = kv[:, HEAD_DIM:2 * HEAD_DIM].astype(BF16)
    qi_all = mm(COL_QI, N_IDX_PAIRS * LANES)
    for s in range(N_IDX_PAIRS):
        qi = qi_all[:, s * LANES:(s + 1) * LANES]
        qi = _rope(qi, ci_ref[...], sai_ref[...], sbi_ref[...], IDX_ROT_DIM // 2).astype(BF16)
        store_stacked(qi_ref, qi, s, N_IDX_PAIRS)
    rest = mm(COL_KW, LANES + WIDTH_B + WIDTH_C)
    kw = rest[:, 0:LANES]
    ki = _rope(kw, ci_ref[...], sai_ref[...], sbi_ref[...], IDX_ROT_DIM // 2)
    lane = lax.broadcasted_iota(I32, ki.shape, 1)
    ki_lo = jnp.where(lane < IDX_DIM, ki, 0.0)
    ki2_ref[:, 0:LANES] = ki_lo.astype(BF16)
    ki2_ref[:, LANES:2 * LANES] = pltpu.roll(ki_lo, IDX_DIM, 1).astype(BF16)
    kw_ref[...] = kw * (N_IDX_HEADS ** -0.5 * IDX_DIM ** -0.5)
    u_ref[...] = rest[:, LANES:LANES + WIDTH_B]
    p_ref[...] = rest[:, LANES + WIDTH_B:]


def _proj(x2, mod, g1, w_in_pad, g_q, g_k, tabs_q, tabs_i, batch, seq, tm):
    n, d = x2.shape
    tpb = seq // tm
    row = lambda i: (i, 0)
    const = lambda i: (0, 0)
    tab_spec = pl.BlockSpec((tm, LANES), row)
    return pl.pallas_call(
        _proj_kernel,
        grid=(n // tm,),
        in_specs=[
            pl.BlockSpec((tm, d), row),
            pl.BlockSpec((1, 6, d), lambda i: (i // tpb, 0, 0)),
            pl.BlockSpec((1, d), const),
            pl.BlockSpec((d, D_IN_PAD), const),
            pl.BlockSpec((1, HEAD_DIM), const),
            pl.BlockSpec((1, HEAD_DIM), const),
            tab_spec, tab_spec, tab_spec, tab_spec, tab_spec, tab_spec,
        ],
        out_specs=[
            pl.BlockSpec((tm, d), row),
            pl.BlockSpec((tm * N_HEADS_A, HEAD_DIM), row),
            pl.BlockSpec((tm, HEAD_DIM), row),
            pl.BlockSpec((tm, HEAD_DIM), row),
            pl.BlockSpec((tm * N_IDX_PAIRS, LANES), row),
            pl.BlockSpec((tm, 2 * LANES), row),
            pl.BlockSpec((tm, LANES), row),
            pl.BlockSpec((tm, WIDTH_B), lambda i: (i % tpb, i // tpb)),
            pl.BlockSpec((tm, WIDTH_C), row),
        ],
        out_shape=[
            jax.ShapeDtypeStruct((n, d), BF16),
            jax.ShapeDtypeStruct((n * N_HEADS_A, HEAD_DIM), BF16),
            jax.ShapeDtypeStruct((n, HEAD_DIM), BF16),
            jax.ShapeDtypeStruct((n, HEAD_DIM), BF16),
            jax.ShapeDtypeStruct((n * N_IDX_PAIRS, LANES), BF16),
            jax.ShapeDtypeStruct((n, 2 * LANES), BF16),
            jax.ShapeDtypeStruct((n, LANES), F32),
            jax.ShapeDtypeStruct((seq, batch * WIDTH_B), F32),
            jax.ShapeDtypeStruct((n, WIDTH_C), F32),
        ],
        compiler_params=_cparams(("arbitrary",)),
        name="proj",
    )(x2, mod, g1, w_in_pad, g_q, g_k, *tabs_q, *tabs_i)


KEY_SLABS = 4
KEY_BLOCK = KEY_SLABS * LANES


def _dsa_scores(qi_ref, ki2_ref, kw_ref, s_ref, nqb):
    row = lax.broadcasted_iota(I32, (Q_BLOCK, KEY_BLOCK), 0)
    lane = lax.broadcasted_iota(I32, (Q_BLOCK, KEY_BLOCK), 1)
    chunk_end = (row // CHUNK + 1) * CHUNK
    trans_b = (((1,), (1,)), ((), ()))

    def qb_body(qq, carry):
        r0 = pl.multiple_of(qq * Q_BLOCK, Q_BLOCK)
        w = kw_ref[pl.ds(r0, Q_BLOCK), :]
        wb = [jnp.broadcast_to(w[:, IDX_DIM + hd:IDX_DIM + hd + 1], (Q_BLOCK, KEY_BLOCK))
              for hd in range(N_IDX_HEADS)]
        limit_row = r0 + chunk_end

        def key_body(j, c2):
            k0 = pl.multiple_of(j * KEY_BLOCK, KEY_BLOCK)
            kia = ki2_ref[pl.ds(k0, KEY_BLOCK), 0:LANES]
            kib = ki2_ref[pl.ds(k0, KEY_BLOCK), LANES:2 * LANES]
            qp = qi_ref[pl.ds(pl.multiple_of(r0 * N_IDX_PAIRS, Q_BLOCK), Q_BLOCK * N_IDX_PAIRS), :]
            sa = lax.dot_general(qp, kia, trans_b, preferred_element_type=F32)
            sb = lax.dot_general(qp, kib, trans_b, preferred_element_type=F32)
            sc = jnp.zeros((Q_BLOCK, KEY_BLOCK), F32)
            for pr in range(N_IDX_PAIRS):
                rows = slice(pr * Q_BLOCK, (pr + 1) * Q_BLOCK)
                sc = sc + jnp.maximum(sa[rows], 0.0) * wb[2 * pr] + jnp.maximum(sb[rows], 0.0) * wb[2 * pr + 1]
            adm = ((k0 + lane) < limit_row) & (sc > NEG_INF * 0.5)
            bits = pltpu.bitcast(sc, I32)
            bits = jnp.where(bits == INT_MIN, 0, bits)
            skey = bits ^ ((bits >> 31) & 0x7FFFFFFF)
            skey = jnp.where(adm, skey, INT_MIN)
            for sl in range(KEY_SLABS):
                s_ref[j * KEY_SLABS + sl, pl.ds(r0, Q_BLOCK), :] = skey[:, sl * LANES:(sl + 1) * LANES]
            return c2

        nkb = (r0 + Q_BLOCK + KEY_BLOCK - 1) // KEY_BLOCK
        lax.fori_loop(0, nkb, key_body, 0)
        return carry

    lax.fori_loop(0, nqb, qb_body, 0)


def _dsa_threshold(s_ref, thr_ref, cnt_ref, nqb, topk):
    thr_ref[...] = jnp.full(thr_ref.shape, INT_MIN, I32)

    ones = jnp.ones((LANES, LANES), BF16)

    def bit_body(i, carry):
        bitval = jnp.left_shift(jnp.int32(1), 31 - i)

        for qq in range(nqb):
            rows = slice(qq * Q_BLOCK, (qq + 1) * Q_BLOCK)
            cand = thr_ref[rows, :] + bitval
            acc = jnp.zeros((Q_BLOCK, LANES), F32)
            for s in range(qq + 1):
                acc = acc + jnp.where(s_ref[s, rows, :] >= cand, 1.0, 0.0)
            cnt_ref[rows, :] = acc.astype(BF16)
        cnt = jnp.dot(cnt_ref[...], ones, preferred_element_type=F32)
        thr = thr_ref[...]
        thr_ref[...] = jnp.where(cnt >= float(topk), thr + bitval, thr)
        return carry

    lax.fori_loop(0, 32, bit_body, 0)


def _dsa_kernel(q_ref, k_ref, v_ref, qi_ref, ki2_ref, kw_ref, o_ref,
                s_ref, thr_ref, cnt_ref, bias_ref, lg_ref, p_ref, acc_ref, vx_ref, *m_refs, topk, nqb):
    qb = pl.program_id(1)

    @pl.when(qb == 0)
    def _():
        _dsa_scores(qi_ref, ki2_ref, kw_ref, s_ref, nqb)
        _dsa_threshold(s_ref, thr_ref, cnt_ref, nqb, topk)
        vx_ref[:, 0:HEAD_DIM] = v_ref[...]
        vx_ref[:, HEAD_DIM:] = jnp.ones((vx_ref.shape[0], LANES), BF16)

    r0 = pl.multiple_of(qb * Q_BLOCK, Q_BLOCK)
    nkb = (r0 + Q_BLOCK + KEY_BLOCK - 1) // KEY_BLOCK
    trans_b = (((1,), (1,)), ((), ()))
    thr = thr_ref[pl.ds(r0, Q_BLOCK), :]

    def bias_body(j, carry):
        for sl in range(KEY_SLABS):
            sk = s_ref[j * KEY_SLABS + sl, pl.ds(r0, Q_BLOCK), :]
            sel = (sk >= thr) & (sk > INT_MIN)
            bias_ref[j * KEY_SLABS + sl] = jnp.where(sel, 0.0, NEG_INF)
        return carry

    lax.fori_loop(0, nkb, bias_body, 0)

    for hd in range(N_HEADS_A):
        m_refs[hd][...] = jnp.full((Q_BLOCK, LANES), NEG_INF, F32)
    acc_ref[...] = jnp.zeros(acc_ref.shape, F32)

    def max_body(j, carry):
        k0 = pl.multiple_of(j * KEY_BLOCK, KEY_BLOCK)
        s = lax.dot_general(q_ref[...], k_ref[pl.ds(k0, KEY_BLOCK), :], trans_b, preferred_element_type=F32)
        for hd in range(N_HEADS_A):
            rows = slice(hd * Q_BLOCK, (hd + 1) * Q_BLOCK)
            m = m_refs[hd][...]
            for sl in range(KEY_SLABS):
                sh = s[rows, sl * LANES:(sl + 1) * LANES] + bias_ref[j * KEY_SLABS + sl]
                lg_ref[j * KEY_SLABS + sl, rows, :] = sh
                m = jnp.maximum(m, sh)
            m_refs[hd][...] = m
        return carry

    lax.fori_loop(0, nkb, max_body, 0)
    for hd in range(N_HEADS_A):
        m_refs[hd][...] = jnp.broadcast_to(jnp.max(m_refs[hd][...], axis=1, keepdims=True), (Q_BLOCK, LANES))

    c_exp = HEAD_DIM ** -0.5 * math.log2(math.e)

    def sum_body(j, carry):
        k0 = pl.multiple_of(j * KEY_BLOCK, KEY_BLOCK)
        for hd in range(N_HEADS_A):
            rows = slice(hd * Q_BLOCK, (hd + 1) * Q_BLOCK)
            m = m_refs[hd][...]
            for sl in range(KEY_SLABS):
                p_s = jnp.exp2((lg_ref[j * KEY_SLABS + sl, rows, :] - m) * c_exp)
                p_ref[rows, sl * LANES:(sl + 1) * LANES] = p_s.astype(BF16)
        acc_ref[...] += jnp.dot(p_ref[...], vx_ref[pl.ds(k0, KEY_BLOCK), :], preferred_element_type=F32)
        return carry

    lax.fori_loop(0, nkb, sum_body, 0)
    for hd in range(N_HEADS_A):
        rows = slice(hd * Q_BLOCK, (hd + 1) * Q_BLOCK)
        o_ref[:, hd * HEAD_DIM:(hd + 1) * HEAD_DIM] = (
            acc_ref[rows, 0:HEAD_DIM] / acc_ref[rows, HEAD_DIM:]).astype(BF16)


def _dsa(q, k, v, qi, ki2, kw, batch, seq, topk):
    assert seq % KEY_BLOCK == 0
    n = k.shape[0]
    nqb = seq // Q_BLOCK
    qrow = lambda b, i: (b * nqb + i, 0)
    brow = lambda b, i: (b, 0)
    nsl = seq // LANES
    head_scratch = [pltpu.VMEM((Q_BLOCK, LANES), F32)] * N_HEADS_A
    return pl.pallas_call(
        functools.partial(_dsa_kernel, topk=topk, nqb=nqb),
        grid=(batch, nqb),
        in_specs=[
            pl.BlockSpec((Q_BLOCK * N_HEADS_A, HEAD_DIM), qrow),
            pl.BlockSpec((seq, HEAD_DIM), brow),
            pl.BlockSpec((seq, HEAD_DIM), brow),
            pl.BlockSpec((seq * N_IDX_PAIRS, LANES), brow),
            pl.BlockSpec((seq, 2 * LANES), brow),
            pl.BlockSpec((seq, LANES), brow),
        ],
        out_specs=pl.BlockSpec((Q_BLOCK, WIDTH_A), qrow),
        out_shape=jax.ShapeDtypeStruct((n, WIDTH_A), BF16),
        scratch_shapes=[
            pltpu.VMEM((nsl, seq, LANES), I32),
            pltpu.VMEM((seq, LANES), I32),
            pltpu.VMEM((seq, LANES), BF16),
            pltpu.VMEM((nsl, Q_BLOCK, LANES), F32),
            pltpu.VMEM((nsl, Q_BLOCK * N_HEADS_A, LANES), F32),
            pltpu.VMEM((Q_BLOCK * N_HEADS_A, KEY_BLOCK), BF16),
            pltpu.VMEM((Q_BLOCK * N_HEADS_A, HEAD_DIM + LANES), F32),
            pltpu.VMEM((seq, HEAD_DIM + LANES), BF16),
        ] + head_scratch,
        compiler_params=_cparams(("arbitrary", "arbitrary")),
        name="dsa",
    )(q, k, v, qi, ki2, kw)


def _s5_kernel(u_ref, bm_ref, cm_ref, lre_ref, lim_ref, dsk_ref, wglu_ref, o_ref,
               bu_ref, st_ref, y_ref, *, batch, tl):
    @pl.when(pl.program_id(0) == 0)
    def _():
        st_ref[...] = jnp.zeros(st_ref.shape, F32)

    for gb in range(N_SSM_BUNDLES):
        ch = slice(gb * BUNDLE_CH, (gb + 1) * BUNDLE_CH)
        u_g = u_ref[:, ch]
        bu_ref[...] = jnp.dot(u_g.astype(BF16), bm_ref[gb], preferred_element_type=F32)
        lam_re = jnp.broadcast_to(lre_ref[gb], (batch, BUNDLE_ST))
        lam_im = jnp.broadcast_to(lim_ref[gb], (batch, BUNDLE_ST))

        def step(t, carry):
            x_re, x_im = carry
            r0 = pl.multiple_of(t * batch, batch)
            n_re = lam_re * x_re - lam_im * x_im + bu_ref[pl.ds(r0, batch), 0:BUNDLE_ST]
            n_im = lam_re * x_im + lam_im * x_re + bu_ref[pl.ds(r0, batch), BUNDLE_ST:2 * BUNDLE_ST]
            bu_ref[pl.ds(r0, batch), 0:BUNDLE_ST] = n_re
            bu_ref[pl.ds(r0, batch), BUNDLE_ST:2 * BUNDLE_ST] = n_im
            return n_re, n_im

        x_re, x_im = lax.fori_loop(
            0, tl, step, (st_ref[gb, :, 0:BUNDLE_ST], st_ref[gb, :, BUNDLE_ST:2 * BUNDLE_ST]))
        st_ref[gb, :, 0:BUNDLE_ST] = x_re
        st_ref[gb, :, BUNDLE_ST:2 * BUNDLE_ST] = x_im
        y = jnp.dot(bu_ref[...].astype(BF16), cm_ref[gb], preferred_element_type=F32)
        y_ref[:, ch] = y + dsk_ref[:, ch] * u_g

    y = jax.nn.gelu(y_ref[...])
    gl = jnp.dot(y.astype(BF16), wglu_ref[...], preferred_element_type=F32)
    o_ref[...] = (y * jax.nn.sigmoid(gl)).astype(BF16)


def _s5_params(a_re, a_im, b_re, b_im, c_re, c_im, log_dt):
    dt = jnp.exp(log_dt)[:, None]
    mag = jnp.exp(a_re * dt)
    lb_re = mag * jnp.cos(a_im * dt)
    lb_im = mag * jnp.sin(a_im * dt)
    nr, ni = lb_re - 1.0, lb_im
    den = a_re * a_re + a_im * a_im
    q_re = (nr * a_re + ni * a_im) / den
    q_im = (ni * a_re - nr * a_im) / den
    bb_re = q_re[..., None] * b_re - q_im[..., None] * b_im
    bb_im = q_re[..., None] * b_im + q_im[..., None] * b_re
    eye = jnp.eye(SSM_BUNDLE, dtype=F32)
    nb = N_SSM_BUNDLES

    def pack_b(m):
        m = m.reshape(nb, SSM_BUNDLE, SSM_STATE, SSM_GROUP)
        return jnp.einsum('bgpi,gh->bgihp', m, eye).reshape(nb, BUNDLE_CH, BUNDLE_ST)

    def pack_c(m):
        m = m.reshape(nb, SSM_BUNDLE, SSM_GROUP, SSM_STATE)
        return jnp.einsum('bgop,gh->bgpho', m, eye).reshape(nb, BUNDLE_ST, BUNDLE_CH)

    bm = jnp.concatenate([pack_b(bb_re), pack_b(bb_im)], axis=2).astype(BF16)
    cm = jnp.concatenate([pack_c(c_re), -pack_c(c_im)], axis=1).astype(BF16)
    return bm, cm, lb_re.reshape(nb, 1, BUNDLE_ST), lb_im.reshape(nb, 1, BUNDLE_ST)


def _s5(u_tm, bm, cm, lre, lim, d_skip, w_glu, batch, seq, tl):
    rows = tl * batch
    const3 = lambda t: (0, 0, 0)
    const2 = lambda t: (0, 0)
    return pl.pallas_call(
        functools.partial(_s5_kernel, batch=batch, tl=tl),
        grid=(seq // tl,),
        in_specs=[
            pl.BlockSpec((rows, WIDTH_B), lambda t: (t, 0)),
            pl.BlockSpec(bm.shape, const3),
            pl.BlockSpec(cm.shape, const3),
            pl.BlockSpec(lre.shape, const3),
            pl.BlockSpec(lim.shape, const3),
            pl.BlockSpec((1, WIDTH_B), const2),
            pl.BlockSpec((WIDTH_B, WIDTH_B), const2),
        ],
        out_specs=pl.BlockSpec((rows, WIDTH_B), lambda t: (t, 0)),
        out_shape=jax.ShapeDtypeStruct((seq * batch, WIDTH_B), BF16),
        scratch_shapes=[
            pltpu.VMEM((rows, 2 * BUNDLE_ST), F32),
            pltpu.VMEM((N_SSM_BUNDLES, batch, 2 * BUNDLE_ST), F32),
            pltpu.VMEM((rows, WIDTH_B), F32),
        ],
        compiler_params=_cparams(("arbitrary",)),
        name="s5",
    )(u_tm, bm, cm, lre, lim, d_skip, w_glu)


def _pool_kernel(p_ref, w_ref, sc_ref, o_ref):
    seq = p_ref.shape[0]
    t = lax.broadcasted_iota(I32, (seq, POOL_GROUP), 0)
    t1 = (t + 1).astype(F32)
    for g, win in enumerate(POOL_WINDOWS):
        ch = slice(g * POOL_GROUP, (g + 1) * POOL_GROUP)
        x = p_ref[:, ch]
        s = x
        sh = 1
        while sh < win:
            s = s + jnp.where(t >= sh, pltpu.roll(s, sh, 0), 0.0)
            sh *= 2
        pooled = s / jnp.minimum(t1, float(win)) - x
        y = jnp.dot(pooled.astype(BF16), w_ref[g], preferred_element_type=F32)
        o_ref[:, ch] = (y * sc_ref[:, ch]).astype(BF16)


def _pool(p, w_pool, pool_scale, batch, seq):
    return pl.pallas_call(
        _pool_kernel,
        grid=(batch,),
        in_specs=[
            pl.BlockSpec((seq, WIDTH_C), lambda b: (b, 0)),
            pl.BlockSpec(w_pool.shape, lambda b: (0, 0, 0)),
            pl.BlockSpec((1, WIDTH_C), lambda b: (0, 0)),
        ],
        out_specs=pl.BlockSpec((seq, WIDTH_C), lambda b: (b, 0)),
        out_shape=jax.ShapeDtypeStruct(p.shape, BF16),
        compiler_params=_cparams(("arbitrary",)),
        name="pool",
    )(p, w_pool, pool_scale)


def _col_tiles(w, tn):
    *lead, k, n = w.shape
    nl = len(lead)
    return w.reshape(*lead, k, n // tn, tn).transpose(*range(nl), nl + 1, nl, nl + 2)


def _merge_kernel(h_ref, xc_ref, oa_ref, ob_ref, oc_ref, wg_ref, bg_ref,
                  pa_ref, pb_ref, pc_ref, wo_ref, gt_ref, o_ref, mg_ref, *, ntiles, tn):
    i = pl.program_id(0)
    j = pl.program_id(1)
    slot = i % 2

    @pl.when(i < ntiles)
    def _():
        h = h_ref[...]
        branches = ((oa_ref, pa_ref), (ob_ref, pb_ref), (oc_ref, pc_ref))
        merged = None
        for b, (o_r, p_r) in enumerate(branches):
            gate = jax.nn.sigmoid(jnp.dot(h, wg_ref[b, 0], preferred_element_type=F32) + bg_ref[0, b:b + 1, :])
            term = gate * jnp.dot(o_r[...], p_r[0], preferred_element_type=F32)
            merged = term if merged is None else merged + term
        c0 = pl.multiple_of(j * tn, tn)
        mg_ref[slot, :, pl.ds(c0, tn)] = merged.astype(BF16)

    @pl.when(i >= 1)
    def _():
        mix = jnp.dot(mg_ref[1 - slot], wo_ref[0], preferred_element_type=F32)
        o_ref[...] = xc_ref[...] + gt_ref[0] * mix


def _merge(x2, h, gt, o_a, o_b_tm, o_c, wg, bg, p_a, p_b, p_c, w_out, seq, tm, tn):
    n, d = x2.shape
    nj = d // tn
    ntiles = n // tm
    tpb = seq // tm
    cur = lambda i: jnp.minimum(i, ntiles - 1)
    prev = lambda i: jnp.maximum(i - 1, 0)
    row = lambda i, j: (cur(i), 0)
    wt = lambda i, j: (j, 0, 0)
    return pl.pallas_call(
        functools.partial(_merge_kernel, ntiles=ntiles, tn=tn),
        grid=(ntiles + 1, nj),
        in_specs=[
            pl.BlockSpec((tm, d), row),
            pl.BlockSpec((tm, tn), lambda i, j: (prev(i), j)),
            pl.BlockSpec((tm, WIDTH_A), row),
            pl.BlockSpec((tm, WIDTH_B), lambda i, j: (cur(i) % tpb, cur(i) // tpb)),
            pl.BlockSpec((tm, WIDTH_C), row),
            pl.BlockSpec((3, 1, d, tn), lambda i, j: (0, j, 0, 0)),
            pl.BlockSpec((1, 3, tn), wt),
            pl.BlockSpec((1, WIDTH_A, tn), wt),
            pl.BlockSpec((1, WIDTH_B, tn), wt),
            pl.BlockSpec((1, WIDTH_C, tn), wt),
            pl.BlockSpec((1, d, tn), wt),
            pl.BlockSpec((1, 1, tn), lambda i, j: (prev(i) // tpb, 0, j)),
        ],
        out_specs=pl.BlockSpec((tm, tn), lambda i, j: (prev(i), jnp.where(i == 0, 0, j))),
        out_shape=jax.ShapeDtypeStruct((n, d), F32),
        scratch_shapes=[pltpu.VMEM((2, tm, d), BF16)],
        compiler_params=_cparams(("arbitrary", "arbitrary")),
        name="merge",
    )(h, x2, o_a, o_b_tm, o_c, _col_tiles(wg, tn), _col_tiles(bg, tn), _col_tiles(p_a, tn),
      _col_tiles(p_b, tn), _col_tiles(p_c, tn), _col_tiles(w_out, tn), gt)


FFN_HALO = 16


def _ffn_kernel(x_ref, xh_ref, mod_ref, g_ref, wa_ref, wb_ref, cw_ref, cb_ref, wd_ref, o_ref,
                h_ref, acc_ref, *, tm, tpb):
    i = pl.program_id(0)
    j = pl.program_id(1)
    nj = pl.num_programs(1)

    @pl.when(j == 0)
    def _():
        g, sc, sh = g_ref[...], mod_ref[0, 4:5, :], mod_ref[0, 3:4, :]
        halo = _norm_mod(xh_ref[...], g, sc, sh)
        h_ref[0:FFN_HALO, :] = jnp.where(i % tpb == 0, 0.0, halo).astype(BF16)
        h_ref[FFN_HALO:, :] = _norm_mod(x_ref[...], g, sc, sh).astype(BF16)
        acc_ref[...] = jnp.zeros(acc_ref.shape, F32)

    a = jnp.dot(h_ref[...], wa_ref[0], preferred_element_type=F32)
    b = jnp.dot(h_ref[FFN_HALO:, :], wb_ref[0], preferred_element_type=F32)
    a_conv = cb_ref[...] + a[FFN_HALO - 2:FFN_HALO - 2 + tm] * cw_ref[0:1, :]
    a_conv = a_conv + a[FFN_HALO - 1:FFN_HALO - 1 + tm] * cw_ref[1:2, :]
    a_conv = a_conv + a[FFN_HALO:] * cw_ref[2:3, :]
    act = (a_conv * jax.nn.sigmoid(a_conv)) * b
    acc_ref[...] += jnp.dot(act.astype(BF16), wd_ref[...], preferred_element_type=F32)

    @pl.when(j == nj - 1)
    def _():
        o_ref[...] = x_ref[...] + mod_ref[0, 5:6, :] * acc_ref[...]


def _ffn(x2, mod, g2, w_a, w_b, conv_w, conv_b, w_down, seq, tm, tn):
    n, d = x2.shape
    dff = w_a.shape[1]
    tpb = seq // tm
    hb = tm // FFN_HALO
    return pl.pallas_call(
        functools.partial(_ffn_kernel, tm=tm, tpb=tpb),
        grid=(n // tm, dff // tn),
        in_specs=[
            pl.BlockSpec((tm, d), lambda i, j: (i, 0)),
            pl.BlockSpec((FFN_HALO, d), lambda i, j: (jnp.maximum(i * hb - 1, 0), 0)),
            pl.BlockSpec((1, 6, d), lambda i, j: (i // tpb, 0, 0)),
            pl.BlockSpec((1, d), lambda i, j: (0, 0)),
            pl.BlockSpec((1, d, tn), lambda i, j: (j, 0, 0)),
            pl.BlockSpec((1, d, tn), lambda i, j: (j, 0, 0)),
            pl.BlockSpec((CONV_WIDTH, tn), lambda i, j: (0, j)),
            pl.BlockSpec((1, tn), lambda i, j: (0, j)),
            pl.BlockSpec((tn, d), lambda i, j: (j, 0)),
        ],
        out_specs=pl.BlockSpec((tm, d), lambda i, j: (i, 0)),
        out_shape=jax.ShapeDtypeStruct((n, d), F32),
        scratch_shapes=[pltpu.VMEM((tm + FFN_HALO, d), BF16), pltpu.VMEM((tm, d), F32)],
        compiler_params=_cparams(("arbitrary", "arbitrary")),
        name="ffn",
    )(x2, x2, mod, g2, _col_tiles(w_a, tn), _col_tiles(w_b, tn), conv_w, conv_b, w_down)


def _rope_tables(positions, rot_dim, period):
    half = rot_dim // 2
    inv_freq = ROPE_THETA ** (-jnp.arange(half, dtype=F32) * (2.0 / rot_dim))
    ang = positions.astype(F32)[..., None] * inv_freq
    cos, sin = jnp.cos(ang), jnp.sin(ang)
    rest = period - 2 * half
    ones = jnp.ones(cos.shape[:-1] + (rest,), F32)
    zh = jnp.zeros_like(sin)
    zr = jnp.zeros_like(ones)
    reps = LANES // period
    out = []
    for parts in ((cos, cos, ones), (-sin, zh, zr), (zh, sin, zr)):
        t = jnp.concatenate(parts, axis=-1)
        out.append(jnp.tile(t, (1, 1, reps)).reshape(-1, LANES))
    return out


def _pad_cols(w, new):
    return jnp.pad(w, ((0, 0), (0, new - w.shape[1])))


def kernel(x, c, positions, w_ada, b_ada, g_norm1, g_norm2, w_in, g_q, g_k, a_re, a_im, b_re, b_im,
           c_re, c_im, d_skip, log_dt, w_glu, w_pool, pool_scale, p_a, p_b, p_c, w_gate, b_gate,
           w_out, w_up, conv_w, conv_b, w_down, *, tm=512, tm_merge=1024, tn_merge=256, tn_ffn=512, tl=32):
    batch, seq, d = x.shape
    depth = w_ada.shape[0]
    n = batch * seq
    tm = min(tm, seq)
    tm_merge = min(tm_merge, seq)
    tl = min(tl, seq)
    topk = min(TOPK_MAX, seq // 4)
    dff_pad = -(-D_FF // tn_ffn) * tn_ffn

    mod_all = _ada(c, w_ada, b_ada).reshape(depth, batch, 6, d)
    tabs_q = _rope_tables(positions, ROT_DIM, HEAD_DIM)
    tabs_i = _rope_tables(positions, IDX_ROT_DIM, IDX_DIM)

    x2 = x.reshape(n, d)
    for l in range(depth):
        mod = mod_all[l]
        gt1 = mod[:, 2:3, :]
        w_in_pad = jnp.concatenate(
            [w_in[l][:, :COL_KW + KW_USED],
             jnp.zeros((d, LANES - KW_USED), F32),
             w_in[l][:, COL_KW + KW_USED:]], axis=1).astype(BF16)
        h, q, k, v, qi, ki2, kw, u_tm, p = _proj(
            x2, mod, g_norm1[l][None], w_in_pad, g_q[l][None], g_k[l][None], tabs_q, tabs_i,
            batch, seq, tm)
        o_a = _dsa(q, k, v, qi, ki2, kw, batch, seq, topk)
        bm, cm, lre, lim = _s5_params(a_re[l], a_im[l], b_re[l], b_im[l], c_re[l], c_im[l], log_dt[l])
        o_b = _s5(u_tm.reshape(seq * batch, WIDTH_B), bm, cm, lre, lim, d_skip[l][None],
                  w_glu[l].astype(BF16), batch, seq, tl)
        o_c = _pool(p, w_pool[l].astype(BF16), pool_scale[l][None], batch, seq)
        x2 = _merge(x2, h, gt1, o_a, o_b.reshape(seq, batch * WIDTH_B), o_c,
                    w_gate[l].astype(BF16), b_gate[l], p_a[l].astype(BF16), p_b[l].astype(BF16),
                    p_c[l].astype(BF16), w_out[l].astype(BF16), seq, tm_merge, tn_merge)
        w_a = _pad_cols(w_up[l][:, :D_FF], dff_pad).astype(BF16)
        w_b = _pad_cols(w_up[l][:, D_FF:], dff_pad).astype(BF16)
        w_d = jnp.pad(w_down[l], ((0, dff_pad - D_FF), (0, 0))).astype(BF16)
        x2 = _ffn(x2, mod, g_norm2[l][None], w_a, w_b, _pad_cols(conv_w[l], dff_pad),
                  _pad_cols(conv_b[l][None], dff_pad), w_d, seq, tm, tn_ffn)
    return x2.reshape(batch, seq, d)
```

```python
import functools
import math

import jax
import jax.numpy as jnp
from jax import lax
from jax.experimental import pallas as pl
from jax.experimental.pallas import tpu as pltpu

F32 = jnp.float32
BF16 = jnp.bfloat16
I32 = jnp.int32

D_MODEL = 2048
DEPTH = 2
CHUNK = 64
EPS = 1e-6
NEG_INF = -1e30
ROPE_THETA = 500000.0

N_HEADS_A = 8
HEAD_DIM = 128
ROT_DIM = HEAD_DIM // 4
N_IDX_HEADS = 8
IDX_DIM = 64
IDX_ROT_DIM = IDX_DIM // 4
N_IDX_PAIRS = N_IDX_HEADS * IDX_DIM // 128
TOPK_MAX = 256
Q_BLOCK = 128
WIDTH_A = N_HEADS_A * HEAD_DIM

WIDTH_B = D_MODEL // 4
SSM_GROUP = 16
N_SSM_GROUPS = WIDTH_B // SSM_GROUP
SSM_STATE = 64
SSM_BUNDLE = 8
N_SSM_BUNDLES = N_SSM_GROUPS // SSM_BUNDLE
BUNDLE_CH = SSM_BUNDLE * SSM_GROUP
BUNDLE_ST = SSM_BUNDLE * SSM_STATE

WIDTH_C = D_MODEL // 4
POOL_WINDOWS = (2, 4, 8, 16)
POOL_GROUP = WIDTH_C // 4

D_FF = 5504
CONV_WIDTH = 3

LANES = 128
INT_MIN = -(2 ** 31)

COL_K = WIDTH_A
COL_V = COL_K + HEAD_DIM
COL_QI = COL_V + HEAD_DIM
COL_KW = COL_QI + N_IDX_HEADS * IDX_DIM
COL_U = COL_KW + LANES
COL_P = COL_U + WIDTH_B
D_IN_PAD = COL_P + WIDTH_C
KW_USED = IDX_DIM + N_IDX_HEADS

VMEM_LIMIT = 56 * 1024 * 1024


def _cparams(sem):
    return pltpu.CompilerParams(dimension_semantics=sem, vmem_limit_bytes=VMEM_LIMIT)


def _norm_mod(x, g, sc, sh):
    ms = jnp.mean(x * x, axis=-1, keepdims=True)
    y = x * lax.rsqrt(ms + EPS)
    return (y * g) * (1.0 + sc) + sh


def _rope(x, c, sa, sb, half):
    return x * c + pltpu.roll(x, LANES - half, 1) * sa + pltpu.roll(x, half, 1) * sb


def _ada_kernel(c_ref, w_ref, b_ref, o_ref):
    c = c_ref[...]
    ca = c * jax.nn.sigmoid(c)
    o_ref[0] = jnp.dot(ca, w_ref[0], preferred_element_type=F32) + b_ref[0]


def _ada(c, w_ada, b_ada, tn=1024):
    depth, d, n = w_ada.shape
    b = c.shape[0]
    return pl.pallas_call(
        _ada_kernel,
        grid=(depth, n // tn),
        in_specs=[
            pl.BlockSpec((b, d), lambda l, j: (0, 0)),
            pl.BlockSpec((1, d, tn), lambda l, j: (l, 0, j)),
            pl.BlockSpec((1, 1, tn), lambda l, j: (l, 0, j)),
        ],
        out_specs=pl.BlockSpec((1, b, tn), lambda l, j: (l, 0, j)),
        out_shape=jax.ShapeDtypeStruct((depth, b, n), F32),
        compiler_params=_cparams(("arbitrary", "arbitrary")),
        name="ada",
    )(c, w_ada, b_ada.reshape(depth, 1, n))


def _proj_kernel(x_ref, mod_ref, g_ref, w_ref, gq_ref, gk_ref, cq_ref, saq_ref, sbq_ref,
                 ci_ref, sai_ref, sbi_ref,
                 h_ref, q_ref, k_ref, v_ref, qi_ref, ki2_ref, kw_ref, u_ref, p_ref):
    h = _norm_mod(x_ref[...], g_ref[...], mod_ref[0, 1:2, :], mod_ref[0, 0:1, :]).astype(BF16)
    h_ref[...] = h

    def mm(c0, width):
        return jnp.dot(h, w_ref[:, c0:c0 + width], preferred_element_type=F32)

    def qk_norm_rope(xh, g):
        ms = jnp.mean(xh * xh, axis=-1, keepdims=True)
        y = xh * lax.rsqrt(ms + EPS) * g
        return _rope(y, cq_ref[...], saq_ref[...], sbq_ref[...], ROT_DIM // 2)

    def store_stacked(ref, slab, idx, nslab):
        for qbl in range(slab.shape[0] // Q_BLOCK):
            r0 = (qbl * nslab + idx) * Q_BLOCK
            ref[r0:r0 + Q_BLOCK, :] = slab[qbl * Q_BLOCK:(qbl + 1) * Q_BLOCK]

    q_all = mm(0, WIDTH_A)
    for hd in range(N_HEADS_A):
        qh = qk_norm_rope(q_all[:, hd * HEAD_DIM:(hd + 1) * HEAD_DIM], gq_ref[...]).astype(BF16)
        store_stacked(q_ref, qh, hd, N_HEADS_A)
    kv = mm(COL_K, 2 * HEAD_DIM)
    k_ref[...] = qk_norm_rope(kv[:, 0:HEAD_DIM], gk_ref[...]).astype(BF16)
    v_ref[...] = kv[:, HEAD_DIM:2 * HEAD_DIM].astype(BF16)
    qi_all = mm(COL_QI, N_IDX_PAIRS * LANES)
    for s in range(N_IDX_PAIRS):
        qi = qi_all[:, s * LANES:(s + 1) * LANES]
        qi = _rope(qi, ci_ref[...], sai_ref[...], sbi_ref[...], IDX_ROT_DIM // 2).astype(BF16)
        store_stacked(qi_ref, qi, s, N_IDX_PAIRS)
    rest = mm(COL_KW, LANES + WIDTH_B + WIDTH_C)
    kw = rest[:, 0:LANES]
    ki = _rope(kw, ci_ref[...], sai_ref[...], sbi_ref[...], IDX_ROT_DIM // 2)
    lane = lax.broadcasted_iota(I32, ki.shape, 1)
    ki_lo = jnp.where(lane < IDX_DIM, ki, 0.0)
    ki2_ref[:, 0:LANES] = ki_lo.astype(BF16)
    ki2_ref[:, LANES:2 * LANES] = pltpu.roll(ki_lo, IDX_DIM, 1).astype(BF16)
    kw_ref[...] = kw * (N_IDX_HEADS ** -0.5 * IDX_DIM ** -0.5)
    u_ref[...] = rest[:, LANES:LANES + WIDTH_B]
    p_ref[...] = rest[:, LANES + WIDTH_B:]


def _proj(x2, mod, g1, w_in_pad, g_q, g_k, tabs_q, tabs_i, batch, seq, tm):
    n, d = x2.shape
    tpb = seq // tm
    row = lambda i: (i, 0)
    const = lambda i: (0, 0)
    tab_spec = pl.BlockSpec((tm, LANES), row)
    return pl.pallas_call(
        _proj_kernel,
        grid=(n // tm,),
        in_specs=[
            pl.BlockSpec((tm, d), row),
            pl.BlockSpec((1, 6, d), lambda i: (i // tpb, 0, 0)),
            pl.BlockSpec((1, d), const),
            pl.BlockSpec((d, D_IN_PAD), const),
            pl.BlockSpec((1, HEAD_DIM), const),
            pl.BlockSpec((1, HEAD_DIM), const),
            tab_spec, tab_spec, tab_spec, tab_spec, tab_spec, tab_spec,
        ],
        out_specs=[
            pl.BlockSpec((tm, d), row),
            pl.BlockSpec((tm * N_HEADS_A, HEAD_DIM), row),
            pl.BlockSpec((tm, HEAD_DIM), row),
            pl.BlockSpec((tm, HEAD_DIM), row),
            pl.BlockSpec((tm * N_IDX_PAIRS, LANES), row),
            pl.BlockSpec((tm, 2 * LANES), row),
            pl.BlockSpec((tm, LANES), row),
            pl.BlockSpec((tm, WIDTH_B), lambda i: (i % tpb, i // tpb)),
            pl.BlockSpec((tm, WIDTH_C), row),
        ],
        out_shape=[
            jax.ShapeDtypeStruct((n, d), BF16),
            jax.ShapeDtypeStruct((n * N_HEADS_A, HEAD_DIM), BF16),
            jax.ShapeDtypeStruct((n, HEAD_DIM), BF16),
            jax.ShapeDtypeStruct((n, HEAD_DIM), BF16),
            jax.ShapeDtypeStruct((n * N_IDX_PAIRS, LANES), BF16),
            jax.ShapeDtypeStruct((n, 2 * LANES), BF16),
            jax.ShapeDtypeStruct((n, LANES), F32),
            jax.ShapeDtypeStruct((seq, batch * WIDTH_B), F32),
            jax.ShapeDtypeStruct((n, WIDTH_C), F32),
        ],
        compiler_params=_cparams(("arbitrary",)),
        name="proj",
    )(x2, mod, g1, w_in_pad, g_q, g_k, *tabs_q, *tabs_i)


KEY_SLABS = 4
KEY_BLOCK = KEY_SLABS * LANES


def _dsa_scores(qi_ref, ki2_ref, kw_ref, s_ref, nqb):
    row = lax.broadcasted_iota(I32, (Q_BLOCK, KEY_BLOCK), 0)
    lane = lax.broadcasted_iota(I32, (Q_BLOCK, KEY_BLOCK), 1)
    chunk_end = (row // CHUNK + 1) * CHUNK
    trans_b = (((1,), (1,)), ((), ()))

    def qb_body(qq, carry):
        r0 = pl.multiple_of(qq * Q_BLOCK, Q_BLOCK)
        w = kw_ref[pl.ds(r0, Q_BLOCK), :]
        wb = [jnp.broadcast_to(w[:, IDX_DIM + hd:IDX_DIM + hd + 1], (Q_BLOCK, KEY_BLOCK))
              for hd in range(N_IDX_HEADS)]
        limit_row = r0 + chunk_end

        def key_body(j, c2):
            k0 = pl.multiple_of(j * KEY_BLOCK, KEY_BLOCK)
            kia = ki2_ref[pl.ds(k0, KEY_BLOCK), 0:LANES]
            kib = ki2_ref[pl.ds(k0, KEY_BLOCK), LANES:2 * LANES]
            qp = qi_ref[pl.ds(pl.multiple_of(r0 * N_IDX_PAIRS, Q_BLOCK), Q_BLOCK * N_IDX_PAIRS), :]
            sa = lax.dot_general(qp, kia, trans_b, preferred_element_type=F32)
            sb = lax.dot_general(qp, kib, trans_b, preferred_element_type=F32)
            sc = jnp.zeros((Q_BLOCK, KEY_BLOCK), F32)
            for pr in range(N_IDX_PAIRS):
                rows = slice(pr * Q_BLOCK, (pr + 1) * Q_BLOCK)
                sc = sc + jnp.maximum(sa[rows], 0.0) * wb[2 * pr] + jnp.maximum(sb[rows], 0.0) * wb[2 * pr + 1]
            adm = ((k0 + lane) < limit_row) & (sc > NEG_INF * 0.5)
            bits = pltpu.bitcast(sc, I32)
            bits = jnp.where(bits == INT_MIN, 0, bits)
            skey = bits ^ ((bits >> 31) & 0x7FFFFFFF)
            skey = jnp.where(adm, skey, INT_MIN)
            for sl in range(KEY_SLABS):
                s_ref[j * KEY_SLABS + sl, pl.ds(r0, Q_BLOCK), :] = skey[:, sl * LANES:(sl + 1) * LANES]
            return c2

        nkb = (r0 + Q_BLOCK + KEY_BLOCK - 1) // KEY_BLOCK
        lax.fori_loop(0, nkb, key_body, 0)
        return carry

    lax.fori_loop(0, nqb, qb_body, 0)


def _dsa_threshold(s_ref, thr_ref, cnt_ref, keep_ref, nqb, topk):
    thr_ref[...] = jnp.full(thr_ref.shape, INT_MIN, I32)

    ones = jnp.ones((LANES, LANES), BF16)

    def count_ge(offset):
        for qq in range(nqb):
            rows = slice(qq * Q_BLOCK, (qq + 1) * Q_BLOCK)
            cand = thr_ref[rows, :] + offset
            acc = jnp.zeros((Q_BLOCK, LANES), F32)
            for s in range(qq + 1):
                acc = acc + jnp.where(s_ref[s, rows, :] >= cand, 1.0, 0.0)
            cnt_ref[rows, :] = acc.astype(BF16)
        return jnp.dot(cnt_ref[...], ones, preferred_element_type=F32)

    def bit_body(i, carry):
        bitval = jnp.left_shift(jnp.int32(1), 31 - i)
        cnt = count_ge(bitval)
        thr = thr_ref[...]
        thr_ref[...] = jnp.where(cnt >= float(topk), thr + bitval, thr)
        return carry

    lax.fori_loop(0, 32, bit_body, 0)

    n_ge = count_ge(0)
    keep_ref[...] = float(topk) - count_ge(1)
    surplus = (n_ge > float(topk)) & (thr_ref[...] > INT_MIN)

    @pl.when(jnp.max(jnp.where(surplus, 1.0, 0.0)) > 0.0)
    def _():
        li = lax.broadcasted_iota(I32, (LANES, LANES), 0)
        lj = lax.broadcasted_iota(I32, (LANES, LANES), 1)
        upper = jnp.where(li <= lj, 1.0, 0.0).astype(BF16)

        def qb_body(qq, carry):
            r0 = pl.multiple_of(qq * Q_BLOCK, Q_BLOCK)
            thr = thr_ref[pl.ds(r0, Q_BLOCK), :]
            keep = keep_ref[pl.ds(r0, Q_BLOCK), :]

            def slab_body(s, seen):
                sk = s_ref[s, pl.ds(r0, Q_BLOCK), :]
                tie = sk == thr
                tie_b = jnp.where(tie, 1.0, 0.0).astype(BF16)
                rank = seen + jnp.dot(tie_b, upper, preferred_element_type=F32)
                drop = tie & (rank > keep) & (thr > INT_MIN)
                s_ref[s, pl.ds(r0, Q_BLOCK), :] = jnp.where(drop, thr - 1, sk)
                return seen + jnp.dot(tie_b, ones, preferred_element_type=F32)

            lax.fori_loop(0, qq + 1, slab_body, jnp.zeros((Q_BLOCK, LANES), F32))
            return carry

        lax.fori_loop(0, nqb, qb_body, 0)


def _dsa_kernel(q_ref, k_ref, v_ref, qi_ref, ki2_ref, kw_ref, o_ref,
                s_ref, thr_ref, cnt_ref, keep_ref, bias_ref, lg_ref, p_ref, acc_ref, vx_ref, *m_refs,
                topk, nqb):
    qb = pl.program_id(1)

    @pl.when(qb == 0)
    def _():
        _dsa_scores(qi_ref, ki2_ref, kw_ref, s_ref, nqb)
        _dsa_threshold(s_ref, thr_ref, cnt_ref, keep_ref, nqb, topk)
        vx_ref[:, 0:HEAD_DIM] = v_ref[...]
        vx_ref[:, HEAD_DIM:] = jnp.ones((vx_ref.shape[0], LANES), BF16)

    r0 = pl.multiple_of(qb * Q_BLOCK, Q_BLOCK)
    nkb = (r0 + Q_BLOCK + KEY_BLOCK - 1) // KEY_BLOCK
    trans_b = (((1,), (1,)), ((), ()))
    thr = thr_ref[pl.ds(r0, Q_BLOCK), :]

    def bias_body(j, carry):
        for sl in range(KEY_SLABS):
            sk = s_ref[j * KEY_SLABS + sl, pl.ds(r0, Q_BLOCK), :]
            sel = (sk >= thr) & (sk > INT_MIN)
            bias_ref[j * KEY_SLABS + sl] = jnp.where(sel, 0.0, NEG_INF)
        return carry

    lax.fori_loop(0, nkb, bias_body, 0)

    for hd in range(N_HEADS_A):
        m_refs[hd][...] = jnp.full((Q_BLOCK, LANES), NEG_INF, F32)
    acc_ref[...] = jnp.zeros(acc_ref.shape, F32)

    def max_body(j, carry):
        k0 = pl.multiple_of(j * KEY_BLOCK, KEY_BLOCK)
        s = lax.dot_general(q_ref[...], k_ref[pl.ds(k0, KEY_BLOCK), :], trans_b, preferred_element_type=F32)
        for hd in range(N_HEADS_A):
            rows = slice(hd * Q_BLOCK, (hd + 1) * Q_BLOCK)
            m = m_refs[hd][...]
            for sl in range(KEY_SLABS):
                sh = s[rows, sl * LANES:(sl + 1) * LANES] + bias_ref[j * KEY_SLABS + sl]
                lg_ref[j * KEY_SLABS + sl, rows, :] = sh
                m = jnp.maximum(m, sh)
            m_refs[hd][...] = m
        return carry

    lax.fori_loop(0, nkb, max_body, 0)
    for hd in range(N_HEADS_A):
        m_refs[hd][...] = jnp.broadcast_to(jnp.max(m_refs[hd][...], axis=1, keepdims=True), (Q_BLOCK, LANES))

    c_exp = HEAD_DIM ** -0.5 * math.log2(math.e)

    def sum_body(j, carry):
        k0 = pl.multiple_of(j * KEY_BLOCK, KEY_BLOCK)
        for hd in range(N_HEADS_A):
            rows = slice(hd * Q_BLOCK, (hd + 1) * Q_BLOCK)
            m = m_refs[hd][...]
            for sl in range(KEY_SLABS):
                p_s = jnp.exp2((lg_ref[j * KEY_SLABS + sl, rows, :] - m) * c_exp)
                p_ref[rows, sl * LANES:(sl + 1) * LANES] = p_s.astype(BF16)
        acc_ref[...] += jnp.dot(p_ref[...], vx_ref[pl.ds(k0, KEY_BLOCK), :], preferred_element_type=F32)
        return carry

    lax.fori_loop(0, nkb, sum_body, 0)
    for hd in range(N_HEADS_A):
        rows = slice(hd * Q_BLOCK, (hd + 1) * Q_BLOCK)
        o_ref[:, hd * HEAD_DIM:(hd + 1) * HEAD_DIM] = (
            acc_ref[rows, 0:HEAD_DIM] / acc_ref[rows, HEAD_DIM:]).astype(BF16)


def _dsa(q, k, v, qi, ki2, kw, batch, seq, topk):
    assert seq % KEY_BLOCK == 0
    n = k.shape[0]
    nqb = seq // Q_BLOCK
    qrow = lambda b, i: (b * nqb + i, 0)
    brow = lambda b, i: (b, 0)
    nsl = seq // LANES
    head_scratch = [pltpu.VMEM((Q_BLOCK, LANES), F32)] * N_HEADS_A
    return pl.pallas_call(
        functools.partial(_dsa_kernel, topk=topk, nqb=nqb),
        grid=(batch, nqb),
        in_specs=[
            pl.BlockSpec((Q_BLOCK * N_HEADS_A, HEAD_DIM), qrow),
            pl.BlockSpec((seq, HEAD_DIM), brow),
            pl.BlockSpec((seq, HEAD_DIM), brow),
            pl.BlockSpec((seq * N_IDX_PAIRS, LANES), brow),
            pl.BlockSpec((seq, 2 * LANES), brow),
            pl.BlockSpec((seq, LANES), brow),
        ],
        out_specs=pl.BlockSpec((Q_BLOCK, WIDTH_A), qrow),
        out_shape=jax.ShapeDtypeStruct((n, WIDTH_A), BF16),
        scratch_shapes=[
            pltpu.VMEM((nsl, seq, LANES), I32),
            pltpu.VMEM((seq, LANES), I32),
            pltpu.VMEM((seq, LANES), BF16),
            pltpu.VMEM((seq, LANES), F32),
            pltpu.VMEM((nsl, Q_BLOCK, LANES), F32),
            pltpu.VMEM((nsl, Q_BLOCK * N_HEADS_A, LANES), F32),
            pltpu.VMEM((Q_BLOCK * N_HEADS_A, KEY_BLOCK), BF16),
            pltpu.VMEM((Q_BLOCK * N_HEADS_A, HEAD_DIM + LANES), F32),
            pltpu.VMEM((seq, HEAD_DIM + LANES), BF16),
        ] + head_scratch,
        compiler_params=_cparams(("arbitrary", "arbitrary")),
        name="dsa",
    )(q, k, v, qi, ki2, kw)


def _s5_kernel(u_ref, bm_ref, cm_ref, lre_ref, lim_ref, dsk_ref, wglu_ref, o_ref,
               bu_ref, st_ref, y_ref, *, batch, tl):
    @pl.when(pl.program_id(0) == 0)
    def _():
        st_ref[...] = jnp.zeros(st_ref.shape, F32)

    for gb in range(N_SSM_BUNDLES):
        ch = slice(gb * BUNDLE_CH, (gb + 1) * BUNDLE_CH)
        u_g = u_ref[:, ch]
        bu_ref[...] = jnp.dot(u_g.astype(BF16), bm_ref[gb], preferred_element_type=F32)
        lam_re = jnp.broadcast_to(lre_ref[gb], (batch, BUNDLE_ST))
        lam_im = jnp.broadcast_to(lim_ref[gb], (batch, BUNDLE_ST))

        def step(t, carry):
            x_re, x_im = carry
            r0 = pl.multiple_of(t * batch, batch)
            n_re = lam_re * x_re - lam_im * x_im + bu_ref[pl.ds(r0, batch), 0:BUNDLE_ST]
            n_im = lam_re * x_im + lam_im * x_re + bu_ref[pl.ds(r0, batch), BUNDLE_ST:2 * BUNDLE_ST]
            bu_ref[pl.ds(r0, batch), 0:BUNDLE_ST] = n_re
            bu_ref[pl.ds(r0, batch), BUNDLE_ST:2 * BUNDLE_ST] = n_im
            return n_re, n_im

        x_re, x_im = lax.fori_loop(
            0, tl, step, (st_ref[gb, :, 0:BUNDLE_ST], st_ref[gb, :, BUNDLE_ST:2 * BUNDLE_ST]))
        st_ref[gb, :, 0:BUNDLE_ST] = x_re
        st_ref[gb, :, BUNDLE_ST:2 * BUNDLE_ST] = x_im
        y = jnp.dot(bu_ref[...].astype(BF16), cm_ref[gb], preferred_element_type=F32)
        y_ref[:, ch] = y + dsk_ref[:, ch] * u_g

    y = jax.nn.gelu(y_ref[...])
    gl = jnp.dot(y.astype(BF16), wglu_ref[...], preferred_element_type=F32)
    o_ref[...] = (y * jax.nn.sigmoid(gl)).astype(BF16)


def _s5_params(a_re, a_im, b_re, b_im, c_re, c_im, log_dt):
    dt = jnp.exp(log_dt)[:, None]
    mag = jnp.exp(a_re * dt)
    lb_re = mag * jnp.cos(a_im * dt)
    lb_im = mag * jnp.sin(a_im * dt)
    nr, ni = lb_re - 1.0, lb_im
    den = a_re * a_re + a_im * a_im
    q_re = (nr * a_re + ni * a_im) / den
    q_im = (ni * a_re - nr * a_im) / den
    bb_re = q_re[..., None] * b_re - q_im[..., None] * b_im
    bb_im = q_re[..., None] * b_im + q_im[..., None] * b_re
    eye = jnp.eye(SSM_BUNDLE, dtype=F32)
    nb = N_SSM_BUNDLES

    def pack_b(m):
        m = m.reshape(nb, SSM_BUNDLE, SSM_STATE, SSM_GROUP)
        return jnp.einsum('bgpi,gh->bgihp', m, eye).reshape(nb, BUNDLE_CH, BUNDLE_ST)

    def pack_c(m):
        m = m.reshape(nb, SSM_BUNDLE, SSM_GROUP, SSM_STATE)
        return jnp.einsum('bgop,gh->bgpho', m, eye).reshape(nb, BUNDLE_ST, BUNDLE_CH)

    bm = jnp.concatenate([pack_b(bb_re), pack_b(bb_im)], axis=2).astype(BF16)
    cm = jnp.concatenate([pack_c(c_re), -pack_c(c_im)], axis=1).astype(BF16)
    return bm, cm, lb_re.reshape(nb, 1, BUNDLE_ST), lb_im.reshape(nb, 1, BUNDLE_ST)


def _s5(u_tm, bm, cm, lre, lim, d_skip, w_glu, batch, seq, tl):
    rows = tl * batch
    const3 = lambda t: (0, 0, 0)
    const2 = lambda t: (0, 0)
    return pl.pallas_call(
        functools.partial(_s5_kernel, batch=batch, tl=tl),
        grid=(seq // tl,),
        in_specs=[
            pl.BlockSpec((rows, WIDTH_B), lambda t: (t, 0)),
            pl.BlockSpec(bm.shape, const3),
            pl.BlockSpec(cm.shape, const3),
            pl.BlockSpec(lre.shape, const3),
            pl.BlockSpec(lim.shape, const3),
            pl.BlockSpec((1, WIDTH_B), const2),
            pl.BlockSpec((WIDTH_B, WIDTH_B), const2),
        ],
        out_specs=pl.BlockSpec((rows, WIDTH_B), lambda t: (t, 0)),
        out_shape=jax.ShapeDtypeStruct((seq * batch, WIDTH_B), BF16),
        scratch_shapes=[
            pltpu.VMEM((rows, 2 * BUNDLE_ST), F32),
            pltpu.VMEM((N_SSM_BUNDLES, batch, 2 * BUNDLE_ST), F32),
            pltpu.VMEM((rows, WIDTH_B), F32),
        ],
        compiler_params=_cparams(("arbitrary",)),
        name="s5",
    )(u_tm, bm, cm, lre, lim, d_skip, w_glu)


def _pool_kernel(p_ref, w_ref, sc_ref, o_ref):
    seq = p_ref.shape[0]
    t = lax.broadcasted_iota(I32, (seq, POOL_GROUP), 0)
    t1 = (t + 1).astype(F32)
    for g, win in enumerate(POOL_WINDOWS):
        ch = slice(g * POOL_GROUP, (g + 1) * POOL_GROUP)
        x = p_ref[:, ch]
        s = x
        sh = 1
        while sh < win:
            s = s + jnp.where(t >= sh, pltpu.roll(s, sh, 0), 0.0)
            sh *= 2
        pooled = s / jnp.minimum(t1, float(win)) - x
        y = jnp.dot(pooled.astype(BF16), w_ref[g], preferred_element_type=F32)
        o_ref[:, ch] = (y * sc_ref[:, ch]).astype(BF16)


def _pool(p, w_pool, pool_scale, batch, seq):
    return pl.pallas_call(
        _pool_kernel,
        grid=(batch,),
        in_specs=[
            pl.BlockSpec((seq, WIDTH_C), lambda b: (b, 0)),
            pl.BlockSpec(w_pool.shape, lambda b: (0, 0, 0)),
            pl.BlockSpec((1, WIDTH_C), lambda b: (0, 0)),
        ],
        out_specs=pl.BlockSpec((seq, WIDTH_C), lambda b: (b, 0)),
        out_shape=jax.ShapeDtypeStruct(p.shape, BF16),
        compiler_params=_cparams(("arbitrary",)),
        name="pool",
    )(p, w_pool, pool_scale)


def _col_tiles(w, tn):
    *lead, k, n = w.shape
    nl = len(lead)
    return w.reshape(*lead, k, n // tn, tn).transpose(*range(nl), nl + 1, nl, nl + 2)


def _merge_kernel(h_ref, xc_ref, oa_ref, ob_ref, oc_ref, wg_ref, bg_ref,
                  pa_ref, pb_ref, pc_ref, wo_ref, gt_ref, o_ref, mg_ref, *, ntiles, tn):
    i = pl.program_id(0)
    j = pl.program_id(1)
    slot = i % 2

    @pl.when(i < ntiles)
    def _():
        h = h_ref[...]
        branches = ((oa_ref, pa_ref), (ob_ref, pb_ref), (oc_ref, pc_ref))
        merged = None
        for b, (o_r, p_r) in enumerate(branches):
            gate = jax.nn.sigmoid(jnp.dot(h, wg_ref[b, 0], preferred_element_type=F32) + bg_ref[0, b:b + 1, :])
            term = gate * jnp.dot(o_r[...], p_r[0], preferred_element_type=F32)
            merged = term if merged is None else merged + term
        c0 = pl.multiple_of(j * tn, tn)
        mg_ref[slot, :, pl.ds(c0, tn)] = merged.astype(BF16)

    @pl.when(i >= 1)
    def _():
        mix = jnp.dot(mg_ref[1 - slot], wo_ref[0], preferred_element_type=F32)
        o_ref[...] = xc_ref[...] + gt_ref[0] * mix


def _merge(x2, h, gt, o_a, o_b_tm, o_c, wg, bg, p_a, p_b, p_c, w_out, seq, tm, tn):
    n, d = x2.shape
    nj = d // tn
    ntiles = n // tm
    tpb = seq // tm
    cur = lambda i: jnp.minimum(i, ntiles - 1)
    prev = lambda i: jnp.maximum(i - 1, 0)
    row = lambda i, j: (cur(i), 0)
    wt = lambda i, j: (j, 0, 0)
    return pl.pallas_call(
        functools.partial(_merge_kernel, ntiles=ntiles, tn=tn),
        grid=(ntiles + 1, nj),
        in_specs=[
            pl.BlockSpec((tm, d), row),
            pl.BlockSpec((tm, tn), lambda i, j: (prev(i), j)),
            pl.BlockSpec((tm, WIDTH_A), row),
            pl.BlockSpec((tm, WIDTH_B), lambda i, j: (cur(i) % tpb, cur(i) // tpb)),
            pl.BlockSpec((tm, WIDTH_C), row),
            pl.BlockSpec((3, 1, d, tn), lambda i, j: (0, j, 0, 0)),
            pl.BlockSpec((1, 3, tn), wt),
            pl.BlockSpec((1, WIDTH_A, tn), wt),
            pl.BlockSpec((1, WIDTH_B, tn), wt),
            pl.BlockSpec((1, WIDTH_C, tn), wt),
            pl.BlockSpec((1, d, tn), wt),
            pl.BlockSpec((1, 1, tn), lambda i, j: (prev(i) // tpb, 0, j)),
        ],
        out_specs=pl.BlockSpec((tm, tn), lambda i, j: (prev(i), jnp.where(i == 0, 0, j))),
        out_shape=jax.ShapeDtypeStruct((n, d), F32),
        scratch_shapes=[pltpu.VMEM((2, tm, d), BF16)],
        compiler_params=_cparams(("arbitrary", "arbitrary")),
        name="merge",
    )(h, x2, o_a, o_b_tm, o_c, _col_tiles(wg, tn), _col_tiles(bg, tn), _col_tiles(p_a, tn),
      _col_tiles(p_b, tn), _col_tiles(p_c, tn), _col_tiles(w_out, tn), gt)


FFN_HALO = 16


def _ffn_kernel(x_ref, xh_ref, mod_ref, g_ref, wa_ref, wb_ref, cw_ref, cb_ref, wd_ref, o_ref,
                h_ref, acc_ref, *, tm, tpb):
    i = pl.program_id(0)
    j = pl.program_id(1)
    nj = pl.num_programs(1)

    @pl.when(j == 0)
    def _():
        g, sc, sh = g_ref[...], mod_ref[0, 4:5, :], mod_ref[0, 3:4, :]
        halo = _norm_mod(xh_ref[...], g, sc, sh)
        h_ref[0:FFN_HALO, :] = jnp.where(i % tpb == 0, 0.0, halo).astype(BF16)
        h_ref[FFN_HALO:, :] = _norm_mod(x_ref[...], g, sc, sh).astype(BF16)
        acc_ref[...] = jnp.zeros(acc_ref.shape, F32)

    a = jnp.dot(h_ref[...], wa_ref[0], preferred_element_type=F32)
    b = jnp.dot(h_ref[FFN_HALO:, :], wb_ref[0], preferred_element_type=F32)
    a_conv = cb_ref[...] + a[FFN_HALO - 2:FFN_HALO - 2 + tm] * cw_ref[0:1, :]
    a_conv = a_conv + a[FFN_HALO - 1:FFN_HALO - 1 + tm] * cw_ref[1:2, :]
    a_conv = a_conv + a[FFN_HALO:] * cw_ref[2:3, :]
    act = (a_conv * jax.nn.sigmoid(a_conv)) * b
    acc_ref[...] += jnp.dot(act.astype(BF16), wd_ref[...], preferred_element_type=F32)

    @pl.when(j == nj - 1)
    def _():
        o_ref[...] = x_ref[...] + mod_ref[0, 5:6, :] * acc_ref[...]


def _ffn(x2, mod, g2, w_a, w_b, conv_w, conv_b, w_down, seq, tm, tn):
    n, d = x2.shape
    dff = w_a.shape[1]
    tpb = seq // tm
    hb = tm // FFN_HALO
    return pl.pallas_call(
        functools.partial(_ffn_kernel, tm=tm, tpb=tpb),
        grid=(n // tm, dff // tn),
        in_specs=[
            pl.BlockSpec((tm, d), lambda i, j: (i, 0)),
            pl.BlockSpec((FFN_HALO, d), lambda i, j: (jnp.maximum(i * hb - 1, 0), 0)),
            pl.BlockSpec((1, 6, d), lambda i, j: (i // tpb, 0, 0)),
            pl.BlockSpec((1, d), lambda i, j: (0, 0)),
            pl.BlockSpec((1, d, tn), lambda i, j: (j, 0, 0)),
            pl.BlockSpec((1, d, tn), lambda i, j: (j, 0, 0)),
            pl.BlockSpec((CONV_WIDTH, tn), lambda i, j: (0, j)),
            pl.BlockSpec((1, tn), lambda i, j: (0, j)),
            pl.BlockSpec((tn, d), lambda i, j: (j, 0)),
        ],
        out_specs=pl.BlockSpec((tm, d), lambda i, j: (i, 0)),
        out_shape=jax.ShapeDtypeStruct((n, d), F32),
        scratch_shapes=[pltpu.VMEM((tm + FFN_HALO, d), BF16), pltpu.VMEM((tm, d), F32)],
        compiler_params=_cparams(("arbitrary", "arbitrary")),
        name="ffn",
    )(x2, x2, mod, g2, _col_tiles(w_a, tn), _col_tiles(w_b, tn), conv_w, conv_b, w_down)


def _rope_tables(positions, rot_dim, period):
    half = rot_dim // 2
    inv_freq = ROPE_THETA ** (-jnp.arange(half, dtype=F32) * (2.0 / rot_dim))
    ang = positions.astype(F32)[..., None] * inv_freq
    cos, sin = jnp.cos(ang), jnp.sin(ang)
    rest = period - 2 * half
    ones = jnp.ones(cos.shape[:-1] + (rest,), F32)
    zh = jnp.zeros_like(sin)
    zr = jnp.zeros_like(ones)
    reps = LANES // period
    out = []
    for parts in ((cos, cos, ones), (-sin, zh, zr), (zh, sin, zr)):
        t = jnp.concatenate(parts, axis=-1)
        out.append(jnp.tile(t, (1, 1, reps)).reshape(-1, LANES))
    return out


def _pad_cols(w, new):
    return jnp.pad(w, ((0, 0), (0, new - w.shape[1])))


def kernel(x, c, positions, w_ada, b_ada, g_norm1, g_norm2, w_in, g_q, g_k, a_re, a_im, b_re, b_im,
           c_re, c_im, d_skip, log_dt, w_glu, w_pool, pool_scale, p_a, p_b, p_c, w_gate, b_gate,
           w_out, w_up, conv_w, conv_b, w_down, *, tm=512, tm_merge=1024, tn_merge=256, tn_ffn=512, tl=32):
    batch, seq, d = x.shape
    depth = w_ada.shape[0]
    n = batch * seq
    tm = min(tm, seq)
    tm_merge = min(tm_merge, seq)
    tl = min(tl, seq)
    topk = min(TOPK_MAX, seq // 4)
    dff_pad = -(-D_FF // tn_ffn) * tn_ffn

    mod_all = _ada(c, w_ada, b_ada).reshape(depth, batch, 6, d)
    tabs_q = _rope_tables(positions, ROT_DIM, HEAD_DIM)
    tabs_i = _rope_tables(positions, IDX_ROT_DIM, IDX_DIM)

    x2 = x.reshape(n, d)
    for l in range(depth):
        mod = mod_all[l]
        gt1 = mod[:, 2:3, :]
        w_in_pad = jnp.concatenate(
            [w_in[l][:, :COL_KW + KW_USED],
             jnp.zeros((d, LANES - KW_USED), F32),
             w_in[l][:, COL_KW + KW_USED:]], axis=1).astype(BF16)
        h, q, k, v, qi, ki2, kw, u_tm, p = _proj(
            x2, mod, g_norm1[l][None], w_in_pad, g_q[l][None], g_k[l][None], tabs_q, tabs_i,
            batch, seq, tm)
        o_a = _dsa(q, k, v, qi, ki2, kw, batch, seq, topk)
        bm, cm, lre, lim = _s5_params(a_re[l], a_im[l], b_re[l], b_im[l], c_re[l], c_im[l], log_dt[l])
        o_b = _s5(u_tm.reshape(seq * batch, WIDTH_B), bm, cm, lre, lim, d_skip[l][None],
                  w_glu[l].astype(BF16), batch, seq, tl)
        o_c = _pool(p, w_pool[l].astype(BF16), pool_scale[l][None], batch, seq)
        x2 = _merge(x2, h, gt1, o_a, o_b.reshape(seq, batch * WIDTH_B), o_c,
                    w_gate[l].astype(BF16), b_gate[l], p_a[l].astype(BF16), p_b[l].astype(BF16),
                    p_c[l].astype(BF16), w_out[l].astype(BF16), seq, tm_merge, tn_merge)
        w_a = _pad_cols(w_up[l][:, :D_FF], dff_pad).astype(BF16)
        w_b = _pad_cols(w_up[l][:, D_FF:], dff_pad).astype(BF16)
        w_d = jnp.pad(w_down[l], ((0, dff_pad - D_FF), (0, 0))).astype(BF16)
        x2 = _ffn(x2, mod, g_norm2[l][None], w_a, w_b, _pad_cols(conv_w[l], dff_pad),
                  _pad_cols(conv_b[l][None], dff_pad), w_d, seq, tm, tn_ffn)
    return x2.reshape(batch, seq, d)
```

```python
import functools
import math

import jax
import jax.numpy as jnp
from jax import lax
from jax.experimental import pallas as pl
from jax.experimental.pallas import tpu as pltpu

F32 = jnp.float32
BF16 = jnp.bfloat16
I32 = jnp.int32

D_MODEL = 2048
DEPTH = 2
CHUNK = 64
EPS = 1e-6
NEG_INF = -1e30
ROPE_THETA = 500000.0

N_HEADS_A = 8
HEAD_DIM = 128
ROT_DIM = HEAD_DIM // 4
N_IDX_HEADS = 8
IDX_DIM = 64
IDX_ROT_DIM = IDX_DIM // 4
N_IDX_PAIRS = N_IDX_HEADS * IDX_DIM // 128
TOPK_MAX = 256
Q_BLOCK = 128
WIDTH_A = N_HEADS_A * HEAD_DIM

WIDTH_B = D_MODEL // 4
SSM_GROUP = 16
N_SSM_GROUPS = WIDTH_B // SSM_GROUP
SSM_STATE = 64
SSM_BUNDLE = 8
N_SSM_BUNDLES = N_SSM_GROUPS // SSM_BUNDLE
BUNDLE_CH = SSM_BUNDLE * SSM_GROUP
BUNDLE_ST = SSM_BUNDLE * SSM_STATE

WIDTH_C = D_MODEL // 4
POOL_WINDOWS = (2, 4, 8, 16)
POOL_GROUP = WIDTH_C // 4

D_FF = 5504
CONV_WIDTH = 3

LANES = 128
INT_MIN = -(2 ** 31)

COL_K = WIDTH_A
COL_V = COL_K + HEAD_DIM
COL_QI = COL_V + HEAD_DIM
COL_KW = COL_QI + N_IDX_HEADS * IDX_DIM
COL_U = COL_KW + LANES
COL_P = COL_U + WIDTH_B
D_IN_PAD = COL_P + WIDTH_C
KW_USED = IDX_DIM + N_IDX_HEADS

VMEM_LIMIT = 56 * 1024 * 1024


def _cparams(sem):
    return pltpu.CompilerParams(dimension_semantics=sem, vmem_limit_bytes=VMEM_LIMIT)


def _norm_mod(x, g, sc, sh):
    ms = jnp.mean(x * x, axis=-1, keepdims=True)
    y = x * lax.rsqrt(ms + EPS)
    return (y * g) * (1.0 + sc) + sh


def _rope(x, c, sa, sb, half):
    return x * c + pltpu.roll(x, LANES - half, 1) * sa + pltpu.roll(x, half, 1) * sb


def _ada_kernel(c_ref, w_ref, b_ref, o_ref):
    c = c_ref[...]
    ca = c * jax.nn.sigmoid(c)
    o_ref[0] = jnp.dot(ca, w_ref[0], preferred_element_type=F32) + b_ref[0]


def _ada(c, w_ada, b_ada, tn=1024):
    depth, d, n = w_ada.shape
    b = c.shape[0]
    return pl.pallas_call(
        _ada_kernel,
        grid=(depth, n // tn),
        in_specs=[
            pl.BlockSpec((b, d), lambda l, j: (0, 0)),
            pl.BlockSpec((1, d, tn), lambda l, j: (l, 0, j)),
            pl.BlockSpec((1, 1, tn), lambda l, j: (l, 0, j)),
        ],
        out_specs=pl.BlockSpec((1, b, tn), lambda l, j: (l, 0, j)),
        out_shape=jax.ShapeDtypeStruct((depth, b, n), F32),
        compiler_params=_cparams(("arbitrary", "arbitrary")),
        name="ada",
    )(c, w_ada, b_ada.reshape(depth, 1, n))


def _proj_kernel(x_ref, mod_ref, g_ref, w_ref, gq_ref, gk_ref, cq_ref, saq_ref, sbq_ref,
                 ci_ref, sai_ref, sbi_ref,
                 h_ref, q_ref, k_ref, v_ref, qi_ref, ki2_ref, kw_ref, u_ref, p_ref):
    h = _norm_mod(x_ref[...], g_ref[...], mod_ref[0, 1:2, :], mod_ref[0, 0:1, :]).astype(BF16)
    h_ref[...] = h

    def mm(c0, width):
        return jnp.dot(h, w_ref[:, c0:c0 + width], preferred_element_type=F32)

    def qk_norm_rope(xh, g):
        ms = jnp.mean(xh * xh, axis=-1, keepdims=True)
        y = xh * lax.rsqrt(ms + EPS) * g
        return _rope(y, cq_ref[...], saq_ref[...], sbq_ref[...], ROT_DIM // 2)

    def store_stacked(ref, slab, idx, nslab):
        for qbl in range(slab.shape[0] // Q_BLOCK):
            r0 = (qbl * nslab + idx) * Q_BLOCK
            ref[r0:r0 + Q_BLOCK, :] = slab[qbl * Q_BLOCK:(qbl + 1) * Q_BLOCK]

    q_all = mm(0, WIDTH_A)
    for hd in range(N_HEADS_A):
        qh = qk_norm_rope(q_all[:, hd * HEAD_DIM:(hd + 1) * HEAD_DIM], gq_ref[...]).astype(BF16)
        store_stacked(q_ref, qh, hd, N_HEADS_A)
    kv = mm(COL_K, 2 * HEAD_DIM)
    k_ref[...] = qk_norm_rope(kv[:, 0:HEAD_DIM], gk_ref[...]).astype(BF16)
    v_ref[...] = kv[:, HEAD_DIM:2 * HEAD_DIM].astype(BF16)
    qi_all = mm(COL_QI, N_IDX_PAIRS * LANES)
    for s in range(N_IDX_PAIRS):
        qi = qi_all[:, s * LANES:(s + 1) * LANES]
        qi = _rope(qi, ci_ref[...], sai_ref[...], sbi_ref[...], IDX_ROT_DIM // 2).astype(BF16)
        store_stacked(qi_ref, qi, s, N_IDX_PAIRS)
    rest = mm(COL_KW, LANES + WIDTH_B + WIDTH_C)
    kw = rest[:, 0:LANES]
    ki = _rope(kw, ci_ref[...], sai_ref[...], sbi_ref[...], IDX_ROT_DIM // 2)
    lane = lax.broadcasted_iota(I32, ki.shape, 1)
    ki_lo = jnp.where(lane < IDX_DIM, ki, 0.0)
    ki2_ref[:, 0:LANES] = ki_lo.astype(BF16)
    ki2_ref[:, LANES:2 * LANES] = pltpu.roll(ki_lo, IDX_DIM, 1).astype(BF16)
    kw_ref[...] = kw * (N_IDX_HEADS ** -0.5 * IDX_DIM ** -0.5)
    u_ref[...] = rest[:, LANES:LANES + WIDTH_B]
    p_ref[...] = rest[:, LANES + WIDTH_B:]


def _proj(x2, mod, g1, w_in_pad, g_q, g_k, tabs_q, tabs_i, batch, seq, tm):
    n, d = x2.shape
    tpb = seq // tm
    row = lambda i: (i, 0)
    const = lambda i: (0, 0)
    tab_spec = pl.BlockSpec((tm, LANES), row)
    return pl.pallas_call(
        _proj_kernel,
        grid=(n // tm,),
        in_specs=[
            pl.BlockSpec((tm, d), row),
            pl.BlockSpec((1, 6, d), lambda i: (i // tpb, 0, 0)),
            pl.BlockSpec((1, d), const),
            pl.BlockSpec((d, D_IN_PAD), const),
            pl.BlockSpec((1, HEAD_DIM), const),
            pl.BlockSpec((1, HEAD_DIM), const),
            tab_spec, tab_spec, tab_spec, tab_spec, tab_spec, tab_spec,
        ],
        out_specs=[
            pl.BlockSpec((tm, d), row),
            pl.BlockSpec((tm * N_HEADS_A, HEAD_DIM), row),
            pl.BlockSpec((tm, HEAD_DIM), row),
            pl.BlockSpec((tm, HEAD_DIM), row),
            pl.BlockSpec((tm * N_IDX_PAIRS, LANES), row),
            pl.BlockSpec((tm, 2 * LANES), row),
            pl.BlockSpec((tm, LANES), row),
            pl.BlockSpec((tm, WIDTH_B), lambda i: (i % tpb, i // tpb)),
            pl.BlockSpec((tm, WIDTH_C), row),
        ],
        out_shape=[
            jax.ShapeDtypeStruct((n, d), BF16),
            jax.ShapeDtypeStruct((n * N_HEADS_A, HEAD_DIM), BF16),
            jax.ShapeDtypeStruct((n, HEAD_DIM), BF16),
            jax.ShapeDtypeStruct((n, HEAD_DIM), BF16),
            jax.ShapeDtypeStruct((n * N_IDX_PAIRS, LANES), BF16),
            jax.ShapeDtypeStruct((n, 2 * LANES), BF16),
            jax.ShapeDtypeStruct((n, LANES), F32),
            jax.ShapeDtypeStruct((seq, batch * WIDTH_B), F32),
            jax.ShapeDtypeStruct((n, WIDTH_C), F32),
        ],
        compiler_params=_cparams(("arbitrary",)),
        name="proj",
    )(x2, mod, g1, w_in_pad, g_q, g_k, *tabs_q, *tabs_i)


KEY_SLABS = 4
KEY_BLOCK = KEY_SLABS * LANES


def _dsa_scores(qi_ref, ki2_ref, kw_ref, s_ref, nqb):
    row = lax.broadcasted_iota(I32, (Q_BLOCK, KEY_BLOCK), 0)
    lane = lax.broadcasted_iota(I32, (Q_BLOCK, KEY_BLOCK), 1)
    chunk_end = (row // CHUNK + 1) * CHUNK
    trans_b = (((1,), (1,)), ((), ()))

    def qb_body(qq, carry):
        r0 = pl.multiple_of(qq * Q_BLOCK, Q_BLOCK)
        w = kw_ref[pl.ds(r0, Q_BLOCK), :]
        wb = [jnp.broadcast_to(w[:, IDX_DIM + hd:IDX_DIM + hd + 1], (Q_BLOCK, KEY_BLOCK))
              for hd in range(N_IDX_HEADS)]
        limit_row = r0 + chunk_end

        def key_body(j, c2):
            k0 = pl.multiple_of(j * KEY_BLOCK, KEY_BLOCK)
            kia = ki2_ref[pl.ds(k0, KEY_BLOCK), 0:LANES]
            kib = ki2_ref[pl.ds(k0, KEY_BLOCK), LANES:2 * LANES]
            qp = qi_ref[pl.ds(pl.multiple_of(r0 * N_IDX_PAIRS, Q_BLOCK), Q_BLOCK * N_IDX_PAIRS), :]
            sa = lax.dot_general(qp, kia, trans_b, preferred_element_type=F32)
            sb = lax.dot_general(qp, kib, trans_b, preferred_element_type=F32)
            sc = jnp.zeros((Q_BLOCK, KEY_BLOCK), F32)
            for pr in range(N_IDX_PAIRS):
                rows = slice(pr * Q_BLOCK, (pr + 1) * Q_BLOCK)
                sc = sc + jnp.maximum(sa[rows], 0.0) * wb[2 * pr] + jnp.maximum(sb[rows], 0.0) * wb[2 * pr + 1]
            adm = ((k0 + lane) < limit_row) & (sc > NEG_INF * 0.5)
            bits = pltpu.bitcast(sc, I32)
            bits = jnp.where(bits == INT_MIN, 0, bits)
            skey = bits ^ ((bits >> 31) & 0x7FFFFFFF)
            skey = jnp.where(adm, skey, INT_MIN)
            for sl in range(KEY_SLABS):
                s_ref[j * KEY_SLABS + sl, pl.ds(r0, Q_BLOCK), :] = skey[:, sl * LANES:(sl + 1) * LANES]
            return c2

        nkb = (r0 + Q_BLOCK + KEY_BLOCK - 1) // KEY_BLOCK
        lax.fori_loop(0, nkb, key_body, 0)
        return carry

    lax.fori_loop(0, nqb, qb_body, 0)


def _dsa_threshold(s_ref, thr_ref, cnt_ref, keep_ref, surplus_ref, nqb, topk):
    thr_ref[...] = jnp.full(thr_ref.shape, INT_MIN, I32)

    ones = jnp.ones((LANES, LANES), BF16)

    def count_ge(offset):
        for qq in range(nqb):
            rows = slice(qq * Q_BLOCK, (qq + 1) * Q_BLOCK)
            cand = thr_ref[rows, :] + offset
            acc = jnp.zeros((Q_BLOCK, LANES), F32)
            for s in range(qq + 1):
                acc = acc + jnp.where(s_ref[s, rows, :] >= cand, 1.0, 0.0)
            cnt_ref[rows, :] = acc.astype(BF16)
        return jnp.dot(cnt_ref[...], ones, preferred_element_type=F32)

    def bit_body(i, carry):
        bitval = jnp.left_shift(jnp.int32(1), 31 - i)
        cnt = count_ge(bitval)
        thr = thr_ref[...]
        thr_ref[...] = jnp.where(cnt >= float(topk), thr + bitval, thr)
        return carry

    lax.fori_loop(0, 32, bit_body, 0)

    n_ge = count_ge(0)
    keep_ref[...] = float(topk) - count_ge(1)
    surplus_ref[...] = jnp.where((n_ge > float(topk)) & (thr_ref[...] > INT_MIN), 1.0, 0.0)
    li = lax.broadcasted_iota(I32, (LANES, LANES), 0)
    lj = lax.broadcasted_iota(I32, (LANES, LANES), 1)
    upper = jnp.where(li <= lj, 1.0, 0.0).astype(BF16)

    def qb_body(qq, carry):
        r0 = pl.multiple_of(qq * Q_BLOCK, Q_BLOCK)

        @pl.when(jnp.max(surplus_ref[pl.ds(r0, Q_BLOCK), :]) > 0.0)
        def _():
            thr = thr_ref[pl.ds(r0, Q_BLOCK), :]
            keep = keep_ref[pl.ds(r0, Q_BLOCK), :]

            def slab_body(s, seen):
                sk = s_ref[s, pl.ds(r0, Q_BLOCK), :]
                tie = sk == thr
                tie_b = jnp.where(tie, 1.0, 0.0).astype(BF16)
                rank = seen + jnp.dot(tie_b, upper, preferred_element_type=F32)
                drop = tie & (rank > keep) & (thr > INT_MIN)
                s_ref[s, pl.ds(r0, Q_BLOCK), :] = jnp.where(drop, thr - 1, sk)
                return seen + jnp.dot(tie_b, ones, preferred_element_type=F32)

            lax.fori_loop(0, qq + 1, slab_body, jnp.zeros((Q_BLOCK, LANES), F32))

        return carry

    lax.fori_loop(0, nqb, qb_body, 0)


def _dsa_kernel(q_ref, k_ref, v_ref, qi_ref, ki2_ref, kw_ref, o_ref,
                s_ref, thr_ref, cnt_ref, keep_ref, surplus_ref, bias_ref, lg_ref, p_ref, acc_ref, vx_ref, *m_refs,
                topk, nqb):
    qb = pl.program_id(1)

    @pl.when(qb == 0)
    def _():
        _dsa_scores(qi_ref, ki2_ref, kw_ref, s_ref, nqb)
        _dsa_threshold(s_ref, thr_ref, cnt_ref, keep_ref, surplus_ref, nqb, topk)
        vx_ref[:, 0:HEAD_DIM] = v_ref[...]
        vx_ref[:, HEAD_DIM:] = jnp.ones((vx_ref.shape[0], LANES), BF16)

    r0 = pl.multiple_of(qb * Q_BLOCK, Q_BLOCK)
    nkb = (r0 + Q_BLOCK + KEY_BLOCK - 1) // KEY_BLOCK
    trans_b = (((1,), (1,)), ((), ()))
    thr = thr_ref[pl.ds(r0, Q_BLOCK), :]

    def bias_body(j, carry):
        for sl in range(KEY_SLABS):
            sk = s_ref[j * KEY_SLABS + sl, pl.ds(r0, Q_BLOCK), :]
            sel = (sk >= thr) & (sk > INT_MIN)
            bias_ref[j * KEY_SLABS + sl] = jnp.where(sel, 0.0, NEG_INF)
        return carry

    lax.fori_loop(0, nkb, bias_body, 0)

    for hd in range(N_HEADS_A):
        m_refs[hd][...] = jnp.full((Q_BLOCK, LANES), NEG_INF, F32)
    acc_ref[...] = jnp.zeros(acc_ref.shape, F32)

    def max_body(j, carry):
        k0 = pl.multiple_of(j * KEY_BLOCK, KEY_BLOCK)
        s = lax.dot_general(q_ref[...], k_ref[pl.ds(k0, KEY_BLOCK), :], trans_b, preferred_element_type=F32)
        for hd in range(N_HEADS_A):
            rows = slice(hd * Q_BLOCK, (hd + 1) * Q_BLOCK)
            m = m_refs[hd][...]
            for sl in range(KEY_SLABS):
                sh = s[rows, sl * LANES:(sl + 1) * LANES] + bias_ref[j * KEY_SLABS + sl]
                lg_ref[j * KEY_SLABS + sl, rows, :] = sh
                m = jnp.maximum(m, sh)
            m_refs[hd][...] = m
        return carry

    lax.fori_loop(0, nkb, max_body, 0)
    for hd in range(N_HEADS_A):
        m_refs[hd][...] = jnp.broadcast_to(jnp.max(m_refs[hd][...], axis=1, keepdims=True), (Q_BLOCK, LANES))

    c_exp = HEAD_DIM ** -0.5 * math.log2(math.e)

    def sum_body(j, carry):
        k0 = pl.multiple_of(j * KEY_BLOCK, KEY_BLOCK)
        for hd in range(N_HEADS_A):
            rows = slice(hd * Q_BLOCK, (hd + 1) * Q_BLOCK)
            m = m_refs[hd][...]
            for sl in range(KEY_SLABS):
                p_s = jnp.exp2((lg_ref[j * KEY_SLABS + sl, rows, :] - m) * c_exp)
                p_ref[rows, sl * LANES:(sl + 1) * LANES] = p_s.astype(BF16)
        acc_ref[...] += jnp.dot(p_ref[...], vx_ref[pl.ds(k0, KEY_BLOCK), :], preferred_element_type=F32)
        return carry

    lax.fori_loop(0, nkb, sum_body, 0)
    for hd in range(N_HEADS_A):
        rows = slice(hd * Q_BLOCK, (hd + 1) * Q_BLOCK)
        o_ref[:, hd * HEAD_DIM:(hd + 1) * HEAD_DIM] = (
            acc_ref[rows, 0:HEAD_DIM] / acc_ref[rows, HEAD_DIM:]).astype(BF16)


def _dsa(q, k, v, qi, ki2, kw, batch, seq, topk):
    assert seq % KEY_BLOCK == 0
    n = k.shape[0]
    nqb = seq // Q_BLOCK
    qrow = lambda b, i: (b * nqb + i, 0)
    brow = lambda b, i: (b, 0)
    nsl = seq // LANES
    head_scratch = [pltpu.VMEM((Q_BLOCK, LANES), F32)] * N_HEADS_A
    return pl.pallas_call(
        functools.partial(_dsa_kernel, topk=topk, nqb=nqb),
        grid=(batch, nqb),
        in_specs=[
            pl.BlockSpec((Q_BLOCK * N_HEADS_A, HEAD_DIM), qrow),
            pl.BlockSpec((seq, HEAD_DIM), brow),
            pl.BlockSpec((seq, HEAD_DIM), brow),
            pl.BlockSpec((seq * N_IDX_PAIRS, LANES), brow),
            pl.BlockSpec((seq, 2 * LANES), brow),
            pl.BlockSpec((seq, LANES), brow),
        ],
        out_specs=pl.BlockSpec((Q_BLOCK, WIDTH_A), qrow),
        out_shape=jax.ShapeDtypeStruct((n, WIDTH_A), BF16),
        scratch_shapes=[
            pltpu.VMEM((nsl, seq, LANES), I32),
            pltpu.VMEM((seq, LANES), I32),
            pltpu.VMEM((seq, LANES), BF16),
            pltpu.VMEM((seq, LANES), F32),
            pltpu.VMEM((seq, LANES), F32),
            pltpu.VMEM((nsl, Q_BLOCK, LANES), F32),
            pltpu.VMEM((nsl, Q_BLOCK * N_HEADS_A, LANES), F32),
            pltpu.VMEM((Q_BLOCK * N_HEADS_A, KEY_BLOCK), BF16),
            pltpu.VMEM((Q_BLOCK * N_HEADS_A, HEAD_DIM + LANES), F32),
            pltpu.VMEM((seq, HEAD_DIM + LANES), BF16),
        ] + head_scratch,
        compiler_params=_cparams(("arbitrary", "arbitrary")),
        name="dsa",
    )(q, k, v, qi, ki2, kw)


def _s5_kernel(u_ref, bm_ref, cm_ref, lre_ref, lim_ref, dsk_ref, wglu_ref, o_ref,
               bu_ref, st_ref, y_ref, *, batch, tl):
    @pl.when(pl.program_id(0) == 0)
    def _():
        st_ref[...] = jnp.zeros(st_ref.shape, F32)

    for gb in range(N_SSM_BUNDLES):
        ch = slice(gb * BUNDLE_CH, (gb + 1) * BUNDLE_CH)
        u_g = u_ref[:, ch]
        bu_ref[...] = jnp.dot(u_g.astype(BF16), bm_ref[gb], preferred_element_type=F32)
        lam_re = jnp.broadcast_to(lre_ref[gb], (batch, BUNDLE_ST))
        lam_im = jnp.broadcast_to(lim_ref[gb], (batch, BUNDLE_ST))

        def step(t, carry):
            x_re, x_im = carry
            r0 = pl.multiple_of(t * batch, batch)
            n_re = lam_re * x_re - lam_im * x_im + bu_ref[pl.ds(r0, batch), 0:BUNDLE_ST]
            n_im = lam_re * x_im + lam_im * x_re + bu_ref[pl.ds(r0, batch), BUNDLE_ST:2 * BUNDLE_ST]
            bu_ref[pl.ds(r0, batch), 0:BUNDLE_ST] = n_re
            bu_ref[pl.ds(r0, batch), BUNDLE_ST:2 * BUNDLE_ST] = n_im
            return n_re, n_im

        x_re, x_im = lax.fori_loop(
            0, tl, step, (st_ref[gb, :, 0:BUNDLE_ST], st_ref[gb, :, BUNDLE_ST:2 * BUNDLE_ST]))
        st_ref[gb, :, 0:BUNDLE_ST] = x_re
        st_ref[gb, :, BUNDLE_ST:2 * BUNDLE_ST] = x_im
        y = jnp.dot(bu_ref[...].astype(BF16), cm_ref[gb], preferred_element_type=F32)
        y_ref[:, ch] = y + dsk_ref[:, ch] * u_g

    y = jax.nn.gelu(y_ref[...])
    gl = jnp.dot(y.astype(BF16), wglu_ref[...], preferred_element_type=F32)
    o_ref[...] = (y * jax.nn.sigmoid(gl)).astype(BF16)


def _s5_params(a_re, a_im, b_re, b_im, c_re, c_im, log_dt):
    dt = jnp.exp(log_dt)[:, None]
    mag = jnp.exp(a_re * dt)
    lb_re = mag * jnp.cos(a_im * dt)
    lb_im = mag * jnp.sin(a_im * dt)
    nr, ni = lb_re - 1.0, lb_im
    den = a_re * a_re + a_im * a_im
    q_re = (nr * a_re + ni * a_im) / den
    q_im = (ni * a_re - nr * a_im) / den
    bb_re = q_re[..., None] * b_re - q_im[..., None] * b_im
    bb_im = q_re[..., None] * b_im + q_im[..., None] * b_re
    eye = jnp.eye(SSM_BUNDLE, dtype=F32)
    nb = N_SSM_BUNDLES

    def pack_b(m):
        m = m.reshape(nb, SSM_BUNDLE, SSM_STATE, SSM_GROUP)
        return jnp.einsum('bgpi,gh->bgihp', m, eye).reshape(nb, BUNDLE_CH, BUNDLE_ST)

    def pack_c(m):
        m = m.reshape(nb, SSM_BUNDLE, SSM_GROUP, SSM_STATE)
        return jnp.einsum('bgop,gh->bgpho', m, eye).reshape(nb, BUNDLE_ST, BUNDLE_CH)

    bm = jnp.concatenate([pack_b(bb_re), pack_b(bb_im)], axis=2).astype(BF16)
    cm = jnp.concatenate([pack_c(c_re), -pack_c(c_im)], axis=1).astype(BF16)
    return bm, cm, lb_re.reshape(nb, 1, BUNDLE_ST), lb_im.reshape(nb, 1, BUNDLE_ST)


def _s5(u_tm, bm, cm, lre, lim, d_skip, w_glu, batch, seq, tl):
    rows = tl * batch
    const3 = lambda t: (0, 0, 0)
    const2 = lambda t: (0, 0)
    return pl.pallas_call(
        functools.partial(_s5_kernel, batch=batch, tl=tl),
        grid=(seq // tl,),
        in_specs=[
            pl.BlockSpec((rows, WIDTH_B), lambda t: (t, 0)),
            pl.BlockSpec(bm.shape, const3),
            pl.BlockSpec(cm.shape, const3),
            pl.BlockSpec(lre.shape, const3),
            pl.BlockSpec(lim.shape, const3),
            pl.BlockSpec((1, WIDTH_B), const2),
            pl.BlockSpec((WIDTH_B, WIDTH_B), const2),
        ],
        out_specs=pl.BlockSpec((rows, WIDTH_B), lambda t: (t, 0)),
        out_shape=jax.ShapeDtypeStruct((seq * batch, WIDTH_B), BF16),
        scratch_shapes=[
            pltpu.VMEM((rows, 2 * BUNDLE_ST), F32),
            pltpu.VMEM((N_SSM_BUNDLES, batch, 2 * BUNDLE_ST), F32),
            pltpu.VMEM((rows, WIDTH_B), F32),
        ],
        compiler_params=_cparams(("arbitrary",)),
        name="s5",
    )(u_tm, bm, cm, lre, lim, d_skip, w_glu)


def _pool_kernel(p_ref, w_ref, sc_ref, o_ref):
    seq = p_ref.shape[0]
    t = lax.broadcasted_iota(I32, (seq, POOL_GROUP), 0)
    t1 = (t + 1).astype(F32)
    for g, win in enumerate(POOL_WINDOWS):
        ch = slice(g * POOL_GROUP, (g + 1) * POOL_GROUP)
        x = p_ref[:, ch]
        s = x
        sh = 1
        while sh < win:
            s = s + jnp.where(t >= sh, pltpu.roll(s, sh, 0), 0.0)
            sh *= 2
        pooled = s / jnp.minimum(t1, float(win)) - x
        y = jnp.dot(pooled.astype(BF16), w_ref[g], preferred_element_type=F32)
        o_ref[:, ch] = (y * sc_ref[:, ch]).astype(BF16)


def _pool(p, w_pool, pool_scale, batch, seq):
    return pl.pallas_call(
        _pool_kernel,
        grid=(batch,),
        in_specs=[
            pl.BlockSpec((seq, WIDTH_C), lambda b: (b, 0)),
            pl.BlockSpec(w_pool.shape, lambda b: (0, 0, 0)),
            pl.BlockSpec((1, WIDTH_C), lambda b: (0, 0)),
        ],
        out_specs=pl.BlockSpec((seq, WIDTH_C), lambda b: (b, 0)),
        out_shape=jax.ShapeDtypeStruct(p.shape, BF16),
        compiler_params=_cparams(("arbitrary",)),
        name="pool",
    )(p, w_pool, pool_scale)


def _col_tiles(w, tn):
    *lead, k, n = w.shape
    nl = len(lead)
    return w.reshape(*lead, k, n // tn, tn).transpose(*range(nl), nl + 1, nl, nl + 2)


def _merge_kernel(h_ref, xc_ref, oa_ref, ob_ref, oc_ref, wg_ref, bg_ref,
                  pa_ref, pb_ref, pc_ref, wo_ref, gt_ref, o_ref, mg_ref, *, ntiles, tn):
    i = pl.program_id(0)
    j = pl.program_id(1)
    slot = i % 2

    @pl.when(i < ntiles)
    def _():
        h = h_ref[...]
        branches = ((oa_ref, pa_ref), (ob_ref, pb_ref), (oc_ref, pc_ref))
        merged = None
        for b, (o_r, p_r) in enumerate(branches):
            gate = jax.nn.sigmoid(jnp.dot(h, wg_ref[b, 0], preferred_element_type=F32) + bg_ref[0, b:b + 1, :])
            term = gate * jnp.dot(o_r[...], p_r[0], preferred_element_type=F32)
            merged = term if merged is None else merged + term
        c0 = pl.multiple_of(j * tn, tn)
        mg_ref[slot, :, pl.ds(c0, tn)] = merged.astype(BF16)

    @pl.when(i >= 1)
    def _():
        mix = jnp.dot(mg_ref[1 - slot], wo_ref[0], preferred_element_type=F32)
        o_ref[...] = xc_ref[...] + gt_ref[0] * mix


def _merge(x2, h, gt, o_a, o_b_tm, o_c, wg, bg, p_a, p_b, p_c, w_out, seq, tm, tn):
    n, d = x2.shape
    nj = d // tn
    ntiles = n // tm
    tpb = seq // tm
    cur = lambda i: jnp.minimum(i, ntiles - 1)
    prev = lambda i: jnp.maximum(i - 1, 0)
    row = lambda i, j: (cur(i), 0)
    wt = lambda i, j: (j, 0, 0)
    return pl.pallas_call(
        functools.partial(_merge_kernel, ntiles=ntiles, tn=tn),
        grid=(ntiles + 1, nj),
        in_specs=[
            pl.BlockSpec((tm, d), row),
            pl.BlockSpec((tm, tn), lambda i, j: (prev(i), j)),
            pl.BlockSpec((tm, WIDTH_A), row),
            pl.BlockSpec((tm, WIDTH_B), lambda i, j: (cur(i) % tpb, cur(i) // tpb)),
            pl.BlockSpec((tm, WIDTH_C), row),
            pl.BlockSpec((3, 1, d, tn), lambda i, j: (0, j, 0, 0)),
            pl.BlockSpec((1, 3, tn), wt),
            pl.BlockSpec((1, WIDTH_A, tn), wt),
            pl.BlockSpec((1, WIDTH_B, tn), wt),
            pl.BlockSpec((1, WIDTH_C, tn), wt),
            pl.BlockSpec((1, d, tn), wt),
            pl.BlockSpec((1, 1, tn), lambda i, j: (prev(i) // tpb, 0, j)),
        ],
        out_specs=pl.BlockSpec((tm, tn), lambda i, j: (prev(i), jnp.where(i == 0, 0, j))),
        out_shape=jax.ShapeDtypeStruct((n, d), F32),
        scratch_shapes=[pltpu.VMEM((2, tm, d), BF16)],
        compiler_params=_cparams(("arbitrary", "arbitrary")),
        name="merge",
    )(h, x2, o_a, o_b_tm, o_c, _col_tiles(wg, tn), _col_tiles(bg, tn), _col_tiles(p_a, tn),
      _col_tiles(p_b, tn), _col_tiles(p_c, tn), _col_tiles(w_out, tn), gt)


FFN_HALO = 16


def _ffn_kernel(x_ref, xh_ref, mod_ref, g_ref, wa_ref, wb_ref, cw_ref, cb_ref, wd_ref, o_ref,
                h_ref, acc_ref, *, tm, tpb):
    i = pl.program_id(0)
    j = pl.program_id(1)
    nj = pl.num_programs(1)

    @pl.when(j == 0)
    def _():
        g, sc, sh = g_ref[...], mod_ref[0, 4:5, :], mod_ref[0, 3:4, :]
        halo = _norm_mod(xh_ref[...], g, sc, sh)
        h_ref[0:FFN_HALO, :] = jnp.where(i % tpb == 0, 0.0, halo).astype(BF16)
        h_ref[FFN_HALO:, :] = _norm_mod(x_ref[...], g, sc, sh).astype(BF16)
        acc_ref[...] = jnp.zeros(acc_ref.shape, F32)

    a = jnp.dot(h_ref[...], wa_ref[0], preferred_element_type=F32)
    b = jnp.dot(h_ref[FFN_HALO:, :], wb_ref[0], preferred_element_type=F32)
    a_conv = cb_ref[...] + a[FFN_HALO - 2:FFN_HALO - 2 + tm] * cw_ref[0:1, :]
    a_conv = a_conv + a[FFN_HALO - 1:FFN_HALO - 1 + tm] * cw_ref[1:2, :]
    a_conv = a_conv + a[FFN_HALO:] * cw_ref[2:3, :]
    act = (a_conv * jax.nn.sigmoid(a_conv)) * b
    acc_ref[...] += jnp.dot(act.astype(BF16), wd_ref[...], preferred_element_type=F32)

    @pl.when(j == nj - 1)
    def _():
        o_ref[...] = x_ref[...] + mod_ref[0, 5:6, :] * acc_ref[...]


def _ffn(x2, mod, g2, w_a, w_b, conv_w, conv_b, w_down, seq, tm, tn):
    n, d = x2.shape
    dff = w_a.shape[1]
    tpb = seq // tm
    hb = tm // FFN_HALO
    return pl.pallas_call(
        functools.partial(_ffn_kernel, tm=tm, tpb=tpb),
        grid=(n // tm, dff // tn),
        in_specs=[
            pl.BlockSpec((tm, d), lambda i, j: (i, 0)),
            pl.BlockSpec((FFN_HALO, d), lambda i, j: (jnp.maximum(i * hb - 1, 0), 0)),
            pl.BlockSpec((1, 6, d), lambda i, j: (i // tpb, 0, 0)),
            pl.BlockSpec((1, d), lambda i, j: (0, 0)),
            pl.BlockSpec((1, d, tn), lambda i, j: (j, 0, 0)),
            pl.BlockSpec((1, d, tn), lambda i, j: (j, 0, 0)),
            pl.BlockSpec((CONV_WIDTH, tn), lambda i, j: (0, j)),
            pl.BlockSpec((1, tn), lambda i, j: (0, j)),
            pl.BlockSpec((tn, d), lambda i, j: (j, 0)),
        ],
        out_specs=pl.BlockSpec((tm, d), lambda i, j: (i, 0)),
        out_shape=jax.ShapeDtypeStruct((n, d), F32),
        scratch_shapes=[pltpu.VMEM((tm + FFN_HALO, d), BF16), pltpu.VMEM((tm, d), F32)],
        compiler_params=_cparams(("arbitrary", "arbitrary")),
        name="ffn",
    )(x2, x2, mod, g2, _col_tiles(w_a, tn), _col_tiles(w_b, tn), conv_w, conv_b, w_down)


def _rope_tables(positions, rot_dim, period):
    half = rot_dim // 2
    inv_freq = ROPE_THETA ** (-jnp.arange(half, dtype=F32) * (2.0 / rot_dim))
    ang = positions.astype(F32)[..., None] * inv_freq
    cos, sin = jnp.cos(ang), jnp.sin(ang)
    rest = period - 2 * half
    ones = jnp.ones(cos.shape[:-1] + (rest,), F32)
    zh = jnp.zeros_like(sin)
    zr = jnp.zeros_like(ones)
    reps = LANES // period
    out = []
    for parts in ((cos, cos, ones), (-sin, zh, zr), (zh, sin, zr)):
        t = jnp.concatenate(parts, axis=-1)
        out.append(jnp.tile(t, (1, 1, reps)).reshape(-1, LANES))
    return out


def _pad_cols(w, new):
    return jnp.pad(w, ((0, 0), (0, new - w.shape[1])))


def kernel(x, c, positions, w_ada, b_ada, g_norm1, g_norm2, w_in, g_q, g_k, a_re, a_im, b_re, b_im,
           c_re, c_im, d_skip, log_dt, w_glu, w_pool, pool_scale, p_a, p_b, p_c, w_gate, b_gate,
           w_out, w_up, conv_w, conv_b, w_down, *, tm=512, tm_merge=1024, tn_merge=256, tn_ffn=512, tl=32):
    batch, seq, d = x.shape
    depth = w_ada.shape[0]
    n = batch * seq
    tm = min(tm, seq)
    tm_merge = min(tm_merge, seq)
    tl = min(tl, seq)
    topk = min(TOPK_MAX, seq // 4)
    dff_pad = -(-D_FF // tn_ffn) * tn_ffn

    mod_all = _ada(c, w_ada, b_ada).reshape(depth, batch, 6, d)
    tabs_q = _rope_tables(positions, ROT_DIM, HEAD_DIM)
    tabs_i = _rope_tables(positions, IDX_ROT_DIM, IDX_DIM)

    x2 = x.reshape(n, d)
    for l in range(depth):
        mod = mod_all[l]
        gt1 = mod[:, 2:3, :]
        w_in_pad = jnp.concatenate(
            [w_in[l][:, :COL_KW + KW_USED],
             jnp.zeros((d, LANES - KW_USED), F32),
             w_in[l][:, COL_KW + KW_USED:]], axis=1).astype(BF16)
        h, q, k, v, qi, ki2, kw, u_tm, p = _proj(
            x2, mod, g_norm1[l][None], w_in_pad, g_q[l][None], g_k[l][None], tabs_q, tabs_i,
            batch, seq, tm)
        o_a = _dsa(q, k, v, qi, ki2, kw, batch, seq, topk)
        bm, cm, lre, lim = _s5_params(a_re[l], a_im[l], b_re[l], b_im[l], c_re[l], c_im[l], log_dt[l])
        o_b = _s5(u_tm.reshape(seq * batch, WIDTH_B), bm, cm, lre, lim, d_skip[l][None],
                  w_glu[l].astype(BF16), batch, seq, tl)
        o_c = _pool(p, w_pool[l].astype(BF16), pool_scale[l][None], batch, seq)
        x2 = _merge(x2, h, gt1, o_a, o_b.reshape(seq, batch * WIDTH_B), o_c,
                    w_gate[l].astype(BF16), b_gate[l], p_a[l].astype(BF16), p_b[l].astype(BF16),
                    p_c[l].astype(BF16), w_out[l].astype(BF16), seq, tm_merge, tn_merge)
        w_a = _pad_cols(w_up[l][:, :D_FF], dff_pad).astype(BF16)
        w_b = _pad_cols(w_up[l][:, D_FF:], dff_pad).astype(BF16)
        w_d = jnp.pad(w_down[l], ((0, dff_pad - D_FF), (0, 0))).astype(BF16)
        x2 = _ffn(x2, mod, g_norm2[l][None], w_a, w_b, _pad_cols(conv_w[l], dff_pad),
                  _pad_cols(conv_b[l][None], dff_pad), w_d, seq, tm, tn_ffn)
    return x2.reshape(batch, seq, d)
```

```python
import functools
import math

import jax
import jax.numpy as jnp
from jax import lax
from jax.experimental import pallas as pl
from jax.experimental.pallas import tpu as pltpu

F32 = jnp.float32
BF16 = jnp.bfloat16
I32 = jnp.int32

D_MODEL = 2048
DEPTH = 2
CHUNK = 64
EPS = 1e-6
NEG_INF = -1e30
ROPE_THETA = 500000.0

N_HEADS_A = 8
HEAD_DIM = 128
ROT_DIM = HEAD_DIM // 4
N_IDX_HEADS = 8
IDX_DIM = 64
IDX_ROT_DIM = IDX_DIM // 4
N_IDX_PAIRS = N_IDX_HEADS * IDX_DIM // 128
TOPK_MAX = 256
Q_BLOCK = 128
WIDTH_A = N_HEADS_A * HEAD_DIM

WIDTH_B = D_MODEL // 4
SSM_GROUP = 16
N_SSM_GROUPS = WIDTH_B // SSM_GROUP
SSM_STATE = 64
SSM_BUNDLE = 8
N_SSM_BUNDLES = N_SSM_GROUPS // SSM_BUNDLE
BUNDLE_CH = SSM_BUNDLE * SSM_GROUP
BUNDLE_ST = SSM_BUNDLE * SSM_STATE

WIDTH_C = D_MODEL // 4
POOL_WINDOWS = (2, 4, 8, 16)
POOL_GROUP = WIDTH_C // 4

D_FF = 5504
CONV_WIDTH = 3

LANES = 128
INT_MIN = -(2 ** 31)

COL_K = WIDTH_A
COL_V = COL_K + HEAD_DIM
COL_QI = COL_V + HEAD_DIM
COL_KW = COL_QI + N_IDX_HEADS * IDX_DIM
COL_U = COL_KW + LANES
COL_P = COL_U + WIDTH_B
D_IN_PAD = COL_P + WIDTH_C
KW_USED = IDX_DIM + N_IDX_HEADS

VMEM_LIMIT = 56 * 1024 * 1024


def _cparams(sem):
    return pltpu.CompilerParams(dimension_semantics=sem, vmem_limit_bytes=VMEM_LIMIT)


def _norm_mod(x, g, sc, sh):
    ms = jnp.mean(x * x, axis=-1, keepdims=True)
    y = x * lax.rsqrt(ms + EPS)
    return (y * g) * (1.0 + sc) + sh


def _rope(x, c, sa, sb, half):
    return x * c + pltpu.roll(x, LANES - half, 1) * sa + pltpu.roll(x, half, 1) * sb


def _ada_kernel(c_ref, w_ref, b_ref, o_ref):
    c = c_ref[...]
    ca = c * jax.nn.sigmoid(c)
    o_ref[0] = jnp.dot(ca, w_ref[0], preferred_element_type=F32) + b_ref[0]


def _ada(c, w_ada, b_ada, tn=1024):
    depth, d, n = w_ada.shape
    b = c.shape[0]
    return pl.pallas_call(
        _ada_kernel,
        grid=(depth, n // tn),
        in_specs=[
            pl.BlockSpec((b, d), lambda l, j: (0, 0)),
            pl.BlockSpec((1, d, tn), lambda l, j: (l, 0, j)),
            pl.BlockSpec((1, 1, tn), lambda l, j: (l, 0, j)),
        ],
        out_specs=pl.BlockSpec((1, b, tn), lambda l, j: (l, 0, j)),
        out_shape=jax.ShapeDtypeStruct((depth, b, n), F32),
        compiler_params=_cparams(("arbitrary", "arbitrary")),
        name="ada",
    )(c, w_ada, b_ada.reshape(depth, 1, n))


def _proj_kernel(x_ref, mod_ref, g_ref, w_ref, gq_ref, gk_ref, cq_ref, saq_ref, sbq_ref,
                 ci_ref, sai_ref, sbi_ref,
                 h_ref, q_ref, k_ref, v_ref, qi_ref, ki2_ref, kw_ref, u_ref, p_ref):
    h = _norm_mod(x_ref[...], g_ref[...], mod_ref[0, 1:2, :], mod_ref[0, 0:1, :]).astype(BF16)
    h_ref[...] = h

    def mm(c0, width):
        return jnp.dot(h, w_ref[:, c0:c0 + width], preferred_element_type=F32)

    def qk_norm_rope(xh, g):
        ms = jnp.mean(xh * xh, axis=-1, keepdims=True)
        y = xh * lax.rsqrt(ms + EPS) * g
        return _rope(y, cq_ref[...], saq_ref[...], sbq_ref[...], ROT_DIM // 2)

    def store_stacked(ref, slab, idx, nslab):
        for qbl in range(slab.shape[0] // Q_BLOCK):
            r0 = (qbl * nslab + idx) * Q_BLOCK
            ref[r0:r0 + Q_BLOCK, :] = slab[qbl * Q_BLOCK:(qbl + 1) * Q_BLOCK]

    q_all = mm(0, WIDTH_A)
    for hd in range(N_HEADS_A):
        qh = qk_norm_rope(q_all[:, hd * HEAD_DIM:(hd + 1) * HEAD_DIM], gq_ref[...]).astype(BF16)
        store_stacked(q_ref, qh, hd, N_HEADS_A)
    kv = mm(COL_K, 2 * HEAD_DIM)
    k_ref[...] = qk_norm_rope(kv[:, 0:HEAD_DIM], gk_ref[...]).astype(BF16)
    v_ref[...] = kv[:, HEAD_DIM:2 * HEAD_DIM].astype(BF16)
    qi_all = mm(COL_QI, N_IDX_PAIRS * LANES)
    for s in range(N_IDX_PAIRS):
        qi = qi_all[:, s * LANES:(s + 1) * LANES]
        qi = _rope(qi, ci_ref[...], sai_ref[...], sbi_ref[...], IDX_ROT_DIM // 2).astype(BF16)
        store_stacked(qi_ref, qi, s, N_IDX_PAIRS)
    rest = mm(COL_KW, LANES + WIDTH_B + WIDTH_C)
    kw = rest[:, 0:LANES]
    ki = _rope(kw, ci_ref[...], sai_ref[...], sbi_ref[...], IDX_ROT_DIM // 2)
    lane = lax.broadcasted_iota(I32, ki.shape, 1)
    ki_lo = jnp.where(lane < IDX_DIM, ki, 0.0)
    ki2_ref[:, 0:LANES] = ki_lo.astype(BF16)
    ki2_ref[:, LANES:2 * LANES] = pltpu.roll(ki_lo, IDX_DIM, 1).astype(BF16)
    kw_ref[...] = kw * (N_IDX_HEADS ** -0.5 * IDX_DIM ** -0.5)
    u_ref[...] = rest[:, LANES:LANES + WIDTH_B]
    p_ref[...] = rest[:, LANES + WIDTH_B:]


def _proj(x2, mod, g1, w_in_pad, g_q, g_k, tabs_q, tabs_i, batch, seq, tm):
    n, d = x2.shape
    tpb = seq // tm
    row = lambda i: (i, 0)
    const = lambda i: (0, 0)
    tab_spec = pl.BlockSpec((tm, LANES), row)
    return pl.pallas_call(
        _proj_kernel,
        grid=(n // tm,),
        in_specs=[
            pl.BlockSpec((tm, d), row),
            pl.BlockSpec((1, 6, d), lambda i: (i // tpb, 0, 0)),
            pl.BlockSpec((1, d), const),
            pl.BlockSpec((d, D_IN_PAD), const),
            pl.BlockSpec((1, HEAD_DIM), const),
            pl.BlockSpec((1, HEAD_DIM), const),
            tab_spec, tab_spec, tab_spec, tab_spec, tab_spec, tab_spec,
        ],
        out_specs=[
            pl.BlockSpec((tm, d), row),
            pl.BlockSpec((tm * N_HEADS_A, HEAD_DIM), row),
            pl.BlockSpec((tm, HEAD_DIM), row),
            pl.BlockSpec((tm, HEAD_DIM), row),
            pl.BlockSpec((tm * N_IDX_PAIRS, LANES), row),
            pl.BlockSpec((tm, 2 * LANES), row),
            pl.BlockSpec((tm, LANES), row),
            pl.BlockSpec((tm, WIDTH_B), lambda i: (i % tpb, i // tpb)),
            pl.BlockSpec((tm, WIDTH_C), row),
        ],
        out_shape=[
            jax.ShapeDtypeStruct((n, d), BF16),
            jax.ShapeDtypeStruct((n * N_HEADS_A, HEAD_DIM), BF16),
            jax.ShapeDtypeStruct((n, HEAD_DIM), BF16),
            jax.ShapeDtypeStruct((n, HEAD_DIM), BF16),
            jax.ShapeDtypeStruct((n * N_IDX_PAIRS, LANES), BF16),
            jax.ShapeDtypeStruct((n, 2 * LANES), BF16),
            jax.ShapeDtypeStruct((n, LANES), F32),
            jax.ShapeDtypeStruct((seq, batch * WIDTH_B), F32),
            jax.ShapeDtypeStruct((n, WIDTH_C), F32),
        ],
        compiler_params=_cparams(("arbitrary",)),
        name="proj",
    )(x2, mod, g1, w_in_pad, g_q, g_k, *tabs_q, *tabs_i)


KEY_SLABS = 4
KEY_BLOCK = KEY_SLABS * LANES


def _dsa_scores(qi_ref, ki2_ref, kw_ref, s_ref, nqb):
    row = lax.broadcasted_iota(I32, (Q_BLOCK, KEY_BLOCK), 0)
    lane = lax.broadcasted_iota(I32, (Q_BLOCK, KEY_BLOCK), 1)
    chunk_end = (row // CHUNK + 1) * CHUNK
    trans_b = (((1,), (1,)), ((), ()))

    def qb_body(qq, carry):
        r0 = pl.multiple_of(qq * Q_BLOCK, Q_BLOCK)
        w = kw_ref[pl.ds(r0, Q_BLOCK), :]
        wb = [jnp.broadcast_to(w[:, IDX_DIM + hd:IDX_DIM + hd + 1], (Q_BLOCK, KEY_BLOCK))
              for hd in range(N_IDX_HEADS)]
        limit_row = r0 + chunk_end

        def key_body(j, c2):
            k0 = pl.multiple_of(j * KEY_BLOCK, KEY_BLOCK)
            kia = ki2_ref[pl.ds(k0, KEY_BLOCK), 0:LANES]
            kib = ki2_ref[pl.ds(k0, KEY_BLOCK), LANES:2 * LANES]
            qp = qi_ref[pl.ds(pl.multiple_of(r0 * N_IDX_PAIRS, Q_BLOCK), Q_BLOCK * N_IDX_PAIRS), :]
            sa = lax.dot_general(qp, kia, trans_b, preferred_element_type=F32)
            sb = lax.dot_general(qp, kib, trans_b, preferred_element_type=F32)
            sc = jnp.zeros((Q_BLOCK, KEY_BLOCK), F32)
            for pr in range(N_IDX_PAIRS):
                rows = slice(pr * Q_BLOCK, (pr + 1) * Q_BLOCK)
                sc = sc + jnp.maximum(sa[rows], 0.0) * wb[2 * pr] + jnp.maximum(sb[rows], 0.0) * wb[2 * pr + 1]
            adm = ((k0 + lane) < limit_row) & (sc > NEG_INF * 0.5)
            bits = pltpu.bitcast(sc, I32)
            bits = jnp.where(bits == INT_MIN, 0, bits)
            skey = bits ^ ((bits >> 31) & 0x7FFFFFFF)
            skey = jnp.where(adm, skey, INT_MIN)
            for sl in range(KEY_SLABS):
                s_ref[j * KEY_SLABS + sl, pl.ds(r0, Q_BLOCK), :] = skey[:, sl * LANES:(sl + 1) * LANES]
            return c2

        nkb = (r0 + Q_BLOCK + KEY_BLOCK - 1) // KEY_BLOCK
        lax.fori_loop(0, nkb, key_body, 0)
        return carry

    lax.fori_loop(0, nqb, qb_body, 0)


def _dsa_threshold(s_ref, thr_ref, cnt_ref, nqb, topk):
    thr_ref[...] = jnp.full(thr_ref.shape, INT_MIN, I32)

    ones = jnp.ones((LANES, LANES), BF16)

    def count_ge(offset):
        for qq in range(nqb):
            rows = slice(qq * Q_BLOCK, (qq + 1) * Q_BLOCK)
            cand = thr_ref[rows, :] + offset
            acc = jnp.zeros((Q_BLOCK, LANES), F32)
            for s in range(qq + 1):
                acc = acc + jnp.where(s_ref[s, rows, :] >= cand, 1.0, 0.0)
            cnt_ref[rows, :] = acc.astype(BF16)
        return jnp.dot(cnt_ref[...], ones, preferred_element_type=F32)

    def bit_body(i, carry):
        bitval = jnp.left_shift(jnp.int32(1), 31 - i)
        cnt = count_ge(bitval)
        thr = thr_ref[...]
        thr_ref[...] = jnp.where(cnt >= float(topk), thr + bitval, thr)
        return carry

    lax.fori_loop(0, 32, bit_body, 0)

    surplus = jnp.where((count_ge(0) > float(topk)) & (thr_ref[...] > INT_MIN), 1.0, 0.0).astype(BF16)
    seq = thr_ref.shape[0]
    block_of_row = lax.broadcasted_iota(I32, (nqb, seq), 1) // Q_BLOCK
    in_block = jnp.where(block_of_row == lax.broadcasted_iota(I32, (nqb, seq), 0), 1.0, 0.0).astype(BF16)
    surplus_rows = jnp.dot(in_block, surplus, preferred_element_type=F32)
    li = lax.broadcasted_iota(I32, (LANES, LANES), 0)
    lj = lax.broadcasted_iota(I32, (LANES, LANES), 1)
    upper = jnp.where(li <= lj, 1.0, 0.0).astype(BF16)

    for qq in range(nqb):
        rows = slice(qq * Q_BLOCK, (qq + 1) * Q_BLOCK)

        @pl.when(surplus_rows[qq, 0] > 0.0)
        def _():
            thr = thr_ref[rows, :]

            def gt_body(s, acc):
                return acc + jnp.where(s_ref[s, rows, :] > thr, 1.0, 0.0)

            n_gt = lax.fori_loop(0, qq + 1, gt_body, jnp.zeros((Q_BLOCK, LANES), F32))
            keep = float(topk) - jnp.dot(n_gt.astype(BF16), ones, preferred_element_type=F32)

            def slab_body(s, seen):
                sk = s_ref[s, rows, :]
                tie = sk == thr
                tie_b = jnp.where(tie, 1.0, 0.0).astype(BF16)
                rank = seen + jnp.dot(tie_b, upper, preferred_element_type=F32)
                drop = tie & (rank > keep) & (thr > INT_MIN)
                s_ref[s, rows, :] = jnp.where(drop, thr - 1, sk)
                return seen + jnp.dot(tie_b, ones, preferred_element_type=F32)

            lax.fori_loop(0, qq + 1, slab_body, jnp.zeros((Q_BLOCK, LANES), F32))


def _dsa_kernel(q_ref, k_ref, v_ref, qi_ref, ki2_ref, kw_ref, o_ref,
                s_ref, thr_ref, cnt_ref, bias_ref, lg_ref, p_ref, acc_ref, vx_ref, *m_refs,
                topk, nqb):
    qb = pl.program_id(1)

    @pl.when(qb == 0)
    def _():
        _dsa_scores(qi_ref, ki2_ref, kw_ref, s_ref, nqb)
        _dsa_threshold(s_ref, thr_ref, cnt_ref, nqb, topk)
        vx_ref[:, 0:HEAD_DIM] = v_ref[...]
        vx_ref[:, HEAD_DIM:] = jnp.ones((vx_ref.shape[0], LANES), BF16)

    r0 = pl.multiple_of(qb * Q_BLOCK, Q_BLOCK)
    nkb = (r0 + Q_BLOCK + KEY_BLOCK - 1) // KEY_BLOCK
    trans_b = (((1,), (1,)), ((), ()))
    thr = thr_ref[pl.ds(r0, Q_BLOCK), :]

    def bias_body(j, carry):
        for sl in range(KEY_SLABS):
            sk = s_ref[j * KEY_SLABS + sl, pl.ds(r0, Q_BLOCK), :]
            sel = (sk >= thr) & (sk > INT_MIN)
            bias_ref[j * KEY_SLABS + sl] = jnp.where(sel, 0.0, NEG_INF)
        return carry

    lax.fori_loop(0, nkb, bias_body, 0)

    for hd in range(N_HEADS_A):
        m_refs[hd][...] = jnp.full((Q_BLOCK, LANES), NEG_INF, F32)
    acc_ref[...] = jnp.zeros(acc_ref.shape, F32)

    def max_body(j, carry):
        k0 = pl.multiple_of(j * KEY_BLOCK, KEY_BLOCK)
        s = lax.dot_general(q_ref[...], k_ref[pl.ds(k0, KEY_BLOCK), :], trans_b, preferred_element_type=F32)
        for hd in range(N_HEADS_A):
            rows = slice(hd * Q_BLOCK, (hd + 1) * Q_BLOCK)
            m = m_refs[hd][...]
            for sl in range(KEY_SLABS):
                sh = s[rows, sl * LANES:(sl + 1) * LANES] + bias_ref[j * KEY_SLABS + sl]
                lg_ref[j * KEY_SLABS + sl, rows, :] = sh
                m = jnp.maximum(m, sh)
            m_refs[hd][...] = m
        return carry

    lax.fori_loop(0, nkb, max_body, 0)
    for hd in range(N_HEADS_A):
        m_refs[hd][...] = jnp.broadcast_to(jnp.max(m_refs[hd][...], axis=1, keepdims=True), (Q_BLOCK, LANES))

    c_exp = HEAD_DIM ** -0.5 * math.log2(math.e)

    def sum_body(j, carry):
        k0 = pl.multiple_of(j * KEY_BLOCK, KEY_BLOCK)
        for hd in range(N_HEADS_A):
            rows = slice(hd * Q_BLOCK, (hd + 1) * Q_BLOCK)
            m = m_refs[hd][...]
            for sl in range(KEY_SLABS):
                p_s = jnp.exp2((lg_ref[j * KEY_SLABS + sl, rows, :] - m) * c_exp)
                p_ref[rows, sl * LANES:(sl + 1) * LANES] = p_s.astype(BF16)
        acc_ref[...] += jnp.dot(p_ref[...], vx_ref[pl.ds(k0, KEY_BLOCK), :], preferred_element_type=F32)
        return carry

    lax.fori_loop(0, nkb, sum_body, 0)
    for hd in range(N_HEADS_A):
        rows = slice(hd * Q_BLOCK, (hd + 1) * Q_BLOCK)
        o_ref[:, hd * HEAD_DIM:(hd + 1) * HEAD_DIM] = (
            acc_ref[rows, 0:HEAD_DIM] / acc_ref[rows, HEAD_DIM:]).astype(BF16)


def _dsa(q, k, v, qi, ki2, kw, batch, seq, topk):
    assert seq % KEY_BLOCK == 0
    n = k.shape[0]
    nqb = seq // Q_BLOCK
    qrow = lambda b, i: (b * nqb + i, 0)
    brow = lambda b, i: (b, 0)
    nsl = seq // LANES
    head_scratch = [pltpu.VMEM((Q_BLOCK, LANES), F32)] * N_HEADS_A
    return pl.pallas_call(
        functools.partial(_dsa_kernel, topk=topk, nqb=nqb),
        grid=(batch, nqb),
        in_specs=[
            pl.BlockSpec((Q_BLOCK * N_HEADS_A, HEAD_DIM), qrow),
            pl.BlockSpec((seq, HEAD_DIM), brow),
            pl.BlockSpec((seq, HEAD_DIM), brow),
            pl.BlockSpec((seq * N_IDX_PAIRS, LANES), brow),
            pl.BlockSpec((seq, 2 * LANES), brow),
            pl.BlockSpec((seq, LANES), brow),
        ],
        out_specs=pl.BlockSpec((Q_BLOCK, WIDTH_A), qrow),
        out_shape=jax.ShapeDtypeStruct((n, WIDTH_A), BF16),
        scratch_shapes=[
            pltpu.VMEM((nsl, seq, LANES), I32),
            pltpu.VMEM((seq, LANES), I32),
            pltpu.VMEM((seq, LANES), BF16),
            pltpu.VMEM((nsl, Q_BLOCK, LANES), F32),
            pltpu.VMEM((nsl, Q_BLOCK * N_HEADS_A, LANES), F32),
            pltpu.VMEM((Q_BLOCK * N_HEADS_A, KEY_BLOCK), BF16),
            pltpu.VMEM((Q_BLOCK * N_HEADS_A, HEAD_DIM + LANES), F32),
            pltpu.VMEM((seq, HEAD_DIM + LANES), BF16),
        ] + head_scratch,
        compiler_params=_cparams(("arbitrary", "arbitrary")),
        name="dsa",
    )(q, k, v, qi, ki2, kw)


def _s5_kernel(u_ref, bm_ref, cm_ref, lre_ref, lim_ref, dsk_ref, wglu_ref, o_ref,
               bu_ref, st_ref, y_ref, *, batch, tl):
    @pl.when(pl.program_id(0) == 0)
    def _():
        st_ref[...] = jnp.zeros(st_ref.shape, F32)

    for gb in range(N_SSM_BUNDLES):
        ch = slice(gb * BUNDLE_CH, (gb + 1) * BUNDLE_CH)
        u_g = u_ref[:, ch]
        bu_ref[...] = jnp.dot(u_g.astype(BF16), bm_ref[gb], preferred_element_type=F32)
        lam_re = jnp.broadcast_to(lre_ref[gb], (batch, BUNDLE_ST))
        lam_im = jnp.broadcast_to(lim_ref[gb], (batch, BUNDLE_ST))

        def step(t, carry):
            x_re, x_im = carry
            r0 = pl.multiple_of(t * batch, batch)
            n_re = lam_re * x_re - lam_im * x_im + bu_ref[pl.ds(r0, batch), 0:BUNDLE_ST]
            n_im = lam_re * x_im + lam_im * x_re + bu_ref[pl.ds(r0, batch), BUNDLE_ST:2 * BUNDLE_ST]
            bu_ref[pl.ds(r0, batch), 0:BUNDLE_ST] = n_re
            bu_ref[pl.ds(r0, batch), BUNDLE_ST:2 * BUNDLE_ST] = n_im
            return n_re, n_im

        x_re, x_im = lax.fori_loop(
            0, tl, step, (st_ref[gb, :, 0:BUNDLE_ST], st_ref[gb, :, BUNDLE_ST:2 * BUNDLE_ST]))
        st_ref[gb, :, 0:BUNDLE_ST] = x_re
        st_ref[gb, :, BUNDLE_ST:2 * BUNDLE_ST] = x_im
        y = jnp.dot(bu_ref[...].astype(BF16), cm_ref[gb], preferred_element_type=F32)
        y_ref[:, ch] = y + dsk_ref[:, ch] * u_g

    y = jax.nn.gelu(y_ref[...])
    gl = jnp.dot(y.astype(BF16), wglu_ref[...], preferred_element_type=F32)
    o_ref[...] = (y * jax.nn.sigmoid(gl)).astype(BF16)


def _s5_params(a_re, a_im, b_re, b_im, c_re, c_im, log_dt):
    dt = jnp.exp(log_dt)[:, None]
    mag = jnp.exp(a_re * dt)
    lb_re = mag * jnp.cos(a_im * dt)
    lb_im = mag * jnp.sin(a_im * dt)
    nr, ni = lb_re - 1.0, lb_im
    den = a_re * a_re + a_im * a_im
    q_re = (nr * a_re + ni * a_im) / den
    q_im = (ni * a_re - nr * a_im) / den
    bb_re = q_re[..., None] * b_re - q_im[..., None] * b_im
    bb_im = q_re[..., None] * b_im + q_im[..., None] * b_re
    eye = jnp.eye(SSM_BUNDLE, dtype=F32)
    nb = N_SSM_BUNDLES

    def pack_b(m):
        m = m.reshape(nb, SSM_BUNDLE, SSM_STATE, SSM_GROUP)
        return jnp.einsum('bgpi,gh->bgihp', m, eye).reshape(nb, BUNDLE_CH, BUNDLE_ST)

    def pack_c(m):
        m = m.reshape(nb, SSM_BUNDLE, SSM_GROUP, SSM_STATE)
        return jnp.einsum('bgop,gh->bgpho', m, eye).reshape(nb, BUNDLE_ST, BUNDLE_CH)

    bm = jnp.concatenate([pack_b(bb_re), pack_b(bb_im)], axis=2).astype(BF16)
    cm = jnp.concatenate([pack_c(c_re), -pack_c(c_im)], axis=1).astype(BF16)
    return bm, cm, lb_re.reshape(nb, 1, BUNDLE_ST), lb_im.reshape(nb, 1, BUNDLE_ST)


def _s5(u_tm, bm, cm, lre, lim, d_skip, w_glu, batch, seq, tl):
    rows = tl * batch
    const3 = lambda t: (0, 0, 0)
    const2 = lambda t: (0, 0)
    return pl.pallas_call(
        functools.partial(_s5_kernel, batch=batch, tl=tl),
        grid=(seq // tl,),
        in_specs=[
            pl.BlockSpec((rows, WIDTH_B), lambda t: (t, 0)),
            pl.BlockSpec(bm.shape, const3),
            pl.BlockSpec(cm.shape, const3),
            pl.BlockSpec(lre.shape, const3),
            pl.BlockSpec(lim.shape, const3),
            pl.BlockSpec((1, WIDTH_B), const2),
            pl.BlockSpec((WIDTH_B, WIDTH_B), const2),
        ],
        out_specs=pl.BlockSpec((rows, WIDTH_B), lambda t: (t, 0)),
        out_shape=jax.ShapeDtypeStruct((seq * batch, WIDTH_B), BF16),
        scratch_shapes=[
            pltpu.VMEM((rows, 2 * BUNDLE_ST), F32),
            pltpu.VMEM((N_SSM_BUNDLES, batch, 2 * BUNDLE_ST), F32),
            pltpu.VMEM((rows, WIDTH_B), F32),
        ],
        compiler_params=_cparams(("arbitrary",)),
        name="s5",
    )(u_tm, bm, cm, lre, lim, d_skip, w_glu)


def _pool_kernel(p_ref, w_ref, sc_ref, o_ref):
    seq = p_ref.shape[0]
    t = lax.broadcasted_iota(I32, (seq, POOL_GROUP), 0)
    t1 = (t + 1).astype(F32)
    for g, win in enumerate(POOL_WINDOWS):
        ch = slice(g * POOL_GROUP, (g + 1) * POOL_GROUP)
        x = p_ref[:, ch]
        s = x
        sh = 1
        while sh < win:
            s = s + jnp.where(t >= sh, pltpu.roll(s, sh, 0), 0.0)
            sh *= 2
        pooled = s / jnp.minimum(t1, float(win)) - x
        y = jnp.dot(pooled.astype(BF16), w_ref[g], preferred_element_type=F32)
        o_ref[:, ch] = (y * sc_ref[:, ch]).astype(BF16)


def _pool(p, w_pool, pool_scale, batch, seq):
    return pl.pallas_call(
        _pool_kernel,
        grid=(batch,),
        in_specs=[
            pl.BlockSpec((seq, WIDTH_C), lambda b: (b, 0)),
            pl.BlockSpec(w_pool.shape, lambda b: (0, 0, 0)),
            pl.BlockSpec((1, WIDTH_C), lambda b: (0, 0)),
        ],
        out_specs=pl.BlockSpec((seq, WIDTH_C), lambda b: (b, 0)),
        out_shape=jax.ShapeDtypeStruct(p.shape, BF16),
        compiler_params=_cparams(("arbitrary",)),
        name="pool",
    )(p, w_pool, pool_scale)


def _col_tiles(w, tn):
    *lead, k, n = w.shape
    nl = len(lead)
    return w.reshape(*lead, k, n // tn, tn).transpose(*range(nl), nl + 1, nl, nl + 2)


def _merge_kernel(h_ref, xc_ref, oa_ref, ob_ref, oc_ref, wg_ref, bg_ref,
                  pa_ref, pb_ref, pc_ref, wo_ref, gt_ref, o_ref, mg_ref, *, ntiles, tn):
    i = pl.program_id(0)
    j = pl.program_id(1)
    slot = i % 2

    @pl.when(i < ntiles)
    def _():
        h = h_ref[...]
        branches = ((oa_ref, pa_ref), (ob_ref, pb_ref), (oc_ref, pc_ref))
        merged = None
        for b, (o_r, p_r) in enumerate(branches):
            gate = jax.nn.sigmoid(jnp.dot(h, wg_ref[b, 0], preferred_element_type=F32) + bg_ref[0, b:b + 1, :])
            term = gate * jnp.dot(o_r[...], p_r[0], preferred_element_type=F32)
            merged = term if merged is None else merged + term
        c0 = pl.multiple_of(j * tn, tn)
        mg_ref[slot, :, pl.ds(c0, tn)] = merged.astype(BF16)

    @pl.when(i >= 1)
    def _():
        mix = jnp.dot(mg_ref[1 - slot], wo_ref[0], preferred_element_type=F32)
        o_ref[...] = xc_ref[...] + gt_ref[0] * mix


def _merge(x2, h, gt, o_a, o_b_tm, o_c, wg, bg, p_a, p_b, p_c, w_out, seq, tm, tn):
    n, d = x2.shape
    nj = d // tn
    ntiles = n // tm
    tpb = seq // tm
    cur = lambda i: jnp.minimum(i, ntiles - 1)
    prev = lambda i: jnp.maximum(i - 1, 0)
    row = lambda i, j: (cur(i), 0)
    wt = lambda i, j: (j, 0, 0)
    return pl.pallas_call(
        functools.partial(_merge_kernel, ntiles=ntiles, tn=tn),
        grid=(ntiles + 1, nj),
        in_specs=[
            pl.BlockSpec((tm, d), row),
            pl.BlockSpec((tm, tn), lambda i, j: (prev(i), j)),
            pl.BlockSpec((tm, WIDTH_A), row),
            pl.BlockSpec((tm, WIDTH_B), lambda i, j: (cur(i) % tpb, cur(i) // tpb)),
            pl.BlockSpec((tm, WIDTH_C), row),
            pl.BlockSpec((3, 1, d, tn), lambda i, j: (0, j, 0, 0)),
            pl.BlockSpec((1, 3, tn), wt),
            pl.BlockSpec((1, WIDTH_A, tn), wt),
            pl.BlockSpec((1, WIDTH_B, tn), wt),
            pl.BlockSpec((1, WIDTH_C, tn), wt),
            pl.BlockSpec((1, d, tn), wt),
            pl.BlockSpec((1, 1, tn), lambda i, j: (prev(i) // tpb, 0, j)),
        ],
        out_specs=pl.BlockSpec((tm, tn), lambda i, j: (prev(i), jnp.where(i == 0, 0, j))),
        out_shape=jax.ShapeDtypeStruct((n, d), F32),
        scratch_shapes=[pltpu.VMEM((2, tm, d), BF16)],
        compiler_params=_cparams(("arbitrary", "arbitrary")),
        name="merge",
    )(h, x2, o_a, o_b_tm, o_c, _col_tiles(wg, tn), _col_tiles(bg, tn), _col_tiles(p_a, tn),
      _col_tiles(p_b, tn), _col_tiles(p_c, tn), _col_tiles(w_out, tn), gt)


FFN_HALO = 16


def _ffn_kernel(x_ref, xh_ref, mod_ref, g_ref, wa_ref, wb_ref, cw_ref, cb_ref, wd_ref, o_ref,
                h_ref, acc_ref, *, tm, tpb):
    i = pl.program_id(0)
    j = pl.program_id(1)
    nj = pl.num_programs(1)

    @pl.when(j == 0)
    def _():
        g, sc, sh = g_ref[...], mod_ref[0, 4:5, :], mod_ref[0, 3:4, :]
        halo = _norm_mod(xh_ref[...], g, sc, sh)
        h_ref[0:FFN_HALO, :] = jnp.where(i % tpb == 0, 0.0, halo).astype(BF16)
        h_ref[FFN_HALO:, :] = _norm_mod(x_ref[...], g, sc, sh).astype(BF16)
        acc_ref[...] = jnp.zeros(acc_ref.shape, F32)

    a = jnp.dot(h_ref[...], wa_ref[0], preferred_element_type=F32)
    b = jnp.dot(h_ref[FFN_HALO:, :], wb_ref[0], preferred_element_type=F32)
    a_conv = cb_ref[...] + a[FFN_HALO - 2:FFN_HALO - 2 + tm] * cw_ref[0:1, :]
    a_conv = a_conv + a[FFN_HALO - 1:FFN_HALO - 1 + tm] * cw_ref[1:2, :]
    a_conv = a_conv + a[FFN_HALO:] * cw_ref[2:3, :]
    act = (a_conv * jax.nn.sigmoid(a_conv)) * b
    acc_ref[...] += jnp.dot(act.astype(BF16), wd_ref[...], preferred_element_type=F32)

    @pl.when(j == nj - 1)
    def _():
        o_ref[...] = x_ref[...] + mod_ref[0, 5:6, :] * acc_ref[...]


def _ffn(x2, mod, g2, w_a, w_b, conv_w, conv_b, w_down, seq, tm, tn):
    n, d = x2.shape
    dff = w_a.shape[1]
    tpb = seq // tm
    hb = tm // FFN_HALO
    return pl.pallas_call(
        functools.partial(_ffn_kernel, tm=tm, tpb=tpb),
        grid=(n // tm, dff // tn),
        in_specs=[
            pl.BlockSpec((tm, d), lambda i, j: (i, 0)),
            pl.BlockSpec((FFN_HALO, d), lambda i, j: (jnp.maximum(i * hb - 1, 0), 0)),
            pl.BlockSpec((1, 6, d), lambda i, j: (i // tpb, 0, 0)),
            pl.BlockSpec((1, d), lambda i, j: (0, 0)),
            pl.BlockSpec((1, d, tn), lambda i, j: (j, 0, 0)),
            pl.BlockSpec((1, d, tn), lambda i, j: (j, 0, 0)),
            pl.BlockSpec((CONV_WIDTH, tn), lambda i, j: (0, j)),
            pl.BlockSpec((1, tn), lambda i, j: (0, j)),
            pl.BlockSpec((tn, d), lambda i, j: (j, 0)),
        ],
        out_specs=pl.BlockSpec((tm, d), lambda i, j: (i, 0)),
        out_shape=jax.ShapeDtypeStruct((n, d), F32),
        scratch_shapes=[pltpu.VMEM((tm + FFN_HALO, d), BF16), pltpu.VMEM((tm, d), F32)],
        compiler_params=_cparams(("arbitrary", "arbitrary")),
        name="ffn",
    )(x2, x2, mod, g2, _col_tiles(w_a, tn), _col_tiles(w_b, tn), conv_w, conv_b, w_down)


def _rope_tables(positions, rot_dim, period):
    half = rot_dim // 2
    inv_freq = ROPE_THETA ** (-jnp.arange(half, dtype=F32) * (2.0 / rot_dim))
    ang = positions.astype(F32)[..., None] * inv_freq
    cos, sin = jnp.cos(ang), jnp.sin(ang)
    rest = period - 2 * half
    ones = jnp.ones(cos.shape[:-1] + (rest,), F32)
    zh = jnp.zeros_like(sin)
    zr = jnp.zeros_like(ones)
    reps = LANES // period
    out = []
    for parts in ((cos, cos, ones), (-sin, zh, zr), (zh, sin, zr)):
        t = jnp.concatenate(parts, axis=-1)
        out.append(jnp.tile(t, (1, 1, reps)).reshape(-1, LANES))
    return out


def _pad_cols(w, new):
    return jnp.pad(w, ((0, 0), (0, new - w.shape[1])))


def kernel(x, c, positions, w_ada, b_ada, g_norm1, g_norm2, w_in, g_q, g_k, a_re, a_im, b_re, b_im,
           c_re, c_im, d_skip, log_dt, w_glu, w_pool, pool_scale, p_a, p_b, p_c, w_gate, b_gate,
           w_out, w_up, conv_w, conv_b, w_down, *, tm=512, tm_merge=1024, tn_merge=256, tn_ffn=512, tl=32):
    batch, seq, d = x.shape
    depth = w_ada.shape[0]
    n = batch * seq
    tm = min(tm, seq)
    tm_merge = min(tm_merge, seq)
    tl = min(tl, seq)
    topk = min(TOPK_MAX, seq // 4)
    dff_pad = -(-D_FF // tn_ffn) * tn_ffn

    mod_all = _ada(c, w_ada, b_ada).reshape(depth, batch, 6, d)
    tabs_q = _rope_tables(positions, ROT_DIM, HEAD_DIM)
    tabs_i = _rope_tables(positions, IDX_ROT_DIM, IDX_DIM)

    x2 = x.reshape(n, d)
    for l in range(depth):
        mod = mod_all[l]
        gt1 = mod[:, 2:3, :]
        w_in_pad = jnp.concatenate(
            [w_in[l][:, :COL_KW + KW_USED],
             jnp.zeros((d, LANES - KW_USED), F32),
             w_in[l][:, COL_KW + KW_USED:]], axis=1).astype(BF16)
        h, q, k, v, qi, ki2, kw, u_tm, p = _proj(
            x2, mod, g_norm1[l][None], w_in_pad, g_q[l][None], g_k[l][None], tabs_q, tabs_i,
            batch, seq, tm)
        o_a = _dsa(q, k, v, qi, ki2, kw, batch, seq, topk)
        bm, cm, lre, lim = _s5_params(a_re[l], a_im[l], b_re[l], b_im[l], c_re[l], c_im[l], log_dt[l])
        o_b = _s5(u_tm.reshape(seq * batch, WIDTH_B), bm, cm, lre, lim, d_skip[l][None],
                  w_glu[l].astype(BF16), batch, seq, tl)
        o_c = _pool(p, w_pool[l].astype(BF16), pool_scale[l][None], batch, seq)
        x2 = _merge(x2, h, gt1, o_a, o_b.reshape(seq, batch * WIDTH_B), o_c,
                    w_gate[l].astype(BF16), b_gate[l], p_a[l].astype(BF16), p_b[l].astype(BF16),
                    p_c[l].astype(BF16), w_out[l].astype(BF16), seq, tm_merge, tn_merge)
        w_a = _pad_cols(w_up[l][:, :D_FF], dff_pad).astype(BF16)
        w_b = _pad_cols(w_up[l][:, D_FF:], dff_pad).astype(BF16)
        w_d = jnp.pad(w_down[l], ((0, dff_pad - D_FF), (0, 0))).astype(BF16)
        x2 = _ffn(x2, mod, g_norm2[l][None], w_a, w_b, _pad_cols(conv_w[l], dff_pad),
                  _pad_cols(conv_b[l][None], dff_pad), w_d, seq, tm, tn_ffn)
    return x2.reshape(batch, seq, d)
```

```python
import functools
import math

import jax
import jax.numpy as jnp
from jax import lax
from jax.experimental import pallas as pl
from jax.experimental.pallas import tpu as pltpu

F32 = jnp.float32
BF16 = jnp.bfloat16
I32 = jnp.int32

D_MODEL = 2048
DEPTH = 2
CHUNK = 64
EPS = 1e-6
NEG_INF = -1e30
ROPE_THETA = 500000.0

N_HEADS_A = 8
HEAD_DIM = 128
ROT_DIM = HEAD_DIM // 4
N_IDX_HEADS = 8
IDX_DIM = 64
IDX_ROT_DIM = IDX_DIM // 4
N_IDX_PAIRS = N_IDX_HEADS * IDX_DIM // 128
TOPK_MAX = 256
Q_BLOCK = 128
WIDTH_A = N_HEADS_A * HEAD_DIM

WIDTH_B = D_MODEL // 4
SSM_GROUP = 16
N_SSM_GROUPS = WIDTH_B // SSM_GROUP
SSM_STATE = 64
SSM_BUNDLE = 8
N_SSM_BUNDLES = N_SSM_GROUPS // SSM_BUNDLE
BUNDLE_CH = SSM_BUNDLE * SSM_GROUP
BUNDLE_ST = SSM_BUNDLE * SSM_STATE

WIDTH_C = D_MODEL // 4
POOL_WINDOWS = (2, 4, 8, 16)
POOL_GROUP = WIDTH_C // 4

D_FF = 5504
CONV_WIDTH = 3

LOGIT_SCALE = HEAD_DIM ** -0.5 * math.log2(math.e)
LANES = 128
INT_MIN = -(2 ** 31)

COL_K = WIDTH_A
COL_V = COL_K + HEAD_DIM
COL_QI = COL_V + HEAD_DIM
COL_KW = COL_QI + N_IDX_HEADS * IDX_DIM
COL_U = COL_KW + LANES
COL_P = COL_U + WIDTH_B
D_IN_PAD = COL_P + WIDTH_C
KW_USED = IDX_DIM + N_IDX_HEADS

VMEM_LIMIT = 56 * 1024 * 1024


def _cparams(sem):
    return pltpu.CompilerParams(dimension_semantics=sem, vmem_limit_bytes=VMEM_LIMIT)


def _norm_mod(x, g, sc, sh):
    ms = jnp.mean(x * x, axis=-1, keepdims=True)
    y = x * lax.rsqrt(ms + EPS)
    return (y * g) * (1.0 + sc) + sh


def _rope(x, c, sa, sb, half):
    return x * c + pltpu.roll(x, LANES - half, 1) * sa + pltpu.roll(x, half, 1) * sb


def _ada_kernel(c_ref, w_ref, b_ref, o_ref):
    c = c_ref[...]
    ca = c * jax.nn.sigmoid(c)
    o_ref[0] = jnp.dot(ca.astype(BF16), w_ref[0].astype(BF16), preferred_element_type=F32) + b_ref[0]


def _ada(c, w_ada, b_ada, tn=1024):
    depth, d, n = w_ada.shape
    b = c.shape[0]
    return pl.pallas_call(
        _ada_kernel,
        grid=(depth, n // tn),
        in_specs=[
            pl.BlockSpec((b, d), lambda l, j: (0, 0)),
            pl.BlockSpec((1, d, tn), lambda l, j: (l, 0, j)),
            pl.BlockSpec((1, 1, tn), lambda l, j: (l, 0, j)),
        ],
        out_specs=pl.BlockSpec((1, b, tn), lambda l, j: (l, 0, j)),
        out_shape=jax.ShapeDtypeStruct((depth, b, n), F32),
        compiler_params=_cparams(("arbitrary", "arbitrary")),
        name="ada",
    )(c, w_ada, b_ada.reshape(depth, 1, n))


def _proj_kernel(x_ref, mod_ref, g_ref, w_ref, gq_ref, gk_ref, cq_ref, saq_ref, sbq_ref,
                 ci_ref, sai_ref, sbi_ref,
                 h_ref, q_ref, k_ref, v_ref, qi_ref, ki2_ref, kw_ref, u_ref, p_ref):
    h = _norm_mod(x_ref[...], g_ref[...], mod_ref[0, 1:2, :], mod_ref[0, 0:1, :]).astype(BF16)
    h_ref[...] = h

    def mm(c0, width):
        return jnp.dot(h, w_ref[:, c0:c0 + width], preferred_element_type=F32)

    def qk_norm_rope(xh, g):
        ms = jnp.mean(xh * xh, axis=-1, keepdims=True)
        y = xh * lax.rsqrt(ms + EPS) * g
        return _rope(y, cq_ref[...], saq_ref[...], sbq_ref[...], ROT_DIM // 2)

    def store_stacked(ref, slab, idx, nslab):
        for qbl in range(slab.shape[0] // Q_BLOCK):
            r0 = (qbl * nslab + idx) * Q_BLOCK
            ref[r0:r0 + Q_BLOCK, :] = slab[qbl * Q_BLOCK:(qbl + 1) * Q_BLOCK]

    q_all = mm(0, WIDTH_A)
    for hd in range(N_HEADS_A):
        qh = (qk_norm_rope(q_all[:, hd * HEAD_DIM:(hd + 1) * HEAD_DIM], gq_ref[...]) * LOGIT_SCALE).astype(BF16)
        store_stacked(q_ref, qh, hd, N_HEADS_A)
    kv = mm(COL_K, 2 * HEAD_DIM)
    k_ref[...] = qk_norm_rope(kv[:, 0:HEAD_DIM], gk_ref[...]).astype(BF16)
    v_ref[...] = kv[:, HEAD_DIM:2 * HEAD_DIM].astype(BF16)
    qi_all = mm(COL_QI, N_IDX_PAIRS * LANES)
    for s in range(N_IDX_PAIRS):
        qi = qi_all[:, s * LANES:(s + 1) * LANES]
        qi = _rope(qi, ci_ref[...], sai_ref[...], sbi_ref[...], IDX_ROT_DIM // 2).astype(BF16)
        store_stacked(qi_ref, qi, s, N_IDX_PAIRS)
    rest = mm(COL_KW, LANES + WIDTH_B + WIDTH_C)
    kw = rest[:, 0:LANES]
    ki = _rope(kw, ci_ref[...], sai_ref[...], sbi_ref[...], IDX_ROT_DIM // 2)
    lane = lax.broadcasted_iota(I32, ki.shape, 1)
    ki_lo = jnp.where(lane < IDX_DIM, ki, 0.0)
    ki2_ref[:, 0:LANES] = ki_lo.astype(BF16)
    ki2_ref[:, LANES:2 * LANES] = pltpu.roll(ki_lo, IDX_DIM, 1).astype(BF16)
    kw_ref[...] = kw * (N_IDX_HEADS ** -0.5 * IDX_DIM ** -0.5)
    u_ref[...] = rest[:, LANES:LANES + WIDTH_B]
    p_ref[...] = rest[:, LANES + WIDTH_B:]


def _proj(x2, mod, g1, w_in_pad, g_q, g_k, tabs_q, tabs_i, batch, seq, tm):
    n, d = x2.shape
    tpb = seq // tm
    row = lambda i: (i, 0)
    const = lambda i: (0, 0)
    tab_spec = pl.BlockSpec((tm, LANES), row)
    return pl.pallas_call(
        _proj_kernel,
        grid=(n // tm,),
        in_specs=[
            pl.BlockSpec((tm, d), row),
            pl.BlockSpec((1, 6, d), lambda i: (i // tpb, 0, 0)),
            pl.BlockSpec((1, d), const),
            pl.BlockSpec((d, D_IN_PAD), const),
            pl.BlockSpec((1, HEAD_DIM), const),
            pl.BlockSpec((1, HEAD_DIM), const),
            tab_spec, tab_spec, tab_spec, tab_spec, tab_spec, tab_spec,
        ],
        out_specs=[
            pl.BlockSpec((tm, d), row),
            pl.BlockSpec((tm * N_HEADS_A, HEAD_DIM), row),
            pl.BlockSpec((tm, HEAD_DIM), row),
            pl.BlockSpec((tm, HEAD_DIM), row),
            pl.BlockSpec((tm * N_IDX_PAIRS, LANES), row),
            pl.BlockSpec((tm, 2 * LANES), row),
            pl.BlockSpec((tm, LANES), row),
            pl.BlockSpec((tm, WIDTH_B), lambda i: (i % tpb, i // tpb)),
            pl.BlockSpec((tm, WIDTH_C), row),
        ],
        out_shape=[
            jax.ShapeDtypeStruct((n, d), BF16),
            jax.ShapeDtypeStruct((n * N_HEADS_A, HEAD_DIM), BF16),
            jax.ShapeDtypeStruct((n, HEAD_DIM), BF16),
            jax.ShapeDtypeStruct((n, HEAD_DIM), BF16),
            jax.ShapeDtypeStruct((n * N_IDX_PAIRS, LANES), BF16),
            jax.ShapeDtypeStruct((n, 2 * LANES), BF16),
            jax.ShapeDtypeStruct((n, LANES), F32),
            jax.ShapeDtypeStruct((seq, batch * WIDTH_B), F32),
            jax.ShapeDtypeStruct((n, WIDTH_C), F32),
        ],
        compiler_params=_cparams(("arbitrary",)),
        name="proj",
    )(x2, mod, g1, w_in_pad, g_q, g_k, *tabs_q, *tabs_i)


KEY_SLABS = 4
KEY_BLOCK = KEY_SLABS * LANES


def _dsa_scores(qi_ref, ki2_ref, kw_ref, s_ref, nqb):
    row = lax.broadcasted_iota(I32, (Q_BLOCK, KEY_BLOCK), 0)
    lane = lax.broadcasted_iota(I32, (Q_BLOCK, KEY_BLOCK), 1)
    chunk_end = (row // CHUNK + 1) * CHUNK
    trans_b = (((1,), (1,)), ((), ()))

    def qb_body(qq, carry):
        r0 = pl.multiple_of(qq * Q_BLOCK, Q_BLOCK)
        w = kw_ref[pl.ds(r0, Q_BLOCK), :]
        wb = [jnp.broadcast_to(w[:, IDX_DIM + hd:IDX_DIM + hd + 1], (Q_BLOCK, KEY_BLOCK))
              for hd in range(N_IDX_HEADS)]
        limit_row = r0 + chunk_end

        def key_body(j, c2):
            k0 = pl.multiple_of(j * KEY_BLOCK, KEY_BLOCK)
            kia = ki2_ref[pl.ds(k0, KEY_BLOCK), 0:LANES]
            kib = ki2_ref[pl.ds(k0, KEY_BLOCK), LANES:2 * LANES]
            qp = qi_ref[pl.ds(pl.multiple_of(r0 * N_IDX_PAIRS, Q_BLOCK), Q_BLOCK * N_IDX_PAIRS), :]
            sa = lax.dot_general(qp, kia, trans_b, preferred_element_type=F32)
            sb = lax.dot_general(qp, kib, trans_b, preferred_element_type=F32)
            sc = jnp.zeros((Q_BLOCK, KEY_BLOCK), F32)
            for pr in range(N_IDX_PAIRS):
                rows = slice(pr * Q_BLOCK, (pr + 1) * Q_BLOCK)
                sc = sc + jnp.maximum(sa[rows], 0.0) * wb[2 * pr] + jnp.maximum(sb[rows], 0.0) * wb[2 * pr + 1]
            adm = ((k0 + lane) < limit_row) & (sc > NEG_INF * 0.5)
            bits = pltpu.bitcast(sc, I32)
            bits = jnp.where(bits == INT_MIN, 0, bits)
            skey = bits ^ ((bits >> 31) & 0x7FFFFFFF)
            skey = jnp.where(adm, skey, INT_MIN)
            for sl in range(KEY_SLABS):
                s_ref[j * KEY_SLABS + sl, pl.ds(r0, Q_BLOCK), :] = skey[:, sl * LANES:(sl + 1) * LANES]
            return c2

        nkb = (r0 + Q_BLOCK + KEY_BLOCK - 1) // KEY_BLOCK
        lax.fori_loop(0, nkb, key_body, 0)
        return carry

    lax.fori_loop(0, nqb, qb_body, 0)


def _dsa_threshold(s_ref, thr_ref, cnt_ref, nqb, topk):
    thr_ref[...] = jnp.full(thr_ref.shape, INT_MIN, I32)

    ones = jnp.ones((LANES, LANES), BF16)

    def count_ge(offset):
        for qq in range(nqb):
            rows = slice(qq * Q_BLOCK, (qq + 1) * Q_BLOCK)
            cand = thr_ref[rows, :] + offset
            acc = jnp.zeros((Q_BLOCK, LANES), F32)
            for s in range(qq + 1):
                acc = acc + jnp.where(s_ref[s, rows, :] >= cand, 1.0, 0.0)
            cnt_ref[rows, :] = acc.astype(BF16)
        return jnp.dot(cnt_ref[...], ones, preferred_element_type=F32)

    def bit_body(i, carry):
        bitval = jnp.left_shift(jnp.int32(1), 31 - i)
        cnt = count_ge(bitval)
        thr = thr_ref[...]
        thr_ref[...] = jnp.where(cnt >= float(topk), thr + bitval, thr)
        return carry

    lax.fori_loop(0, 32, bit_body, 0)

    surplus = jnp.where((count_ge(0) > float(topk)) & (thr_ref[...] > INT_MIN), 1.0, 0.0).astype(BF16)
    seq = thr_ref.shape[0]
    block_of_row = lax.broadcasted_iota(I32, (nqb, seq), 1) // Q_BLOCK
    in_block = jnp.where(block_of_row == lax.broadcasted_iota(I32, (nqb, seq), 0), 1.0, 0.0).astype(BF16)
    surplus_rows = jnp.dot(in_block, surplus, preferred_element_type=F32)
    li = lax.broadcasted_iota(I32, (LANES, LANES), 0)
    lj = lax.broadcasted_iota(I32, (LANES, LANES), 1)
    upper = jnp.where(li <= lj, 1.0, 0.0).astype(BF16)

    for qq in range(nqb):
        rows = slice(qq * Q_BLOCK, (qq + 1) * Q_BLOCK)

        @pl.when(surplus_rows[qq, 0] > 0.0)
        def _():
            thr = thr_ref[rows, :]

            n_gt = jnp.zeros((Q_BLOCK, LANES), F32)
            for s in range(qq + 1):
                n_gt = n_gt + jnp.where(s_ref[s, rows, :] > thr, 1.0, 0.0)
            keep = float(topk) - jnp.dot(n_gt.astype(BF16), ones, preferred_element_type=F32)
            seen = jnp.zeros((Q_BLOCK, LANES), F32)
            for s in range(qq + 1):
                sk = s_ref[s, rows, :]
                tie = sk == thr
                tie_b = jnp.where(tie, 1.0, 0.0).astype(BF16)
                rank = seen + jnp.dot(tie_b, upper, preferred_element_type=F32)
                drop = tie & (rank > keep) & (thr > INT_MIN)
                s_ref[s, rows, :] = jnp.where(drop, thr - 1, sk)
                seen = seen + jnp.dot(tie_b, ones, preferred_element_type=F32)


def _dsa_kernel(q_ref, k_ref, v_ref, qi_ref, ki2_ref, kw_ref, o_ref,
                s_ref, thr_ref, cnt_ref, bias_ref, lg_ref, p_ref, acc_ref, vx_ref, *m_refs,
                topk, nqb):
    qb = pl.program_id(1)

    @pl.when(qb == 0)
    def _():
        _dsa_scores(qi_ref, ki2_ref, kw_ref, s_ref, nqb)
        _dsa_threshold(s_ref, thr_ref, cnt_ref, nqb, topk)
        vx_ref[:, 0:HEAD_DIM] = v_ref[...]
        vx_ref[:, HEAD_DIM:] = jnp.ones((vx_ref.shape[0], LANES), BF16)

    r0 = pl.multiple_of(qb * Q_BLOCK, Q_BLOCK)
    nkb = (r0 + Q_BLOCK + KEY_BLOCK - 1) // KEY_BLOCK
    trans_b = (((1,), (1,)), ((), ()))
    thr = thr_ref[pl.ds(r0, Q_BLOCK), :]

    def bias_body(j, carry):
        for sl in range(KEY_SLABS):
            sk = s_ref[j * KEY_SLABS + sl, pl.ds(r0, Q_BLOCK), :]
            sel = (sk >= thr) & (sk > INT_MIN)
            bias_ref[j * KEY_SLABS + sl] = jnp.where(sel, 0.0, NEG_INF)
        return carry

    lax.fori_loop(0, nkb, bias_body, 0)

    for hd in range(N_HEADS_A):
        m_refs[hd][...] = jnp.full((Q_BLOCK, LANES), NEG_INF, F32)
    acc_ref[...] = jnp.zeros(acc_ref.shape, F32)

    def max_body(j, carry):
        k0 = pl.multiple_of(j * KEY_BLOCK, KEY_BLOCK)
        s = lax.dot_general(q_ref[...], k_ref[pl.ds(k0, KEY_BLOCK), :], trans_b, preferred_element_type=F32)
        for hd in range(N_HEADS_A):
            rows = slice(hd * Q_BLOCK, (hd + 1) * Q_BLOCK)
            m = m_refs[hd][...]
            for sl in range(KEY_SLABS):
                sh = s[rows, sl * LANES:(sl + 1) * LANES] + bias_ref[j * KEY_SLABS + sl]
                lg_ref[j * KEY_SLABS + sl, rows, :] = sh
                m = jnp.maximum(m, sh)
            m_refs[hd][...] = m
        return carry

    lax.fori_loop(0, nkb, max_body, 0)
    for hd in range(N_HEADS_A):
        m_refs[hd][...] = jnp.broadcast_to(jnp.max(m_refs[hd][...], axis=1, keepdims=True), (Q_BLOCK, LANES))

    def sum_body(j, carry):
        k0 = pl.multiple_of(j * KEY_BLOCK, KEY_BLOCK)
        for hd in range(N_HEADS_A):
            rows = slice(hd * Q_BLOCK, (hd + 1) * Q_BLOCK)
            m = m_refs[hd][...]
            for sl in range(KEY_SLABS):
                p_s = jnp.exp2(lg_ref[j * KEY_SLABS + sl, rows, :] - m)
                p_ref[rows, sl * LANES:(sl + 1) * LANES] = p_s.astype(BF16)
        acc_ref[...] += jnp.dot(p_ref[...], vx_ref[pl.ds(k0, KEY_BLOCK), :], preferred_element_type=F32)
        return carry

    lax.fori_loop(0, nkb, sum_body, 0)
    for hd in range(N_HEADS_A):
        rows = slice(hd * Q_BLOCK, (hd + 1) * Q_BLOCK)
        o_ref[:, hd * HEAD_DIM:(hd + 1) * HEAD_DIM] = (
            acc_ref[rows, 0:HEAD_DIM] / acc_ref[rows, HEAD_DIM:]).astype(BF16)


def _dsa(q, k, v, qi, ki2, kw, batch, seq, topk):
    assert seq % KEY_BLOCK == 0
    n = k.shape[0]
    nqb = seq // Q_BLOCK
    qrow = lambda b, i: (b * nqb + i, 0)
    brow = lambda b, i: (b, 0)
    nsl = seq // LANES
    head_scratch = [pltpu.VMEM((Q_BLOCK, LANES), F32)] * N_HEADS_A
    return pl.pallas_call(
        functools.partial(_dsa_kernel, topk=topk, nqb=nqb),
        grid=(batch, nqb),
        in_specs=[
            pl.BlockSpec((Q_BLOCK * N_HEADS_A, HEAD_DIM), qrow),
            pl.BlockSpec((seq, HEAD_DIM), brow),
            pl.BlockSpec((seq, HEAD_DIM), brow),
            pl.BlockSpec((seq * N_IDX_PAIRS, LANES), brow),
            pl.BlockSpec((seq, 2 * LANES), brow),
            pl.BlockSpec((seq, LANES), brow),
        ],
        out_specs=pl.BlockSpec((Q_BLOCK, WIDTH_A), qrow),
        out_shape=jax.ShapeDtypeStruct((n, WIDTH_A), BF16),
        scratch_shapes=[
            pltpu.VMEM((nsl, seq, LANES), I32),
            pltpu.VMEM((seq, LANES), I32),
            pltpu.VMEM((seq, LANES), BF16),
            pltpu.VMEM((nsl, Q_BLOCK, LANES), F32),
            pltpu.VMEM((nsl, Q_BLOCK * N_HEADS_A, LANES), F32),
            pltpu.VMEM((Q_BLOCK * N_HEADS_A, KEY_BLOCK), BF16),
            pltpu.VMEM((Q_BLOCK * N_HEADS_A, HEAD_DIM + LANES), F32),
            pltpu.VMEM((seq, HEAD_DIM + LANES), BF16),
        ] + head_scratch,
        compiler_params=_cparams(("arbitrary", "arbitrary")),
        name="dsa",
    )(q, k, v, qi, ki2, kw)


def _s5_kernel(u_ref, bm_ref, cm_ref, lre_ref, lim_ref, dsk_ref, wglu_ref, o_ref,
               bu_ref, st_ref, y_ref, *, batch, tl):
    @pl.when(pl.program_id(0) == 0)
    def _():
        st_ref[...] = jnp.zeros(st_ref.shape, F32)

    for gb in range(N_SSM_BUNDLES):
        ch = slice(gb * BUNDLE_CH, (gb + 1) * BUNDLE_CH)
        u_g = u_ref[:, ch]
        bu_ref[...] = jnp.dot(u_g.astype(BF16), bm_ref[gb], preferred_element_type=F32)
        lam_re = jnp.broadcast_to(lre_ref[gb], (batch, BUNDLE_ST))
        lam_im = jnp.broadcast_to(lim_ref[gb], (batch, BUNDLE_ST))

        def step(t, carry):
            x_re, x_im = carry
            r0 = pl.multiple_of(t * batch, batch)
            n_re = lam_re * x_re - lam_im * x_im + bu_ref[pl.ds(r0, batch), 0:BUNDLE_ST]
            n_im = lam_re * x_im + lam_im * x_re + bu_ref[pl.ds(r0, batch), BUNDLE_ST:2 * BUNDLE_ST]
            bu_ref[pl.ds(r0, batch), 0:BUNDLE_ST] = n_re
            bu_ref[pl.ds(r0, batch), BUNDLE_ST:2 * BUNDLE_ST] = n_im
            return n_re, n_im

        x_re, x_im = lax.fori_loop(
            0, tl, step, (st_ref[gb, :, 0:BUNDLE_ST], st_ref[gb, :, BUNDLE_ST:2 * BUNDLE_ST]))
        st_ref[gb, :, 0:BUNDLE_ST] = x_re
        st_ref[gb, :, BUNDLE_ST:2 * BUNDLE_ST] = x_im
        y = jnp.dot(bu_ref[...].astype(BF16), cm_ref[gb], preferred_element_type=F32)
        y_ref[:, ch] = y + dsk_ref[:, ch] * u_g

    y = jax.nn.gelu(y_ref[...])
    gl = jnp.dot(y.astype(BF16), wglu_ref[...], preferred_element_type=F32)
    o_ref[...] = (y * jax.nn.sigmoid(gl)).astype(BF16)


def _s5_params(a_re, a_im, b_re, b_im, c_re, c_im, log_dt):
    dt = jnp.exp(log_dt)[:, None]
    mag = jnp.exp(a_re * dt)
    lb_re = mag * jnp.cos(a_im * dt)
    lb_im = mag * jnp.sin(a_im * dt)
    nr, ni = lb_re - 1.0, lb_im
    den = a_re * a_re + a_im * a_im
    q_re = (nr * a_re + ni * a_im) / den
    q_im = (ni * a_re - nr * a_im) / den
    bb_re = q_re[..., None] * b_re - q_im[..., None] * b_im
    bb_im = q_re[..., None] * b_im + q_im[..., None] * b_re
    eye = jnp.eye(SSM_BUNDLE, dtype=F32)
    nb = N_SSM_BUNDLES

    def pack_b(m):
        m = m.reshape(nb, SSM_BUNDLE, SSM_STATE, SSM_GROUP)
        return jnp.einsum('bgpi,gh->bgihp', m, eye).reshape(nb, BUNDLE_CH, BUNDLE_ST)

    def pack_c(m):
        m = m.reshape(nb, SSM_BUNDLE, SSM_GROUP, SSM_STATE)
        return jnp.einsum('bgop,gh->bgpho', m, eye).reshape(nb, BUNDLE_ST, BUNDLE_CH)

    bm = jnp.concatenate([pack_b(bb_re), pack_b(bb_im)], axis=2).astype(BF16)
    cm = jnp.concatenate([pack_c(c_re), -pack_c(c_im)], axis=1).astype(BF16)
    return bm, cm, lb_re.reshape(nb, 1, BUNDLE_ST), lb_im.reshape(nb, 1, BUNDLE_ST)


def _s5(u_tm, bm, cm, lre, lim, d_skip, w_glu, batch, seq, tl):
    rows = tl * batch
    const3 = lambda t: (0, 0, 0)
    const2 = lambda t: (0, 0)
    return pl.pallas_call(
        functools.partial(_s5_kernel, batch=batch, tl=tl),
        grid=(seq // tl,),
        in_specs=[
            pl.BlockSpec((rows, WIDTH_B), lambda t: (t, 0)),
            pl.BlockSpec(bm.shape, const3),
            pl.BlockSpec(cm.shape, const3),
            pl.BlockSpec(lre.shape, const3),
            pl.BlockSpec(lim.shape, const3),
            pl.BlockSpec((1, WIDTH_B), const2),
            pl.BlockSpec((WIDTH_B, WIDTH_B), const2),
        ],
        out_specs=pl.BlockSpec((rows, WIDTH_B), lambda t: (t, 0)),
        out_shape=jax.ShapeDtypeStruct((seq * batch, WIDTH_B), BF16),
        scratch_shapes=[
            pltpu.VMEM((rows, 2 * BUNDLE_ST), F32),
            pltpu.VMEM((N_SSM_BUNDLES, batch, 2 * BUNDLE_ST), F32),
            pltpu.VMEM((rows, WIDTH_B), F32),
        ],
        compiler_params=_cparams(("arbitrary",)),
        name="s5",
    )(u_tm, bm, cm, lre, lim, d_skip, w_glu)


def _pool_kernel(p_ref, w_ref, sc_ref, o_ref):
    seq = p_ref.shape[0]
    t = lax.broadcasted_iota(I32, (seq, POOL_GROUP), 0)
    t1 = (t + 1).astype(F32)
    for g, win in enumerate(POOL_WINDOWS):
        ch = slice(g * POOL_GROUP, (g + 1) * POOL_GROUP)
        x = p_ref[:, ch]
        s = x
        sh = 1
        while sh < win:
            s = s + jnp.where(t >= sh, pltpu.roll(s, sh, 0), 0.0)
            sh *= 2
        pooled = s / jnp.minimum(t1, float(win)) - x
        y = jnp.dot(pooled.astype(BF16), w_ref[g], preferred_element_type=F32)
        o_ref[:, ch] = (y * sc_ref[:, ch]).astype(BF16)


def _pool(p, w_pool, pool_scale, batch, seq):
    return pl.pallas_call(
        _pool_kernel,
        grid=(batch,),
        in_specs=[
            pl.BlockSpec((seq, WIDTH_C), lambda b: (b, 0)),
            pl.BlockSpec(w_pool.shape, lambda b: (0, 0, 0)),
            pl.BlockSpec((1, WIDTH_C), lambda b: (0, 0)),
        ],
        out_specs=pl.BlockSpec((seq, WIDTH_C), lambda b: (b, 0)),
        out_shape=jax.ShapeDtypeStruct(p.shape, BF16),
        compiler_params=_cparams(("arbitrary",)),
        name="pool",
    )(p, w_pool, pool_scale)


def _col_tiles(w, tn):
    *lead, k, n = w.shape
    nl = len(lead)
    return w.reshape(*lead, k, n // tn, tn).transpose(*range(nl), nl + 1, nl, nl + 2)


def _merge_kernel(h_ref, xc_ref, oa_ref, ob_ref, oc_ref, wg_ref, bg_ref,
                  pa_ref, pb_ref, pc_ref, wo_ref, gt_ref, o_ref, mg_ref, *, ntiles, tn):
    i = pl.program_id(0)
    j = pl.program_id(1)
    slot = i % 2

    @pl.when(i < ntiles)
    def _():
        h = h_ref[...]
        branches = ((oa_ref, pa_ref), (ob_ref, pb_ref), (oc_ref, pc_ref))
        merged = None
        for b, (o_r, p_r) in enumerate(branches):
            gate = jax.nn.sigmoid(jnp.dot(h, wg_ref[b, 0], preferred_element_type=F32) + bg_ref[0, b:b + 1, :])
            term = gate * jnp.dot(o_r[...], p_r[0], preferred_element_type=F32)
            merged = term if merged is None else merged + term
        c0 = pl.multiple_of(j * tn, tn)
        mg_ref[slot, :, pl.ds(c0, tn)] = merged.astype(BF16)

    @pl.when(i >= 1)
    def _():
        mix = jnp.dot(mg_ref[1 - slot], wo_ref[0], preferred_element_type=F32)
        o_ref[...] = xc_ref[...] + gt_ref[0] * mix


def _merge(x2, h, gt, o_a, o_b_tm, o_c, wg, bg, p_a, p_b, p_c, w_out, seq, tm, tn):
    n, d = x2.shape
    nj = d // tn
    ntiles = n // tm
    tpb = seq // tm
    cur = lambda i: jnp.minimum(i, ntiles - 1)
    prev = lambda i: jnp.maximum(i - 1, 0)
    row = lambda i, j: (cur(i), 0)
    wt = lambda i, j: (j, 0, 0)
    return pl.pallas_call(
        functools.partial(_merge_kernel, ntiles=ntiles, tn=tn),
        grid=(ntiles + 1, nj),
        in_specs=[
            pl.BlockSpec((tm, d), row),
            pl.BlockSpec((tm, tn), lambda i, j: (prev(i), j)),
            pl.BlockSpec((tm, WIDTH_A), row),
            pl.BlockSpec((tm, WIDTH_B), lambda i, j: (cur(i) % tpb, cur(i) // tpb)),
            pl.BlockSpec((tm, WIDTH_C), row),
            pl.BlockSpec((3, 1, d, tn), lambda i, j: (0, j, 0, 0)),
            pl.BlockSpec((1, 3, tn), wt),
            pl.BlockSpec((1, WIDTH_A, tn), wt),
            pl.BlockSpec((1, WIDTH_B, tn), wt),
            pl.BlockSpec((1, WIDTH_C, tn), wt),
            pl.BlockSpec((1, d, tn), wt),
            pl.BlockSpec((1, 1, tn), lambda i, j: (prev(i) // tpb, 0, j)),
        ],
        out_specs=pl.BlockSpec((tm, tn), lambda i, j: (prev(i), jnp.where(i == 0, 0, j))),
        out_shape=jax.ShapeDtypeStruct((n, d), F32),
        scratch_shapes=[pltpu.VMEM((2, tm, d), BF16)],
        compiler_params=_cparams(("arbitrary", "arbitrary")),
        name="merge",
    )(h, x2, o_a, o_b_tm, o_c, _col_tiles(wg, tn), _col_tiles(bg, tn), _col_tiles(p_a, tn),
      _col_tiles(p_b, tn), _col_tiles(p_c, tn), _col_tiles(w_out, tn), gt)


FFN_HALO = 16


def _ffn_kernel(x_ref, xh_ref, mod_ref, g_ref, wa_ref, wb_ref, cw_ref, cb_ref, wd_ref, o_ref,
                h_ref, acc_ref, *, tm, tpb):
    i = pl.program_id(0)
    j = pl.program_id(1)
    nj = pl.num_programs(1)

    @pl.when(j == 0)
    def _():
        g, sc, sh = g_ref[...], mod_ref[0, 4:5, :], mod_ref[0, 3:4, :]
        halo = _norm_mod(xh_ref[...], g, sc, sh)
        h_ref[0:FFN_HALO, :] = jnp.where(i % tpb == 0, 0.0, halo).astype(BF16)
        h_ref[FFN_HALO:, :] = _norm_mod(x_ref[...], g, sc, sh).astype(BF16)
        acc_ref[...] = jnp.zeros(acc_ref.shape, F32)

    a = jnp.dot(h_ref[...], wa_ref[0], preferred_element_type=F32)
    b = jnp.dot(h_ref[FFN_HALO:, :], wb_ref[0], preferred_element_type=F32)
    a_conv = cb_ref[...] + a[FFN_HALO - 2:FFN_HALO - 2 + tm] * cw_ref[0:1, :]
    a_conv = a_conv + a[FFN_HALO - 1:FFN_HALO - 1 + tm] * cw_ref[1:2, :]
    a_conv = a_conv + a[FFN_HALO:] * cw_ref[2:3, :]
    act = (a_conv * jax.nn.sigmoid(a_conv)) * b
    acc_ref[...] += jnp.dot(act.astype(BF16), wd_ref[...], preferred_element_type=F32)

    @pl.when(j == nj - 1)
    def _():
        o_ref[...] = x_ref[...] + mod_ref[0, 5:6, :] * acc_ref[...]


def _ffn(x2, mod, g2, w_a, w_b, conv_w, conv_b, w_down, seq, tm, tn):
    n, d = x2.shape
    dff = w_a.shape[1]
    tpb = seq // tm
    hb = tm // FFN_HALO
    return pl.pallas_call(
        functools.partial(_ffn_kernel, tm=tm, tpb=tpb),
        grid=(n // tm, dff // tn),
        in_specs=[
            pl.BlockSpec((tm, d), lambda i, j: (i, 0)),
            pl.BlockSpec((FFN_HALO, d), lambda i, j: (jnp.maximum(i * hb - 1, 0), 0)),
            pl.BlockSpec((1, 6, d), lambda i, j: (i // tpb, 0, 0)),
            pl.BlockSpec((1, d), lambda i, j: (0, 0)),
            pl.BlockSpec((1, d, tn), lambda i, j: (j, 0, 0)),
            pl.BlockSpec((1, d, tn), lambda i, j: (j, 0, 0)),
            pl.BlockSpec((CONV_WIDTH, tn), lambda i, j: (0, j)),
            pl.BlockSpec((1, tn), lambda i, j: (0, j)),
            pl.BlockSpec((tn, d), lambda i, j: (j, 0)),
        ],
        out_specs=pl.BlockSpec((tm, d), lambda i, j: (i, 0)),
        out_shape=jax.ShapeDtypeStruct((n, d), F32),
        scratch_shapes=[pltpu.VMEM((tm + FFN_HALO, d), BF16), pltpu.VMEM((tm, d), F32)],
        compiler_params=_cparams(("arbitrary", "arbitrary")),
        name="ffn",
    )(x2, x2, mod, g2, _col_tiles(w_a, tn), _col_tiles(w_b, tn), conv_w, conv_b, w_down)


def _rope_tables(positions, rot_dim, period):
    half = rot_dim // 2
    inv_freq = ROPE_THETA ** (-jnp.arange(half, dtype=F32) * (2.0 / rot_dim))
    ang = positions.astype(F32)[..., None] * inv_freq
    cos, sin = jnp.cos(ang), jnp.sin(ang)
    rest = period - 2 * half
    ones = jnp.ones(cos.shape[:-1] + (rest,), F32)
    zh = jnp.zeros_like(sin)
    zr = jnp.zeros_like(ones)
    reps = LANES // period
    out = []
    for parts in ((cos, cos, ones), (-sin, zh, zr), (zh, sin, zr)):
        t = jnp.concatenate(parts, axis=-1)
        out.append(jnp.tile(t, (1, 1, reps)).reshape(-1, LANES))
    return out


def _pad_cols(w, new):
    return jnp.pad(w, ((0, 0), (0, new - w.shape[1])))


def kernel(x, c, positions, w_ada, b_ada, g_norm1, g_norm2, w_in, g_q, g_k, a_re, a_im, b_re, b_im,
           c_re, c_im, d_skip, log_dt, w_glu, w_pool, pool_scale, p_a, p_b, p_c, w_gate, b_gate,
           w_out, w_up, conv_w, conv_b, w_down, *, tm=512, tm_merge=1024, tn_merge=256, tn_ffn=512, tl=32):
    batch, seq, d = x.shape
    depth = w_ada.shape[0]
    n = batch * seq
    tm = min(tm, seq)
    tm_merge = min(tm_merge, seq)
    tl = min(tl, seq)
    topk = min(TOPK_MAX, seq // 4)
    dff_pad = -(-D_FF // tn_ffn) * tn_ffn

    mod_all = _ada(c, w_ada, b_ada).reshape(depth, batch, 6, d)
    tabs_q = _rope_tables(positions, ROT_DIM, HEAD_DIM)
    tabs_i = _rope_tables(positions, IDX_ROT_DIM, IDX_DIM)

    x2 = x.reshape(n, d)
    for l in range(depth):
        mod = mod_all[l]
        gt1 = mod[:, 2:3, :]
        w_in_pad = jnp.concatenate(
            [w_in[l][:, :COL_KW + KW_USED],
             jnp.zeros((d, LANES - KW_USED), F32),
             w_in[l][:, COL_KW + KW_USED:]], axis=1).astype(BF16)
        h, q, k, v, qi, ki2, kw, u_tm, p = _proj(
            x2, mod, g_norm1[l][None], w_in_pad, g_q[l][None], g_k[l][None], tabs_q, tabs_i,
            batch, seq, tm)
        o_a = _dsa(q, k, v, qi, ki2, kw, batch, seq, topk)
        bm, cm, lre, lim = _s5_params(a_re[l], a_im[l], b_re[l], b_im[l], c_re[l], c_im[l], log_dt[l])
        o_b = _s5(u_tm.reshape(seq * batch, WIDTH_B), bm, cm, lre, lim, d_skip[l][None],
                  w_glu[l].astype(BF16), batch, seq, tl)
        o_c = _pool(p, w_pool[l].astype(BF16), pool_scale[l][None], batch, seq)
        x2 = _merge(x2, h, gt1, o_a, o_b.reshape(seq, batch * WIDTH_B), o_c,
                    w_gate[l].astype(BF16), b_gate[l], p_a[l].astype(BF16), p_b[l].astype(BF16),
                    p_c[l].astype(BF16), w_out[l].astype(BF16), seq, tm_merge, tn_merge)
        w_a = _pad_cols(w_up[l][:, :D_FF], dff_pad).astype(BF16)
        w_b = _pad_cols(w_up[l][:, D_FF:], dff_pad).astype(BF16)
        w_d = jnp.pad(w_down[l], ((0, dff_pad - D_FF), (0, 0))).astype(BF16)
        x2 = _ffn(x2, mod, g_norm2[l][None], w_a, w_b, _pad_cols(conv_w[l], dff_pad),
                  _pad_cols(conv_b[l][None], dff_pad), w_d, seq, tm, tn_ffn)
    return x2.reshape(batch, seq, d)
```

```python
import functools
import math

import jax
import jax.numpy as jnp
from jax import lax
from jax.experimental import pallas as pl
from jax.experimental.pallas import tpu as pltpu

F32 = jnp.float32
BF16 = jnp.bfloat16
I32 = jnp.int32

D_MODEL = 2048
DEPTH = 2
CHUNK = 64
EPS = 1e-6
NEG_INF = -1e30
ROPE_THETA = 500000.0

N_HEADS_A = 8
HEAD_DIM = 128
ROT_DIM = HEAD_DIM // 4
N_IDX_HEADS = 8
IDX_DIM = 64
IDX_ROT_DIM = IDX_DIM // 4
N_IDX_PAIRS = N_IDX_HEADS * IDX_DIM // 128
TOPK_MAX = 256
Q_BLOCK = 128
WIDTH_A = N_HEADS_A * HEAD_DIM

WIDTH_B = D_MODEL // 4
SSM_GROUP = 16
N_SSM_GROUPS = WIDTH_B // SSM_GROUP
SSM_STATE = 64
SSM_BUNDLE = 8
N_SSM_BUNDLES = N_SSM_GROUPS // SSM_BUNDLE
BUNDLE_CH = SSM_BUNDLE * SSM_GROUP
BUNDLE_ST = SSM_BUNDLE * SSM_STATE

WIDTH_C = D_MODEL // 4
POOL_WINDOWS = (2, 4, 8, 16)
POOL_GROUP = WIDTH_C // 4

D_FF = 5504
CONV_WIDTH = 3

LOGIT_SCALE = HEAD_DIM ** -0.5 * math.log2(math.e)
LANES = 128
INT_MIN = -(2 ** 31)

COL_K = WIDTH_A
COL_V = COL_K + HEAD_DIM
COL_QI = COL_V + HEAD_DIM
COL_KW = COL_QI + N_IDX_HEADS * IDX_DIM
COL_U = COL_KW + LANES
COL_P = COL_U + WIDTH_B
D_IN_PAD = COL_P + WIDTH_C
KW_USED = IDX_DIM + N_IDX_HEADS

VMEM_LIMIT = 56 * 1024 * 1024


def _cparams(sem):
    return pltpu.CompilerParams(dimension_semantics=sem, vmem_limit_bytes=VMEM_LIMIT)


def _norm_mod(x, g, sc, sh):
    ms = jnp.mean(x * x, axis=-1, keepdims=True)
    y = x * lax.rsqrt(ms + EPS)
    return (y * g) * (1.0 + sc) + sh


def _rope(x, c, sa, sb, half):
    return x * c + pltpu.roll(x, LANES - half, 1) * sa + pltpu.roll(x, half, 1) * sb


def _ada_kernel(c_ref, w_ref, b_ref, o_ref):
    c = c_ref[...]
    ca = c * jax.nn.sigmoid(c)
    o_ref[0] = jnp.dot(ca.astype(BF16), w_ref[0].astype(BF16), preferred_element_type=F32) + b_ref[0]


def _ada(c, w_ada, b_ada, tn=1024):
    depth, d, n = w_ada.shape
    b = c.shape[0]
    return pl.pallas_call(
        _ada_kernel,
        grid=(depth, n // tn),
        in_specs=[
            pl.BlockSpec((b, d), lambda l, j: (0, 0)),
            pl.BlockSpec((1, d, tn), lambda l, j: (l, 0, j)),
            pl.BlockSpec((1, 1, tn), lambda l, j: (l, 0, j)),
        ],
        out_specs=pl.BlockSpec((1, b, tn), lambda l, j: (l, 0, j)),
        out_shape=jax.ShapeDtypeStruct((depth, b, n), F32),
        compiler_params=_cparams(("arbitrary", "arbitrary")),
        name="ada",
    )(c, w_ada, b_ada.reshape(depth, 1, n))


def _proj_kernel(x_ref, mod_ref, g_ref, w_ref, gq_ref, gk_ref, cq_ref, saq_ref, sbq_ref,
                 ci_ref, sai_ref, sbi_ref,
                 h_ref, q_ref, k_ref, v_ref, qi_ref, ki2_ref, kw_ref, u_ref, p_ref):
    h = _norm_mod(x_ref[...], g_ref[...], mod_ref[0, 1:2, :], mod_ref[0, 0:1, :]).astype(BF16)
    h_ref[...] = h

    def mm(c0, width):
        return jnp.dot(h, w_ref[:, c0:c0 + width], preferred_element_type=F32)

    def qk_norm_rope(xh, g):
        ms = jnp.mean(xh * xh, axis=-1, keepdims=True)
        y = xh * lax.rsqrt(ms + EPS) * g
        return _rope(y, cq_ref[...], saq_ref[...], sbq_ref[...], ROT_DIM // 2)

    def store_stacked(ref, slab, idx, nslab):
        for qbl in range(slab.shape[0] // Q_BLOCK):
            r0 = (qbl * nslab + idx) * Q_BLOCK
            ref[r0:r0 + Q_BLOCK, :] = slab[qbl * Q_BLOCK:(qbl + 1) * Q_BLOCK]

    q_all = mm(0, WIDTH_A)
    for hd in range(N_HEADS_A):
        qh = (qk_norm_rope(q_all[:, hd * HEAD_DIM:(hd + 1) * HEAD_DIM], gq_ref[...]) * LOGIT_SCALE).astype(BF16)
        store_stacked(q_ref, qh, hd, N_HEADS_A)
    kv = mm(COL_K, 2 * HEAD_DIM)
    k_ref[...] = qk_norm_rope(kv[:, 0:HEAD_DIM], gk_ref[...]).astype(BF16)
    v_ref[...] = kv[:, HEAD_DIM:2 * HEAD_DIM].astype(BF16)
    qi_all = mm(COL_QI, N_IDX_PAIRS * LANES)
    for s in range(N_IDX_PAIRS):
        qi = qi_all[:, s * LANES:(s + 1) * LANES]
        qi = _rope(qi, ci_ref[...], sai_ref[...], sbi_ref[...], IDX_ROT_DIM // 2).astype(BF16)
        store_stacked(qi_ref, qi, s, N_IDX_PAIRS)
    rest = mm(COL_KW, LANES + WIDTH_B + WIDTH_C)
    kw = rest[:, 0:LANES]
    ki = _rope(kw, ci_ref[...], sai_ref[...], sbi_ref[...], IDX_ROT_DIM // 2)
    lane = lax.broadcasted_iota(I32, ki.shape, 1)
    ki_lo = jnp.where(lane < IDX_DIM, ki, 0.0)
    ki2_ref[:, 0:LANES] = ki_lo.astype(BF16)
    ki2_ref[:, LANES:2 * LANES] = pltpu.roll(ki_lo, IDX_DIM, 1).astype(BF16)
    kw_ref[...] = kw * (N_IDX_HEADS ** -0.5 * IDX_DIM ** -0.5)
    u_ref[...] = rest[:, LANES:LANES + WIDTH_B]
    p_ref[...] = rest[:, LANES + WIDTH_B:]


def _proj(x2, mod, g1, w_in_pad, g_q, g_k, tabs_q, tabs_i, batch, seq, tm):
    n, d = x2.shape
    tpb = seq // tm
    row = lambda i: (i, 0)
    const = lambda i: (0, 0)
    tab_spec = pl.BlockSpec((tm, LANES), row)
    return pl.pallas_call(
        _proj_kernel,
        grid=(n // tm,),
        in_specs=[
            pl.BlockSpec((tm, d), row),
            pl.BlockSpec((1, 6, d), lambda i: (i // tpb, 0, 0)),
            pl.BlockSpec((1, d), const),
            pl.BlockSpec((d, D_IN_PAD), const),
            pl.BlockSpec((1, HEAD_DIM), const),
            pl.BlockSpec((1, HEAD_DIM), const),
            tab_spec, tab_spec, tab_spec, tab_spec, tab_spec, tab_spec,
        ],
        out_specs=[
            pl.BlockSpec((tm, d), row),
            pl.BlockSpec((tm * N_HEADS_A, HEAD_DIM), row),
            pl.BlockSpec((tm, HEAD_DIM), row),
            pl.BlockSpec((tm, HEAD_DIM), row),
            pl.BlockSpec((tm * N_IDX_PAIRS, LANES), row),
            pl.BlockSpec((tm, 2 * LANES), row),
            pl.BlockSpec((tm, LANES), row),
            pl.BlockSpec((tm, WIDTH_B), lambda i: (i % tpb, i // tpb)),
            pl.BlockSpec((tm, WIDTH_C), row),
        ],
        out_shape=[
            jax.ShapeDtypeStruct((n, d), BF16),
            jax.ShapeDtypeStruct((n * N_HEADS_A, HEAD_DIM), BF16),
            jax.ShapeDtypeStruct((n, HEAD_DIM), BF16),
            jax.ShapeDtypeStruct((n, HEAD_DIM), BF16),
            jax.ShapeDtypeStruct((n * N_IDX_PAIRS, LANES), BF16),
            jax.ShapeDtypeStruct((n, 2 * LANES), BF16),
            jax.ShapeDtypeStruct((n, LANES), F32),
            jax.ShapeDtypeStruct((seq, batch * WIDTH_B), F32),
            jax.ShapeDtypeStruct((n, WIDTH_C), F32),
        ],
        compiler_params=_cparams(("arbitrary",)),
        name="proj",
    )(x2, mod, g1, w_in_pad, g_q, g_k, *tabs_q, *tabs_i)


KEY_SLABS = 4
KEY_BLOCK = KEY_SLABS * LANES


def _dsa_scores(qi_ref, ki2_ref, kw_ref, s_ref, nqb):
    row = lax.broadcasted_iota(I32, (Q_BLOCK, KEY_BLOCK), 0)
    lane = lax.broadcasted_iota(I32, (Q_BLOCK, KEY_BLOCK), 1)
    chunk_end = (row // CHUNK + 1) * CHUNK
    trans_b = (((1,), (1,)), ((), ()))

    def qb_body(qq, carry):
        r0 = pl.multiple_of(qq * Q_BLOCK, Q_BLOCK)
        w = kw_ref[pl.ds(r0, Q_BLOCK), :]
        wb = [jnp.broadcast_to(w[:, IDX_DIM + hd:IDX_DIM + hd + 1], (Q_BLOCK, LANES))
              for hd in range(N_IDX_HEADS)]
        limit_row = r0 + chunk_end

        def key_body(j, c2):
            k0 = pl.multiple_of(j * KEY_BLOCK, KEY_BLOCK)
            kia = ki2_ref[pl.ds(k0, KEY_BLOCK), 0:LANES]
            kib = ki2_ref[pl.ds(k0, KEY_BLOCK), LANES:2 * LANES]
            qp = qi_ref[pl.ds(pl.multiple_of(r0 * N_IDX_PAIRS, Q_BLOCK), Q_BLOCK * N_IDX_PAIRS), :]
            sa = lax.dot_general(qp, kia, trans_b, preferred_element_type=F32)
            sb = lax.dot_general(qp, kib, trans_b, preferred_element_type=F32)
            slabs = []
            for sl in range(KEY_SLABS):
                cols = slice(sl * LANES, (sl + 1) * LANES)
                acc = jnp.zeros((Q_BLOCK, LANES), F32)
                for pr in range(N_IDX_PAIRS):
                    rows = slice(pr * Q_BLOCK, (pr + 1) * Q_BLOCK)
                    acc = (acc + jnp.maximum(sa[rows, cols], 0.0) * wb[2 * pr]
                           + jnp.maximum(sb[rows, cols], 0.0) * wb[2 * pr + 1])
                slabs.append(acc)
            sc = jnp.concatenate(slabs, axis=1)
            adm = ((k0 + lane) < limit_row) & (sc > NEG_INF * 0.5)
            bits = pltpu.bitcast(sc, I32)
            bits = jnp.where(bits == INT_MIN, 0, bits)
            skey = bits ^ ((bits >> 31) & 0x7FFFFFFF)
            skey = jnp.where(adm, skey, INT_MIN)
            for sl in range(KEY_SLABS):
                s_ref[j * KEY_SLABS + sl, pl.ds(r0, Q_BLOCK), :] = skey[:, sl * LANES:(sl + 1) * LANES]
            return c2

        nkb = (r0 + Q_BLOCK + KEY_BLOCK - 1) // KEY_BLOCK
        lax.fori_loop(0, nkb, key_body, 0)
        return carry

    lax.fori_loop(0, nqb, qb_body, 0)


def _dsa_threshold(s_ref, thr_ref, cnt_ref, nqb, topk):
    thr_ref[...] = jnp.full(thr_ref.shape, INT_MIN, I32)

    ones = jnp.ones((LANES, LANES), BF16)

    def count_ge(offset):
        for qq in range(nqb):
            rows = slice(qq * Q_BLOCK, (qq + 1) * Q_BLOCK)
            cand = thr_ref[rows, :] + offset
            acc = jnp.zeros((Q_BLOCK, LANES), F32)
            for s in range(qq + 1):
                acc = acc + jnp.where(s_ref[s, rows, :] >= cand, 1.0, 0.0)
            cnt_ref[rows, :] = acc.astype(BF16)
        return jnp.dot(cnt_ref[...], ones, preferred_element_type=F32)

    def bit_body(i, carry):
        bitval = jnp.left_shift(jnp.int32(1), 31 - i)
        cnt = count_ge(bitval)
        thr = thr_ref[...]
        thr_ref[...] = jnp.where(cnt >= float(topk), thr + bitval, thr)
        return carry

    lax.fori_loop(0, 32, bit_body, 0)

    surplus = jnp.where((count_ge(0) > float(topk)) & (thr_ref[...] > INT_MIN), 1.0, 0.0).astype(BF16)
    seq = thr_ref.shape[0]
    block_of_row = lax.broadcasted_iota(I32, (nqb, seq), 1) // Q_BLOCK
    in_block = jnp.where(block_of_row == lax.broadcasted_iota(I32, (nqb, seq), 0), 1.0, 0.0).astype(BF16)
    surplus_rows = jnp.dot(in_block, surplus, preferred_element_type=F32)
    li = lax.broadcasted_iota(I32, (LANES, LANES), 0)
    lj = lax.broadcasted_iota(I32, (LANES, LANES), 1)
    upper = jnp.where(li <= lj, 1.0, 0.0).astype(BF16)

    for qq in range(nqb):
        rows = slice(qq * Q_BLOCK, (qq + 1) * Q_BLOCK)

        @pl.when(surplus_rows[qq, 0] > 0.0)
        def _():
            thr = thr_ref[rows, :]

            n_gt = jnp.zeros((Q_BLOCK, LANES), F32)
            for s in range(qq + 1):
                n_gt = n_gt + jnp.where(s_ref[s, rows, :] > thr, 1.0, 0.0)
            keep = float(topk) - jnp.dot(n_gt.astype(BF16), ones, preferred_element_type=F32)
            seen = jnp.zeros((Q_BLOCK, LANES), F32)
            for s in range(qq + 1):
                sk = s_ref[s, rows, :]
                tie = sk == thr
                tie_b = jnp.where(tie, 1.0, 0.0).astype(BF16)
                rank = seen + jnp.dot(tie_b, upper, preferred_element_type=F32)
                drop = tie & (rank > keep) & (thr > INT_MIN)
                s_ref[s, rows, :] = jnp.where(drop, thr - 1, sk)
                seen = seen + jnp.dot(tie_b, ones, preferred_element_type=F32)


def _dsa_kernel(q_ref, k_ref, v_ref, qi_ref, ki2_ref, kw_ref, o_ref,
                s_ref, thr_ref, cnt_ref, bias_ref, lg_ref, p_ref, acc_ref, vx_ref, *m_refs,
                topk, nqb):
    qb = pl.program_id(1)

    @pl.when(qb == 0)
    def _():
        _dsa_scores(qi_ref, ki2_ref, kw_ref, s_ref, nqb)
        _dsa_threshold(s_ref, thr_ref, cnt_ref, nqb, topk)
        vx_ref[:, 0:HEAD_DIM] = v_ref[...]
        vx_ref[:, HEAD_DIM:] = jnp.ones((vx_ref.shape[0], LANES), BF16)

    r0 = pl.multiple_of(qb * Q_BLOCK, Q_BLOCK)
    nkb = (r0 + Q_BLOCK + KEY_BLOCK - 1) // KEY_BLOCK
    trans_b = (((1,), (1,)), ((), ()))
    thr = thr_ref[pl.ds(r0, Q_BLOCK), :]

    def bias_body(j, carry):
        for sl in range(KEY_SLABS):
            sk = s_ref[j * KEY_SLABS + sl, pl.ds(r0, Q_BLOCK), :]
            sel = (sk >= thr) & (sk > INT_MIN)
            bias_ref[j * KEY_SLABS + sl] = jnp.where(sel, 0.0, NEG_INF)
        return carry

    lax.fori_loop(0, nkb, bias_body, 0)

    for hd in range(N_HEADS_A):
        m_refs[hd][...] = jnp.full((Q_BLOCK, LANES), NEG_INF, F32)
    acc_ref[...] = jnp.zeros(acc_ref.shape, F32)

    def max_body(j, carry):
        k0 = pl.multiple_of(j * KEY_BLOCK, KEY_BLOCK)
        s = lax.dot_general(q_ref[...], k_ref[pl.ds(k0, KEY_BLOCK), :], trans_b, preferred_element_type=F32)
        for hd in range(N_HEADS_A):
            rows = slice(hd * Q_BLOCK, (hd + 1) * Q_BLOCK)
            m = m_refs[hd][...]
            for sl in range(KEY_SLABS):
                sh = s[rows, sl * LANES:(sl + 1) * LANES] + bias_ref[j * KEY_SLABS + sl]
                lg_ref[j * KEY_SLABS + sl, rows, :] = sh
                m = jnp.maximum(m, sh)
            m_refs[hd][...] = m
        return carry

    lax.fori_loop(0, nkb, max_body, 0)
    for hd in range(N_HEADS_A):
        m_refs[hd][...] = jnp.broadcast_to(jnp.max(m_refs[hd][...], axis=1, keepdims=True), (Q_BLOCK, LANES))

    def sum_body(j, carry):
        k0 = pl.multiple_of(j * KEY_BLOCK, KEY_BLOCK)
        for hd in range(N_HEADS_A):
            rows = slice(hd * Q_BLOCK, (hd + 1) * Q_BLOCK)
            m = m_refs[hd][...]
            for sl in range(KEY_SLABS):
                p_s = jnp.exp2(lg_ref[j * KEY_SLABS + sl, rows, :] - m)
                p_ref[rows, sl * LANES:(sl + 1) * LANES] = p_s.astype(BF16)
        acc_ref[...] += jnp.dot(p_ref[...], vx_ref[pl.ds(k0, KEY_BLOCK), :], preferred_element_type=F32)
        return carry

    lax.fori_loop(0, nkb, sum_body, 0)
    for hd in range(N_HEADS_A):
        rows = slice(hd * Q_BLOCK, (hd + 1) * Q_BLOCK)
        o_ref[:, hd * HEAD_DIM:(hd + 1) * HEAD_DIM] = (
            acc_ref[rows, 0:HEAD_DIM] / acc_ref[rows, HEAD_DIM:]).astype(BF16)


def _dsa(q, k, v, qi, ki2, kw, batch, seq, topk):
    assert seq % KEY_BLOCK == 0
    n = k.shape[0]
    nqb = seq // Q_BLOCK
    qrow = lambda b, i: (b * nqb + i, 0)
    brow = lambda b, i: (b, 0)
    nsl = seq // LANES
    head_scratch = [pltpu.VMEM((Q_BLOCK, LANES), F32)] * N_HEADS_A
    return pl.pallas_call(
        functools.partial(_dsa_kernel, topk=topk, nqb=nqb),
        grid=(batch, nqb),
        in_specs=[
            pl.BlockSpec((Q_BLOCK * N_HEADS_A, HEAD_DIM), qrow),
            pl.BlockSpec((seq, HEAD_DIM), brow),
            pl.BlockSpec((seq, HEAD_DIM), brow),
            pl.BlockSpec((seq * N_IDX_PAIRS, LANES), brow),
            pl.BlockSpec((seq, 2 * LANES), brow),
            pl.BlockSpec((seq, LANES), brow),
        ],
        out_specs=pl.BlockSpec((Q_BLOCK, WIDTH_A), qrow),
        out_shape=jax.ShapeDtypeStruct((n, WIDTH_A), BF16),
        scratch_shapes=[
            pltpu.VMEM((nsl, seq, LANES), I32),
            pltpu.VMEM((seq, LANES), I32),
            pltpu.VMEM((seq, LANES), BF16),
            pltpu.VMEM((nsl, Q_BLOCK, LANES), F32),
            pltpu.VMEM((nsl, Q_BLOCK * N_HEADS_A, LANES), F32),
            pltpu.VMEM((Q_BLOCK * N_HEADS_A, KEY_BLOCK), BF16),
            pltpu.VMEM((Q_BLOCK * N_HEADS_A, HEAD_DIM + LANES), F32),
            pltpu.VMEM((seq, HEAD_DIM + LANES), BF16),
        ] + head_scratch,
        compiler_params=_cparams(("arbitrary", "arbitrary")),
        name="dsa",
    )(q, k, v, qi, ki2, kw)


def _s5_kernel(u_ref, bm_ref, cm_ref, lre_ref, lim_ref, dsk_ref, wglu_ref, o_ref,
               bu_ref, st_ref, y_ref, *, batch, tl):
    @pl.when(pl.program_id(0) == 0)
    def _():
        st_ref[...] = jnp.zeros(st_ref.shape, F32)

    for gb in range(N_SSM_BUNDLES):
        ch = slice(gb * BUNDLE_CH, (gb + 1) * BUNDLE_CH)
        u_g = u_ref[:, ch]
        bu_ref[...] = jnp.dot(u_g.astype(BF16), bm_ref[gb], preferred_element_type=F32)
        lam_re = jnp.broadcast_to(lre_ref[gb], (batch, BUNDLE_ST))
        lam_im = jnp.broadcast_to(lim_ref[gb], (batch, BUNDLE_ST))

        def step(t, carry):
            x_re, x_im = carry
            r0 = pl.multiple_of(t * batch, batch)
            n_re = lam_re * x_re - lam_im * x_im + bu_ref[pl.ds(r0, batch), 0:BUNDLE_ST]
            n_im = lam_re * x_im + lam_im * x_re + bu_ref[pl.ds(r0, batch), BUNDLE_ST:2 * BUNDLE_ST]
            bu_ref[pl.ds(r0, batch), 0:BUNDLE_ST] = n_re
            bu_ref[pl.ds(r0, batch), BUNDLE_ST:2 * BUNDLE_ST] = n_im
            return n_re, n_im

        x_re, x_im = lax.fori_loop(
            0, tl, step, (st_ref[gb, :, 0:BUNDLE_ST], st_ref[gb, :, BUNDLE_ST:2 * BUNDLE_ST]), unroll=4)
        st_ref[gb, :, 0:BUNDLE_ST] = x_re
        st_ref[gb, :, BUNDLE_ST:2 * BUNDLE_ST] = x_im
        y = jnp.dot(bu_ref[...].astype(BF16), cm_ref[gb], preferred_element_type=F32)
        y_ref[:, ch] = y + dsk_ref[:, ch] * u_g

    y = jax.nn.gelu(y_ref[...])
    gl = jnp.dot(y.astype(BF16), wglu_ref[...], preferred_element_type=F32)
    o_ref[...] = (y * jax.nn.sigmoid(gl)).astype(BF16)


def _s5_params(a_re, a_im, b_re, b_im, c_re, c_im, log_dt):
    dt = jnp.exp(log_dt)[:, None]
    mag = jnp.exp(a_re * dt)
    lb_re = mag * jnp.cos(a_im * dt)
    lb_im = mag * jnp.sin(a_im * dt)
    nr, ni = lb_re - 1.0, lb_im
    den = a_re * a_re + a_im * a_im
    q_re = (nr * a_re + ni * a_im) / den
    q_im = (ni * a_re - nr * a_im) / den
    bb_re = q_re[..., None] * b_re - q_im[..., None] * b_im
    bb_im = q_re[..., None] * b_im + q_im[..., None] * b_re
    eye = jnp.eye(SSM_BUNDLE, dtype=F32)
    nb = N_SSM_BUNDLES

    def pack_b(m):
        m = m.reshape(nb, SSM_BUNDLE, SSM_STATE, SSM_GROUP)
        return jnp.einsum('bgpi,gh->bgihp', m, eye).reshape(nb, BUNDLE_CH, BUNDLE_ST)

    def pack_c(m):
        m = m.reshape(nb, SSM_BUNDLE, SSM_GROUP, SSM_STATE)
        return jnp.einsum('bgop,gh->bgpho', m, eye).reshape(nb, BUNDLE_ST, BUNDLE_CH)

    bm = jnp.concatenate([pack_b(bb_re), pack_b(bb_im)], axis=2).astype(BF16)
    cm = jnp.concatenate([pack_c(c_re), -pack_c(c_im)], axis=1).astype(BF16)
    return bm, cm, lb_re.reshape(nb, 1, BUNDLE_ST), lb_im.reshape(nb, 1, BUNDLE_ST)


def _s5(u_tm, bm, cm, lre, lim, d_skip, w_glu, batch, seq, tl):
    rows = tl * batch
    const3 = lambda t: (0, 0, 0)
    const2 = lambda t: (0, 0)
    return pl.pallas_call(
        functools.partial(_s5_kernel, batch=batch, tl=tl),
        grid=(seq // tl,),
        in_specs=[
            pl.BlockSpec((rows, WIDTH_B), lambda t: (t, 0)),
            pl.BlockSpec(bm.shape, const3),
            pl.BlockSpec(cm.shape, const3),
            pl.BlockSpec(lre.shape, const3),
            pl.BlockSpec(lim.shape, const3),
            pl.BlockSpec((1, WIDTH_B), const2),
            pl.BlockSpec((WIDTH_B, WIDTH_B), const2),
        ],
        out_specs=pl.BlockSpec((rows, WIDTH_B), lambda t: (t, 0)),
        out_shape=jax.ShapeDtypeStruct((seq * batch, WIDTH_B), BF16),
        scratch_shapes=[
            pltpu.VMEM((rows, 2 * BUNDLE_ST), F32),
            pltpu.VMEM((N_SSM_BUNDLES, batch, 2 * BUNDLE_ST), F32),
            pltpu.VMEM((rows, WIDTH_B), F32),
        ],
        compiler_params=_cparams(("arbitrary",)),
        name="s5",
    )(u_tm, bm, cm, lre, lim, d_skip, w_glu)


def _pool_kernel(p_ref, w_ref, sc_ref, o_ref):
    seq = p_ref.shape[0]
    t = lax.broadcasted_iota(I32, (seq, POOL_GROUP), 0)
    t1 = (t + 1).astype(F32)
    for g, win in enumerate(POOL_WINDOWS):
        ch = slice(g * POOL_GROUP, (g + 1) * POOL_GROUP)
        x = p_ref[:, ch]
        s = x
        sh = 1
        while sh < win:
            s = s + jnp.where(t >= sh, pltpu.roll(s, sh, 0), 0.0)
            sh *= 2
        pooled = s / jnp.minimum(t1, float(win)) - x
        y = jnp.dot(pooled.astype(BF16), w_ref[g], preferred_element_type=F32)
        o_ref[:, ch] = (y * sc_ref[:, ch]).astype(BF16)


def _pool(p, w_pool, pool_scale, batch, seq):
    return pl.pallas_call(
        _pool_kernel,
        grid=(batch,),
        in_specs=[
            pl.BlockSpec((seq, WIDTH_C), lambda b: (b, 0)),
            pl.BlockSpec(w_pool.shape, lambda b: (0, 0, 0)),
            pl.BlockSpec((1, WIDTH_C), lambda b: (0, 0)),
        ],
        out_specs=pl.BlockSpec((seq, WIDTH_C), lambda b: (b, 0)),
        out_shape=jax.ShapeDtypeStruct(p.shape, BF16),
        compiler_params=_cparams(("arbitrary",)),
        name="pool",
    )(p, w_pool, pool_scale)


def _col_tiles(w, tn):
    *lead, k, n = w.shape
    nl = len(lead)
    return w.reshape(*lead, k, n // tn, tn).transpose(*range(nl), nl + 1, nl, nl + 2)


def _merge_kernel(h_ref, xc_ref, oa_ref, ob_ref, oc_ref, wg_ref, bg_ref,
                  pa_ref, pb_ref, pc_ref, wo_ref, gt_ref, o_ref, mg_ref, *, ntiles, tn):
    i = pl.program_id(0)
    j = pl.program_id(1)
    slot = i % 2

    @pl.when(i < ntiles)
    def _():
        h = h_ref[...]
        branches = ((oa_ref, pa_ref), (ob_ref, pb_ref), (oc_ref, pc_ref))
        merged = None
        for b, (o_r, p_r) in enumerate(branches):
            gate = jax.nn.sigmoid(jnp.dot(h, wg_ref[b, 0], preferred_element_type=F32) + bg_ref[0, b:b + 1, :])
            term = gate * jnp.dot(o_r[...], p_r[0], preferred_element_type=F32)
            merged = term if merged is None else merged + term
        c0 = pl.multiple_of(j * tn, tn)
        mg_ref[slot, :, pl.ds(c0, tn)] = merged.astype(BF16)

    @pl.when(i >= 1)
    def _():
        mix = jnp.dot(mg_ref[1 - slot], wo_ref[0], preferred_element_type=F32)
        o_ref[...] = xc_ref[...] + gt_ref[0] * mix


def _merge(x2, h, gt, o_a, o_b_tm, o_c, wg, bg, p_a, p_b, p_c, w_out, seq, tm, tn):
    n, d = x2.shape
    nj = d // tn
    ntiles = n // tm
    tpb = seq // tm
    cur = lambda i: jnp.minimum(i, ntiles - 1)
    prev = lambda i: jnp.maximum(i - 1, 0)
    row = lambda i, j: (cur(i), 0)
    wt = lambda i, j: (j, 0, 0)
    return pl.pallas_call(
        functools.partial(_merge_kernel, ntiles=ntiles, tn=tn),
        grid=(ntiles + 1, nj),
        in_specs=[
            pl.BlockSpec((tm, d), row),
            pl.BlockSpec((tm, tn), lambda i, j: (prev(i), j)),
            pl.BlockSpec((tm, WIDTH_A), row),
            pl.BlockSpec((tm, WIDTH_B), lambda i, j: (cur(i) % tpb, cur(i) // tpb)),
            pl.BlockSpec((tm, WIDTH_C), row),
            pl.BlockSpec((3, 1, d, tn), lambda i, j: (0, j, 0, 0)),
            pl.BlockSpec((1, 3, tn), wt),
            pl.BlockSpec((1, WIDTH_A, tn), wt),
            pl.BlockSpec((1, WIDTH_B, tn), wt),
            pl.BlockSpec((1, WIDTH_C, tn), wt),
            pl.BlockSpec((1, d, tn), wt),
            pl.BlockSpec((1, 1, tn), lambda i, j: (prev(i) // tpb, 0, j)),
        ],
        out_specs=pl.BlockSpec((tm, tn), lambda i, j: (prev(i), jnp.where(i == 0, 0, j))),
        out_shape=jax.ShapeDtypeStruct((n, d), F32),
        scratch_shapes=[pltpu.VMEM((2, tm, d), BF16)],
        compiler_params=_cparams(("arbitrary", "arbitrary")),
        name="merge",
    )(h, x2, o_a, o_b_tm, o_c, _col_tiles(wg, tn), _col_tiles(bg, tn), _col_tiles(p_a, tn),
      _col_tiles(p_b, tn), _col_tiles(p_c, tn), _col_tiles(w_out, tn), gt)


FFN_HALO = 16


def _ffn_kernel(x_ref, xh_ref, mod_ref, g_ref, wa_ref, wb_ref, cw_ref, cb_ref, wd_ref, o_ref,
                h_ref, acc_ref, *, tm, tpb):
    i = pl.program_id(0)
    j = pl.program_id(1)
    nj = pl.num_programs(1)

    @pl.when(j == 0)
    def _():
        g, sc, sh = g_ref[...], mod_ref[0, 4:5, :], mod_ref[0, 3:4, :]
        halo = _norm_mod(xh_ref[...], g, sc, sh)
        h_ref[0:FFN_HALO, :] = jnp.where(i % tpb == 0, 0.0, halo).astype(BF16)
        h_ref[FFN_HALO:, :] = _norm_mod(x_ref[...], g, sc, sh).astype(BF16)
        acc_ref[...] = jnp.zeros(acc_ref.shape, F32)

    a = jnp.dot(h_ref[...], wa_ref[0], preferred_element_type=F32)
    b = jnp.dot(h_ref[FFN_HALO:, :], wb_ref[0], preferred_element_type=F32)
    a_conv = cb_ref[...] + a[FFN_HALO - 2:FFN_HALO - 2 + tm] * cw_ref[0:1, :]
    a_conv = a_conv + a[FFN_HALO - 1:FFN_HALO - 1 + tm] * cw_ref[1:2, :]
    a_conv = a_conv + a[FFN_HALO:] * cw_ref[2:3, :]
    act = (a_conv * jax.nn.sigmoid(a_conv)) * b
    acc_ref[...] += jnp.dot(act.astype(BF16), wd_ref[...], preferred_element_type=F32)

    @pl.when(j == nj - 1)
    def _():
        o_ref[...] = x_ref[...] + mod_ref[0, 5:6, :] * acc_ref[...]


def _ffn(x2, mod, g2, w_a, w_b, conv_w, conv_b, w_down, seq, tm, tn):
    n, d = x2.shape
    dff = w_a.shape[1]
    tpb = seq // tm
    hb = tm // FFN_HALO
    return pl.pallas_call(
        functools.partial(_ffn_kernel, tm=tm, tpb=tpb),
        grid=(n // tm, dff // tn),
        in_specs=[
            pl.BlockSpec((tm, d), lambda i, j: (i, 0)),
            pl.BlockSpec((FFN_HALO, d), lambda i, j: (jnp.maximum(i * hb - 1, 0), 0)),
            pl.BlockSpec((1, 6, d), lambda i, j: (i // tpb, 0, 0)),
            pl.BlockSpec((1, d), lambda i, j: (0, 0)),
            pl.BlockSpec((1, d, tn), lambda i, j: (j, 0, 0)),
            pl.BlockSpec((1, d, tn), lambda i, j: (j, 0, 0)),
            pl.BlockSpec((CONV_WIDTH, tn), lambda i, j: (0, j)),
            pl.BlockSpec((1, tn), lambda i, j: (0, j)),
            pl.BlockSpec((tn, d), lambda i, j: (j, 0)),
        ],
        out_specs=pl.BlockSpec((tm, d), lambda i, j: (i, 0)),
        out_shape=jax.ShapeDtypeStruct((n, d), F32),
        scratch_shapes=[pltpu.VMEM((tm + FFN_HALO, d), BF16), pltpu.VMEM((tm, d), F32)],
        compiler_params=_cparams(("arbitrary", "arbitrary")),
        name="ffn",
    )(x2, x2, mod, g2, _col_tiles(w_a, tn), _col_tiles(w_b, tn), conv_w, conv_b, w_down)


def _rope_tables(positions, rot_dim, period):
    half = rot_dim // 2
    inv_freq = ROPE_THETA ** (-jnp.arange(half, dtype=F32) * (2.0 / rot_dim))
    ang = positions.astype(F32)[..., None] * inv_freq
    cos, sin = jnp.cos(ang), jnp.sin(ang)
    rest = period - 2 * half
    ones = jnp.ones(cos.shape[:-1] + (rest,), F32)
    zh = jnp.zeros_like(sin)
    zr = jnp.zeros_like(ones)
    reps = LANES // period
    out = []
    for parts in ((cos, cos, ones), (-sin, zh, zr), (zh, sin, zr)):
        t = jnp.concatenate(parts, axis=-1)
        out.append(jnp.tile(t, (1, 1, reps)).reshape(-1, LANES))
    return out


def _pad_cols(w, new):
    return jnp.pad(w, ((0, 0), (0, new - w.shape[1])))


def kernel(x, c, positions, w_ada, b_ada, g_norm1, g_norm2, w_in, g_q, g_k, a_re, a_im, b_re, b_im,
           c_re, c_im, d_skip, log_dt, w_glu, w_pool, pool_scale, p_a, p_b, p_c, w_gate, b_gate,
           w_out, w_up, conv_w, conv_b, w_down, *, tm=512, tm_merge=1024, tn_merge=256, tn_ffn=512, tl=128):
    batch, seq, d = x.shape
    depth = w_ada.shape[0]
    n = batch * seq
    tm = min(tm, seq)
    tm_merge = min(tm_merge, seq)
    tl = min(tl, seq)
    topk = min(TOPK_MAX, seq // 4)
    dff_pad = -(-D_FF // tn_ffn) * tn_ffn

    mod_all = _ada(c, w_ada, b_ada).reshape(depth, batch, 6, d)
    tabs_q = _rope_tables(positions, ROT_DIM, HEAD_DIM)
    tabs_i = _rope_tables(positions, IDX_ROT_DIM, IDX_DIM)

    x2 = x.reshape(n, d)
    for l in range(depth):
        mod = mod_all[l]
        gt1 = mod[:, 2:3, :]
        w_in_pad = jnp.concatenate(
            [w_in[l][:, :COL_KW + KW_USED],
             jnp.zeros((d, LANES - KW_USED), F32),
             w_in[l][:, COL_KW + KW_USED:]], axis=1).astype(BF16)
        h, q, k, v, qi, ki2, kw, u_tm, p = _proj(
            x2, mod, g_norm1[l][None], w_in_pad, g_q[l][None], g_k[l][None], tabs_q, tabs_i,
            batch, seq, tm)
        o_a = _dsa(q, k, v, qi, ki2, kw, batch, seq, topk)
        bm, cm, lre, lim = _s5_params(a_re[l], a_im[l], b_re[l], b_im[l], c_re[l], c_im[l], log_dt[l])
        o_b = _s5(u_tm.reshape(seq * batch, WIDTH_B), bm, cm, lre, lim, d_skip[l][None],
                  w_glu[l].astype(BF16), batch, seq, tl)
        o_c = _pool(p, w_pool[l].astype(BF16), pool_scale[l][None], batch, seq)
        x2 = _merge(x2, h, gt1, o_a, o_b.reshape(seq, batch * WIDTH_B), o_c,
                    w_gate[l].astype(BF16), b_gate[l], p_a[l].astype(BF16), p_b[l].astype(BF16),
                    p_c[l].astype(BF16), w_out[l].astype(BF16), seq, tm_merge, tn_merge)
        w_a = _pad_cols(w_up[l][:, :D_FF], dff_pad).astype(BF16)
        w_b = _pad_cols(w_up[l][:, D_FF:], dff_pad).astype(BF16)
        w_d = jnp.pad(w_down[l], ((0, dff_pad - D_FF), (0, 0))).astype(BF16)
        x2 = _ffn(x2, mod, g_norm2[l][None], w_a, w_b, _pad_cols(conv_w[l], dff_pad),
                  _pad_cols(conv_b[l][None], dff_pad), w_d, seq, tm, tn_ffn)
    return x2.reshape(batch, seq, d)
```

```python
import functools
import math

import jax
import jax.numpy as jnp
from jax import lax
from jax.experimental import pallas as pl
from jax.experimental.pallas import tpu as pltpu

F32 = jnp.float32
BF16 = jnp.bfloat16
I32 = jnp.int32

D_MODEL = 2048
DEPTH = 2
CHUNK = 64
EPS = 1e-6
NEG_INF = -1e30
ROPE_THETA = 500000.0

N_HEADS_A = 8
HEAD_DIM = 128
ROT_DIM = HEAD_DIM // 4
N_IDX_HEADS = 8
IDX_DIM = 64
IDX_ROT_DIM = IDX_DIM // 4
LANES = 128
N_IDX_PAIRS = N_IDX_HEADS * IDX_DIM // LANES
TOPK_MAX = 256
Q_BLOCK = 128
WIDTH_A = N_HEADS_A * HEAD_DIM

WIDTH_B = D_MODEL // 4
SSM_GROUP = 16
N_SSM_GROUPS = WIDTH_B // SSM_GROUP
SSM_STATE = 64
SSM_BUNDLE = 8
N_SSM_BUNDLES = N_SSM_GROUPS // SSM_BUNDLE
BUNDLE_CH = SSM_BUNDLE * SSM_GROUP
BUNDLE_ST = SSM_BUNDLE * SSM_STATE

WIDTH_C = D_MODEL // 4
POOL_WINDOWS = (2, 4, 8, 16)
POOL_GROUP = WIDTH_C // 4

D_FF = 5504
CONV_WIDTH = 3

LOGIT_SCALE = HEAD_DIM ** -0.5 * math.log2(math.e)
INT_MIN = -(2 ** 31)

COL_K = WIDTH_A
COL_V = COL_K + HEAD_DIM
COL_QI = COL_V + HEAD_DIM
COL_KW = COL_QI + N_IDX_HEADS * IDX_DIM
COL_U = COL_KW + LANES
COL_P = COL_U + WIDTH_B
D_IN_PAD = COL_P + WIDTH_C
KW_USED = IDX_DIM + N_IDX_HEADS

VMEM_LIMIT = 56 * 1024 * 1024

ROW_TILE = 512
MERGE_ROW_TILE = 1024
MERGE_COL_TILE = 256
FFN_COL_TILE = 512
ADA_COL_TILE = 1024
S5_TIME_BLOCK = 128


def _cparams(sem):
    return pltpu.CompilerParams(dimension_semantics=sem, vmem_limit_bytes=VMEM_LIMIT)


def _norm_mod(x, g, sc, sh):
    ms = jnp.mean(x * x, axis=-1, keepdims=True)
    y = x * lax.rsqrt(ms + EPS)
    return (y * g) * (1.0 + sc) + sh


def _rope(x, c, sa, sb, half):
    return x * c + pltpu.roll(x, LANES - half, 1) * sa + pltpu.roll(x, half, 1) * sb


def _ada_kernel(c_ref, w_ref, b_ref, o_ref):
    c = c_ref[...]
    ca = c * jax.nn.sigmoid(c)
    o_ref[0] = jnp.dot(ca.astype(BF16), w_ref[0].astype(BF16), preferred_element_type=F32) + b_ref[0]


def _ada(c, w_ada, b_ada):
    depth, d, n = w_ada.shape
    tn = ADA_COL_TILE
    b = c.shape[0]
    return pl.pallas_call(
        _ada_kernel,
        grid=(depth, n // tn),
        in_specs=[
            pl.BlockSpec((b, d), lambda l, j: (0, 0)),
            pl.BlockSpec((1, d, tn), lambda l, j: (l, 0, j)),
            pl.BlockSpec((1, 1, tn), lambda l, j: (l, 0, j)),
        ],
        out_specs=pl.BlockSpec((1, b, tn), lambda l, j: (l, 0, j)),
        out_shape=jax.ShapeDtypeStruct((depth, b, n), F32),
        compiler_params=_cparams(("arbitrary", "arbitrary")),
        name="ada",
    )(c, w_ada, b_ada.reshape(depth, 1, n))


def _proj_kernel(x_ref, mod_ref, g_ref, w_ref, gq_ref, gk_ref, cq_ref, saq_ref, sbq_ref,
                 ci_ref, sai_ref, sbi_ref,
                 h_ref, q_ref, k_ref, v_ref, qi_ref, ki2_ref, kw_ref, u_ref, p_ref):
    h = _norm_mod(x_ref[...], g_ref[...], mod_ref[0, 1:2, :], mod_ref[0, 0:1, :]).astype(BF16)
    h_ref[...] = h

    def mm(c0, width):
        return jnp.dot(h, w_ref[:, c0:c0 + width], preferred_element_type=F32)

    def qk_norm_rope(xh, g):
        ms = jnp.mean(xh * xh, axis=-1, keepdims=True)
        y = xh * lax.rsqrt(ms + EPS) * g
        return _rope(y, cq_ref[...], saq_ref[...], sbq_ref[...], ROT_DIM // 2)

    def store_stacked(ref, slab, idx, nslab):
        for qbl in range(slab.shape[0] // Q_BLOCK):
            r0 = (qbl * nslab + idx) * Q_BLOCK
            ref[r0:r0 + Q_BLOCK, :] = slab[qbl * Q_BLOCK:(qbl + 1) * Q_BLOCK]

    q_all = mm(0, WIDTH_A)
    for hd in range(N_HEADS_A):
        qh = (qk_norm_rope(q_all[:, hd * HEAD_DIM:(hd + 1) * HEAD_DIM], gq_ref[...]) * LOGIT_SCALE).astype(BF16)
        store_stacked(q_ref, qh, hd, N_HEADS_A)
    kv = mm(COL_K, 2 * HEAD_DIM)
    k_ref[...] = qk_norm_rope(kv[:, 0:HEAD_DIM], gk_ref[...]).astype(BF16)
    v_ref[...] = kv[:, HEAD_DIM:2 * HEAD_DIM].astype(BF16)
    qi_all = mm(COL_QI, N_IDX_PAIRS * LANES)
    for s in range(N_IDX_PAIRS):
        qi = qi_all[:, s * LANES:(s + 1) * LANES]
        qi = _rope(qi, ci_ref[...], sai_ref[...], sbi_ref[...], IDX_ROT_DIM // 2).astype(BF16)
        store_stacked(qi_ref, qi, s, N_IDX_PAIRS)
    rest = mm(COL_KW, LANES + WIDTH_B + WIDTH_C)
    kw = rest[:, 0:LANES]
    ki = _rope(kw, ci_ref[...], sai_ref[...], sbi_ref[...], IDX_ROT_DIM // 2)
    lane = lax.broadcasted_iota(I32, ki.shape, 1)
    ki_lo = jnp.where(lane < IDX_DIM, ki, 0.0)
    ki2_ref[:, 0:LANES] = ki_lo.astype(BF16)
    ki2_ref[:, LANES:2 * LANES] = pltpu.roll(ki_lo, IDX_DIM, 1).astype(BF16)
    kw_ref[...] = kw * (N_IDX_HEADS ** -0.5 * IDX_DIM ** -0.5)
    u_ref[...] = rest[:, LANES:LANES + WIDTH_B]
    p_ref[...] = rest[:, LANES + WIDTH_B:]


def _proj(x2, mod, g1, w_in_pad, g_q, g_k, tabs_q, tabs_i, batch, seq, tm):
    n, d = x2.shape
    tpb = seq // tm
    row = lambda i: (i, 0)
    const = lambda i: (0, 0)
    tab_spec = pl.BlockSpec((tm, LANES), row)
    return pl.pallas_call(
        _proj_kernel,
        grid=(n // tm,),
        in_specs=[
            pl.BlockSpec((tm, d), row),
            pl.BlockSpec((1, 6, d), lambda i: (i // tpb, 0, 0)),
            pl.BlockSpec((1, d), const),
            pl.BlockSpec((d, D_IN_PAD), const),
            pl.BlockSpec((1, HEAD_DIM), const),
            pl.BlockSpec((1, HEAD_DIM), const),
            tab_spec, tab_spec, tab_spec, tab_spec, tab_spec, tab_spec,
        ],
        out_specs=[
            pl.BlockSpec((tm, d), row),
            pl.BlockSpec((tm * N_HEADS_A, HEAD_DIM), row),
            pl.BlockSpec((tm, HEAD_DIM), row),
            pl.BlockSpec((tm, HEAD_DIM), row),
            pl.BlockSpec((tm * N_IDX_PAIRS, LANES), row),
            pl.BlockSpec((tm, 2 * LANES), row),
            pl.BlockSpec((tm, LANES), row),
            pl.BlockSpec((tm, WIDTH_B), lambda i: (i % tpb, i // tpb)),
            pl.BlockSpec((tm, WIDTH_C), row),
        ],
        out_shape=[
            jax.ShapeDtypeStruct((n, d), BF16),
            jax.ShapeDtypeStruct((n * N_HEADS_A, HEAD_DIM), BF16),
            jax.ShapeDtypeStruct((n, HEAD_DIM), BF16),
            jax.ShapeDtypeStruct((n, HEAD_DIM), BF16),
            jax.ShapeDtypeStruct((n * N_IDX_PAIRS, LANES), BF16),
            jax.ShapeDtypeStruct((n, 2 * LANES), BF16),
            jax.ShapeDtypeStruct((n, LANES), F32),
            jax.ShapeDtypeStruct((seq, batch * WIDTH_B), F32),
            jax.ShapeDtypeStruct((n, WIDTH_C), F32),
        ],
        compiler_params=_cparams(("arbitrary",)),
        name="proj",
    )(x2, mod, g1, w_in_pad, g_q, g_k, *tabs_q, *tabs_i)


KEY_SLABS = 4
KEY_BLOCK = KEY_SLABS * LANES


def _dsa_scores(qi_ref, ki2_ref, kw_ref, s_ref, nqb):
    row = lax.broadcasted_iota(I32, (Q_BLOCK, KEY_BLOCK), 0)
    lane = lax.broadcasted_iota(I32, (Q_BLOCK, KEY_BLOCK), 1)
    chunk_end = (row // CHUNK + 1) * CHUNK
    trans_b = (((1,), (1,)), ((), ()))

    def qb_body(qq, carry):
        r0 = pl.multiple_of(qq * Q_BLOCK, Q_BLOCK)
        w = kw_ref[pl.ds(r0, Q_BLOCK), :]
        wb = [jnp.broadcast_to(w[:, IDX_DIM + hd:IDX_DIM + hd + 1], (Q_BLOCK, LANES))
              for hd in range(N_IDX_HEADS)]
        limit_row = r0 + chunk_end

        def key_body(j, c2):
            k0 = pl.multiple_of(j * KEY_BLOCK, KEY_BLOCK)
            kia = ki2_ref[pl.ds(k0, KEY_BLOCK), 0:LANES]
            kib = ki2_ref[pl.ds(k0, KEY_BLOCK), LANES:2 * LANES]
            qp = qi_ref[pl.ds(pl.multiple_of(r0 * N_IDX_PAIRS, Q_BLOCK), Q_BLOCK * N_IDX_PAIRS), :]
            sa = lax.dot_general(qp, kia, trans_b, preferred_element_type=F32)
            sb = lax.dot_general(qp, kib, trans_b, preferred_element_type=F32)
            slabs = []
            for sl in range(KEY_SLABS):
                cols = slice(sl * LANES, (sl + 1) * LANES)
                acc = jnp.zeros((Q_BLOCK, LANES), F32)
                for pr in range(N_IDX_PAIRS):
                    rows = slice(pr * Q_BLOCK, (pr + 1) * Q_BLOCK)
                    acc = (acc + jnp.maximum(sa[rows, cols], 0.0) * wb[2 * pr]
                           + jnp.maximum(sb[rows, cols], 0.0) * wb[2 * pr + 1])
                slabs.append(acc)
            sc = jnp.concatenate(slabs, axis=1)
            adm = ((k0 + lane) < limit_row) & (sc > NEG_INF * 0.5)
            bits = pltpu.bitcast(sc, I32)
            bits = jnp.where(bits == INT_MIN, 0, bits)
            skey = bits ^ ((bits >> 31) & 0x7FFFFFFF)
            skey = jnp.where(adm, skey, INT_MIN)
            for sl in range(KEY_SLABS):
                s_ref[j * KEY_SLABS + sl, pl.ds(r0, Q_BLOCK), :] = skey[:, sl * LANES:(sl + 1) * LANES]
            return c2

        nkb = (r0 + Q_BLOCK + KEY_BLOCK - 1) // KEY_BLOCK
        lax.fori_loop(0, nkb, key_body, 0)
        return carry

    lax.fori_loop(0, nqb, qb_body, 0)


def _dsa_threshold(s_ref, thr_ref, cnt_ref, nqb, topk):
    thr_ref[...] = jnp.full(thr_ref.shape, INT_MIN, I32)

    ones = jnp.ones((LANES, LANES), BF16)

    def count_ge(offset):
        for qq in range(nqb):
            rows = slice(qq * Q_BLOCK, (qq + 1) * Q_BLOCK)
            cand = thr_ref[rows, :] + offset
            acc = jnp.zeros((Q_BLOCK, LANES), F32)
            for s in range(qq + 1):
                acc = acc + jnp.where(s_ref[s, rows, :] >= cand, 1.0, 0.0)
            cnt_ref[rows, :] = acc.astype(BF16)
        return jnp.dot(cnt_ref[...], ones, preferred_element_type=F32)

    def bit_body(i, carry):
        bitval = jnp.left_shift(jnp.int32(1), 31 - i)
        cnt = count_ge(bitval)
        thr = thr_ref[...]
        thr_ref[...] = jnp.where(cnt >= float(topk), thr + bitval, thr)
        return carry

    lax.fori_loop(0, 32, bit_body, 0)

    surplus = jnp.where((count_ge(0) > float(topk)) & (thr_ref[...] > INT_MIN), 1.0, 0.0).astype(BF16)
    seq = thr_ref.shape[0]
    block_of_row = lax.broadcasted_iota(I32, (nqb, seq), 1) // Q_BLOCK
    in_block = jnp.where(block_of_row == lax.broadcasted_iota(I32, (nqb, seq), 0), 1.0, 0.0).astype(BF16)
    surplus_rows = jnp.dot(in_block, surplus, preferred_element_type=F32)
    li = lax.broadcasted_iota(I32, (LANES, LANES), 0)
    lj = lax.broadcasted_iota(I32, (LANES, LANES), 1)
    upper = jnp.where(li <= lj, 1.0, 0.0).astype(BF16)

    for qq in range(nqb):
        rows = slice(qq * Q_BLOCK, (qq + 1) * Q_BLOCK)

        @pl.when(surplus_rows[qq, 0] > 0.0)
        def _():
            thr = thr_ref[rows, :]

            n_gt = jnp.zeros((Q_BLOCK, LANES), F32)
            for s in range(qq + 1):
                n_gt = n_gt + jnp.where(s_ref[s, rows, :] > thr, 1.0, 0.0)
            keep = float(topk) - jnp.dot(n_gt.astype(BF16), ones, preferred_element_type=F32)
            seen = jnp.zeros((Q_BLOCK, LANES), F32)
            for s in range(qq + 1):
                sk = s_ref[s, rows, :]
                tie = sk == thr
                tie_b = jnp.where(tie, 1.0, 0.0).astype(BF16)
                rank = seen + jnp.dot(tie_b, upper, preferred_element_type=F32)
                drop = tie & (rank > keep) & (thr > INT_MIN)
                s_ref[s, rows, :] = jnp.where(drop, thr - 1, sk)
                seen = seen + jnp.dot(tie_b, ones, preferred_element_type=F32)


def _dsa_kernel(q_ref, k_ref, v_ref, qi_ref, ki2_ref, kw_ref, o_ref,
                s_ref, thr_ref, cnt_ref, bias_ref, lg_ref, p_ref, acc_ref, vx_ref, *m_refs,
                topk, nqb):
    qb = pl.program_id(1)

    @pl.when(qb == 0)
    def _():
        _dsa_scores(qi_ref, ki2_ref, kw_ref, s_ref, nqb)
        _dsa_threshold(s_ref, thr_ref, cnt_ref, nqb, topk)
        vx_ref[:, 0:HEAD_DIM] = v_ref[...]
        vx_ref[:, HEAD_DIM:] = jnp.ones((vx_ref.shape[0], LANES), BF16)

    r0 = pl.multiple_of(qb * Q_BLOCK, Q_BLOCK)
    nkb = (r0 + Q_BLOCK + KEY_BLOCK - 1) // KEY_BLOCK
    trans_b = (((1,), (1,)), ((), ()))
    thr = thr_ref[pl.ds(r0, Q_BLOCK), :]

    def bias_body(j, carry):
        for sl in range(KEY_SLABS):
            sk = s_ref[j * KEY_SLABS + sl, pl.ds(r0, Q_BLOCK), :]
            sel = (sk >= thr) & (sk > INT_MIN)
            bias_ref[j * KEY_SLABS + sl] = jnp.where(sel, 0.0, NEG_INF)
        return carry

    lax.fori_loop(0, nkb, bias_body, 0)

    for hd in range(N_HEADS_A):
        m_refs[hd][...] = jnp.full((Q_BLOCK, LANES), NEG_INF, F32)
    acc_ref[...] = jnp.zeros(acc_ref.shape, F32)

    def max_body(j, carry):
        k0 = pl.multiple_of(j * KEY_BLOCK, KEY_BLOCK)
        s = lax.dot_general(q_ref[...], k_ref[pl.ds(k0, KEY_BLOCK), :], trans_b, preferred_element_type=F32)
        for hd in range(N_HEADS_A):
            rows = slice(hd * Q_BLOCK, (hd + 1) * Q_BLOCK)
            m = m_refs[hd][...]
            for sl in range(KEY_SLABS):
                sh = s[rows, sl * LANES:(sl + 1) * LANES] + bias_ref[j * KEY_SLABS + sl]
                lg_ref[j * KEY_SLABS + sl, rows, :] = sh
                m = jnp.maximum(m, sh)
            m_refs[hd][...] = m
        return carry

    lax.fori_loop(0, nkb, max_body, 0)
    for hd in range(N_HEADS_A):
        m_refs[hd][...] = jnp.broadcast_to(jnp.max(m_refs[hd][...], axis=1, keepdims=True), (Q_BLOCK, LANES))

    def sum_body(j, carry):
        k0 = pl.multiple_of(j * KEY_BLOCK, KEY_BLOCK)
        for hd in range(N_HEADS_A):
            rows = slice(hd * Q_BLOCK, (hd + 1) * Q_BLOCK)
            m = m_refs[hd][...]
            for sl in range(KEY_SLABS):
                p_s = jnp.exp2(lg_ref[j * KEY_SLABS + sl, rows, :] - m)
                p_ref[rows, sl * LANES:(sl + 1) * LANES] = p_s.astype(BF16)
        acc_ref[...] += jnp.dot(p_ref[...], vx_ref[pl.ds(k0, KEY_BLOCK), :], preferred_element_type=F32)
        return carry

    lax.fori_loop(0, nkb, sum_body, 0)
    for hd in range(N_HEADS_A):
        rows = slice(hd * Q_BLOCK, (hd + 1) * Q_BLOCK)
        o_ref[:, hd * HEAD_DIM:(hd + 1) * HEAD_DIM] = (
            acc_ref[rows, 0:HEAD_DIM] / acc_ref[rows, HEAD_DIM:]).astype(BF16)


def _dsa(q, k, v, qi, ki2, kw, batch, seq, topk):
    assert seq % KEY_BLOCK == 0
    n = k.shape[0]
    nqb = seq // Q_BLOCK
    qrow = lambda b, i: (b * nqb + i, 0)
    brow = lambda b, i: (b, 0)
    nsl = seq // LANES
    head_scratch = [pltpu.VMEM((Q_BLOCK, LANES), F32)] * N_HEADS_A
    return pl.pallas_call(
        functools.partial(_dsa_kernel, topk=topk, nqb=nqb),
        grid=(batch, nqb),
        in_specs=[
            pl.BlockSpec((Q_BLOCK * N_HEADS_A, HEAD_DIM), qrow),
            pl.BlockSpec((seq, HEAD_DIM), brow),
            pl.BlockSpec((seq, HEAD_DIM), brow),
            pl.BlockSpec((seq * N_IDX_PAIRS, LANES), brow),
            pl.BlockSpec((seq, 2 * LANES), brow),
            pl.BlockSpec((seq, LANES), brow),
        ],
        out_specs=pl.BlockSpec((Q_BLOCK, WIDTH_A), qrow),
        out_shape=jax.ShapeDtypeStruct((n, WIDTH_A), BF16),
        scratch_shapes=[
            pltpu.VMEM((nsl, seq, LANES), I32),
            pltpu.VMEM((seq, LANES), I32),
            pltpu.VMEM((seq, LANES), BF16),
            pltpu.VMEM((nsl, Q_BLOCK, LANES), F32),
            pltpu.VMEM((nsl, Q_BLOCK * N_HEADS_A, LANES), F32),
            pltpu.VMEM((Q_BLOCK * N_HEADS_A, KEY_BLOCK), BF16),
            pltpu.VMEM((Q_BLOCK * N_HEADS_A, HEAD_DIM + LANES), F32),
            pltpu.VMEM((seq, HEAD_DIM + LANES), BF16),
        ] + head_scratch,
        compiler_params=_cparams(("arbitrary", "arbitrary")),
        name="dsa",
    )(q, k, v, qi, ki2, kw)


def _s5_kernel(u_ref, bm_ref, cm_ref, lre_ref, lim_ref, dsk_ref, wglu_ref, o_ref,
               bu_ref, st_ref, y_ref, *, batch, tl):
    @pl.when(pl.program_id(0) == 0)
    def _():
        st_ref[...] = jnp.zeros(st_ref.shape, F32)

    for gb in range(N_SSM_BUNDLES):
        ch = slice(gb * BUNDLE_CH, (gb + 1) * BUNDLE_CH)
        u_g = u_ref[:, ch]
        bu_ref[...] = jnp.dot(u_g.astype(BF16), bm_ref[gb], preferred_element_type=F32)
        lam_re = jnp.broadcast_to(lre_ref[gb], (batch, BUNDLE_ST))
        lam_im = jnp.broadcast_to(lim_ref[gb], (batch, BUNDLE_ST))

        def step(t, carry):
            x_re, x_im = carry
            r0 = pl.multiple_of(t * batch, batch)
            n_re = lam_re * x_re - lam_im * x_im + bu_ref[pl.ds(r0, batch), 0:BUNDLE_ST]
            n_im = lam_re * x_im + lam_im * x_re + bu_ref[pl.ds(r0, batch), BUNDLE_ST:2 * BUNDLE_ST]
            bu_ref[pl.ds(r0, batch), 0:BUNDLE_ST] = n_re
            bu_ref[pl.ds(r0, batch), BUNDLE_ST:2 * BUNDLE_ST] = n_im
            return n_re, n_im

        x_re, x_im = lax.fori_loop(
            0, tl, step, (st_ref[gb, :, 0:BUNDLE_ST], st_ref[gb, :, BUNDLE_ST:2 * BUNDLE_ST]), unroll=4)
        st_ref[gb, :, 0:BUNDLE_ST] = x_re
        st_ref[gb, :, BUNDLE_ST:2 * BUNDLE_ST] = x_im
        y = jnp.dot(bu_ref[...].astype(BF16), cm_ref[gb], preferred_element_type=F32)
        y_ref[:, ch] = y + dsk_ref[:, ch] * u_g

    y = jax.nn.gelu(y_ref[...])
    gl = jnp.dot(y.astype(BF16), wglu_ref[...], preferred_element_type=F32)
    o_ref[...] = (y * jax.nn.sigmoid(gl)).astype(BF16)


def _s5_params(a_re, a_im, b_re, b_im, c_re, c_im, log_dt):
    dt = jnp.exp(log_dt)[:, None]
    mag = jnp.exp(a_re * dt)
    lb_re = mag * jnp.cos(a_im * dt)
    lb_im = mag * jnp.sin(a_im * dt)
    nr, ni = lb_re - 1.0, lb_im
    den = a_re * a_re + a_im * a_im
    q_re = (nr * a_re + ni * a_im) / den
    q_im = (ni * a_re - nr * a_im) / den
    bb_re = q_re[..., None] * b_re - q_im[..., None] * b_im
    bb_im = q_re[..., None] * b_im + q_im[..., None] * b_re
    eye = jnp.eye(SSM_BUNDLE, dtype=F32)
    nb = N_SSM_BUNDLES

    def pack_b(m):
        m = m.reshape(nb, SSM_BUNDLE, SSM_STATE, SSM_GROUP)
        return jnp.einsum('bgpi,gh->bgihp', m, eye).reshape(nb, BUNDLE_CH, BUNDLE_ST)

    def pack_c(m):
        m = m.reshape(nb, SSM_BUNDLE, SSM_GROUP, SSM_STATE)
        return jnp.einsum('bgop,gh->bgpho', m, eye).reshape(nb, BUNDLE_ST, BUNDLE_CH)

    bm = jnp.concatenate([pack_b(bb_re), pack_b(bb_im)], axis=2).astype(BF16)
    cm = jnp.concatenate([pack_c(c_re), -pack_c(c_im)], axis=1).astype(BF16)
    return bm, cm, lb_re.reshape(nb, 1, BUNDLE_ST), lb_im.reshape(nb, 1, BUNDLE_ST)


def _s5(u_tm, bm, cm, lre, lim, d_skip, w_glu, batch, seq, tl):
    rows = tl * batch
    const3 = lambda t: (0, 0, 0)
    const2 = lambda t: (0, 0)
    return pl.pallas_call(
        functools.partial(_s5_kernel, batch=batch, tl=tl),
        grid=(seq // tl,),
        in_specs=[
            pl.BlockSpec((rows, WIDTH_B), lambda t: (t, 0)),
            pl.BlockSpec(bm.shape, const3),
            pl.BlockSpec(cm.shape, const3),
            pl.BlockSpec(lre.shape, const3),
            pl.BlockSpec(lim.shape, const3),
            pl.BlockSpec((1, WIDTH_B), const2),
            pl.BlockSpec((WIDTH_B, WIDTH_B), const2),
        ],
        out_specs=pl.BlockSpec((rows, WIDTH_B), lambda t: (t, 0)),
        out_shape=jax.ShapeDtypeStruct((seq * batch, WIDTH_B), BF16),
        scratch_shapes=[
            pltpu.VMEM((rows, 2 * BUNDLE_ST), F32),
            pltpu.VMEM((N_SSM_BUNDLES, batch, 2 * BUNDLE_ST), F32),
            pltpu.VMEM((rows, WIDTH_B), F32),
        ],
        compiler_params=_cparams(("arbitrary",)),
        name="s5",
    )(u_tm, bm, cm, lre, lim, d_skip, w_glu)


def _pool_kernel(p_ref, w_ref, sc_ref, o_ref):
    seq = p_ref.shape[0]
    t = lax.broadcasted_iota(I32, (seq, POOL_GROUP), 0)
    t1 = (t + 1).astype(F32)
    for g, win in enumerate(POOL_WINDOWS):
        ch = slice(g * POOL_GROUP, (g + 1) * POOL_GROUP)
        x = p_ref[:, ch]
        s = x
        sh = 1
        while sh < win:
            s = s + jnp.where(t >= sh, pltpu.roll(s, sh, 0), 0.0)
            sh *= 2
        pooled = s / jnp.minimum(t1, float(win)) - x
        y = jnp.dot(pooled.astype(BF16), w_ref[g], preferred_element_type=F32)
        o_ref[:, ch] = (y * sc_ref[:, ch]).astype(BF16)


def _pool(p, w_pool, pool_scale, batch, seq):
    return pl.pallas_call(
        _pool_kernel,
        grid=(batch,),
        in_specs=[
            pl.BlockSpec((seq, WIDTH_C), lambda b: (b, 0)),
            pl.BlockSpec(w_pool.shape, lambda b: (0, 0, 0)),
            pl.BlockSpec((1, WIDTH_C), lambda b: (0, 0)),
        ],
        out_specs=pl.BlockSpec((seq, WIDTH_C), lambda b: (b, 0)),
        out_shape=jax.ShapeDtypeStruct(p.shape, BF16),
        compiler_params=_cparams(("arbitrary",)),
        name="pool",
    )(p, w_pool, pool_scale)


def _col_tiles(w, tn):
    *lead, k, n = w.shape
    nl = len(lead)
    return w.reshape(*lead, k, n // tn, tn).transpose(*range(nl), nl + 1, nl, nl + 2)


def _merge_kernel(h_ref, xc_ref, oa_ref, ob_ref, oc_ref, wg_ref, bg_ref,
                  pa_ref, pb_ref, pc_ref, wo_ref, gt_ref, o_ref, mg_ref, *, ntiles, tn):
    i = pl.program_id(0)
    j = pl.program_id(1)
    slot = i % 2

    @pl.when(i < ntiles)
    def _():
        h = h_ref[...]
        branches = ((oa_ref, pa_ref), (ob_ref, pb_ref), (oc_ref, pc_ref))
        merged = None
        for b, (o_r, p_r) in enumerate(branches):
            gate = jax.nn.sigmoid(jnp.dot(h, wg_ref[b, 0], preferred_element_type=F32) + bg_ref[0, b:b + 1, :])
            term = gate * jnp.dot(o_r[...], p_r[0], preferred_element_type=F32)
            merged = term if merged is None else merged + term
        c0 = pl.multiple_of(j * tn, tn)
        mg_ref[slot, :, pl.ds(c0, tn)] = merged.astype(BF16)

    @pl.when(i >= 1)
    def _():
        mix = jnp.dot(mg_ref[1 - slot], wo_ref[0], preferred_element_type=F32)
        o_ref[...] = xc_ref[...] + gt_ref[0] * mix


def _merge(x2, h, gt, o_a, o_b_tm, o_c, wg, bg, p_a, p_b, p_c, w_out, seq, tm, tn):
    n, d = x2.shape
    nj = d // tn
    ntiles = n // tm
    tpb = seq // tm
    cur = lambda i: jnp.minimum(i, ntiles - 1)
    prev = lambda i: jnp.maximum(i - 1, 0)
    row = lambda i, j: (cur(i), 0)
    wt = lambda i, j: (j, 0, 0)
    return pl.pallas_call(
        functools.partial(_merge_kernel, ntiles=ntiles, tn=tn),
        grid=(ntiles + 1, nj),
        in_specs=[
            pl.BlockSpec((tm, d), row),
            pl.BlockSpec((tm, tn), lambda i, j: (prev(i), j)),
            pl.BlockSpec((tm, WIDTH_A), row),
            pl.BlockSpec((tm, WIDTH_B), lambda i, j: (cur(i) % tpb, cur(i) // tpb)),
            pl.BlockSpec((tm, WIDTH_C), row),
            pl.BlockSpec((3, 1, d, tn), lambda i, j: (0, j, 0, 0)),
            pl.BlockSpec((1, 3, tn), wt),
            pl.BlockSpec((1, WIDTH_A, tn), wt),
            pl.BlockSpec((1, WIDTH_B, tn), wt),
            pl.BlockSpec((1, WIDTH_C, tn), wt),
            pl.BlockSpec((1, d, tn), wt),
            pl.BlockSpec((1, 1, tn), lambda i, j: (prev(i) // tpb, 0, j)),
        ],
        out_specs=pl.BlockSpec((tm, tn), lambda i, j: (prev(i), jnp.where(i == 0, 0, j))),
        out_shape=jax.ShapeDtypeStruct((n, d), F32),
        scratch_shapes=[pltpu.VMEM((2, tm, d), BF16)],
        compiler_params=_cparams(("arbitrary", "arbitrary")),
        name="merge",
    )(h, x2, o_a, o_b_tm, o_c, _col_tiles(wg, tn), _col_tiles(bg, tn), _col_tiles(p_a, tn),
      _col_tiles(p_b, tn), _col_tiles(p_c, tn), _col_tiles(w_out, tn), gt)


FFN_HALO = 16


def _ffn_kernel(x_ref, xh_ref, mod_ref, g_ref, wa_ref, wb_ref, cw_ref, cb_ref, wd_ref, o_ref,
                h_ref, acc_ref, *, tm, tpb):
    i = pl.program_id(0)
    j = pl.program_id(1)
    nj = pl.num_programs(1)

    @pl.when(j == 0)
    def _():
        g, sc, sh = g_ref[...], mod_ref[0, 4:5, :], mod_ref[0, 3:4, :]
        halo = _norm_mod(xh_ref[...], g, sc, sh)
        h_ref[0:FFN_HALO, :] = jnp.where(i % tpb == 0, 0.0, halo).astype(BF16)
        h_ref[FFN_HALO:, :] = _norm_mod(x_ref[...], g, sc, sh).astype(BF16)
        acc_ref[...] = jnp.zeros(acc_ref.shape, F32)

    a = jnp.dot(h_ref[...], wa_ref[0], preferred_element_type=F32)
    b = jnp.dot(h_ref[FFN_HALO:, :], wb_ref[0], preferred_element_type=F32)
    a_conv = cb_ref[...] + a[FFN_HALO - 2:FFN_HALO - 2 + tm] * cw_ref[0:1, :]
    a_conv = a_conv + a[FFN_HALO - 1:FFN_HALO - 1 + tm] * cw_ref[1:2, :]
    a_conv = a_conv + a[FFN_HALO:] * cw_ref[2:3, :]
    act = (a_conv * jax.nn.sigmoid(a_conv)) * b
    acc_ref[...] += jnp.dot(act.astype(BF16), wd_ref[...], preferred_element_type=F32)

    @pl.when(j == nj - 1)
    def _():
        o_ref[...] = x_ref[...] + mod_ref[0, 5:6, :] * acc_ref[...]


def _ffn(x2, mod, g2, w_a, w_b, conv_w, conv_b, w_down, seq, tm, tn):
    n, d = x2.shape
    dff = w_a.shape[1]
    tpb = seq // tm
    hb = tm // FFN_HALO
    return pl.pallas_call(
        functools.partial(_ffn_kernel, tm=tm, tpb=tpb),
        grid=(n // tm, dff // tn),
        in_specs=[
            pl.BlockSpec((tm, d), lambda i, j: (i, 0)),
            pl.BlockSpec((FFN_HALO, d), lambda i, j: (jnp.maximum(i * hb - 1, 0), 0)),
            pl.BlockSpec((1, 6, d), lambda i, j: (i // tpb, 0, 0)),
            pl.BlockSpec((1, d), lambda i, j: (0, 0)),
            pl.BlockSpec((1, d, tn), lambda i, j: (j, 0, 0)),
            pl.BlockSpec((1, d, tn), lambda i, j: (j, 0, 0)),
            pl.BlockSpec((CONV_WIDTH, tn), lambda i, j: (0, j)),
            pl.BlockSpec((1, tn), lambda i, j: (0, j)),
            pl.BlockSpec((tn, d), lambda i, j: (j, 0)),
        ],
        out_specs=pl.BlockSpec((tm, d), lambda i, j: (i, 0)),
        out_shape=jax.ShapeDtypeStruct((n, d), F32),
        scratch_shapes=[pltpu.VMEM((tm + FFN_HALO, d), BF16), pltpu.VMEM((tm, d), F32)],
        compiler_params=_cparams(("arbitrary", "arbitrary")),
        name="ffn",
    )(x2, x2, mod, g2, _col_tiles(w_a, tn), _col_tiles(w_b, tn), conv_w, conv_b, w_down)


def _rope_tables(positions, rot_dim, period):
    half = rot_dim // 2
    inv_freq = ROPE_THETA ** (-jnp.arange(half, dtype=F32) * (2.0 / rot_dim))
    ang = positions.astype(F32)[..., None] * inv_freq
    cos, sin = jnp.cos(ang), jnp.sin(ang)
    rest = period - 2 * half
    ones = jnp.ones(cos.shape[:-1] + (rest,), F32)
    zh = jnp.zeros_like(sin)
    zr = jnp.zeros_like(ones)
    reps = LANES // period
    out = []
    for parts in ((cos, cos, ones), (-sin, zh, zr), (zh, sin, zr)):
        t = jnp.concatenate(parts, axis=-1)
        out.append(jnp.tile(t, (1, 1, reps)).reshape(-1, LANES))
    return out


def _pad_cols(w, new):
    return jnp.pad(w, ((0, 0), (0, new - w.shape[1])))


def kernel(x, c, positions, w_ada, b_ada, g_norm1, g_norm2, w_in, g_q, g_k, a_re, a_im, b_re, b_im,
           c_re, c_im, d_skip, log_dt, w_glu, w_pool, pool_scale, p_a, p_b, p_c, w_gate, b_gate,
           w_out, w_up, conv_w, conv_b, w_down):
    batch, seq, d = x.shape
    depth = w_ada.shape[0]
    n = batch * seq
    tm, tm_merge, tl = min(ROW_TILE, seq), min(MERGE_ROW_TILE, seq), min(S5_TIME_BLOCK, seq)
    tn_merge, tn_ffn = MERGE_COL_TILE, FFN_COL_TILE
    topk = min(TOPK_MAX, seq // 4)
    dff_pad = -(-D_FF // tn_ffn) * tn_ffn

    mod_all = _ada(c, w_ada, b_ada).reshape(depth, batch, 6, d)
    tabs_q = _rope_tables(positions, ROT_DIM, HEAD_DIM)
    tabs_i = _rope_tables(positions, IDX_ROT_DIM, IDX_DIM)

    x2 = x.reshape(n, d)
    for l in range(depth):
        mod = mod_all[l]
        gt1 = mod[:, 2:3, :]
        w_in_pad = jnp.concatenate(
            [w_in[l][:, :COL_KW + KW_USED],
             jnp.zeros((d, LANES - KW_USED), F32),
             w_in[l][:, COL_KW + KW_USED:]], axis=1).astype(BF16)
        h, q, k, v, qi, ki2, kw, u_tm, p = _proj(
            x2, mod, g_norm1[l][None], w_in_pad, g_q[l][None], g_k[l][None], tabs_q, tabs_i,
            batch, seq, tm)
        o_a = _dsa(q, k, v, qi, ki2, kw, batch, seq, topk)
        bm, cm, lre, lim = _s5_params(a_re[l], a_im[l], b_re[l], b_im[l], c_re[l], c_im[l], log_dt[l])
        o_b = _s5(u_tm.reshape(seq * batch, WIDTH_B), bm, cm, lre, lim, d_skip[l][None],
                  w_glu[l].astype(BF16), batch, seq, tl)
        o_c = _pool(p, w_pool[l].astype(BF16), pool_scale[l][None], batch, seq)
        x2 = _merge(x2, h, gt1, o_a, o_b.reshape(seq, batch * WIDTH_B), o_c,
                    w_gate[l].astype(BF16), b_gate[l], p_a[l].astype(BF16), p_b[l].astype(BF16),
                    p_c[l].astype(BF16), w_out[l].astype(BF16), seq, tm_merge, tn_merge)
        w_a = _pad_cols(w_up[l][:, :D_FF], dff_pad).astype(BF16)
        w_b = _pad_cols(w_up[l][:, D_FF:], dff_pad).astype(BF16)
        w_d = jnp.pad(w_down[l], ((0, dff_pad - D_FF), (0, 0))).astype(BF16)
        x2 = _ffn(x2, mod, g_norm2[l][None], w_a, w_b, _pad_cols(conv_w[l], dff_pad),
                  _pad_cols(conv_b[l][None], dff_pad), w_d, seq, tm, tn_ffn)
    return x2.reshape(batch, seq, d)
```

```python
import functools
import math

import jax
import jax.numpy as jnp
from jax import lax
from jax.experimental import pallas as pl
from jax.experimental.pallas import tpu as pltpu

F32 = jnp.float32
BF16 = jnp.bfloat16
I32 = jnp.int32

D_MODEL = 2048
DEPTH = 2
CHUNK = 64
EPS = 1e-6
NEG_INF = -1e30
ROPE_THETA = 500000.0

N_HEADS_A = 8
HEAD_DIM = 128
ROT_DIM = HEAD_DIM // 4
N_IDX_HEADS = 8
IDX_DIM = 64
IDX_ROT_DIM = IDX_DIM // 4
LANES = 128
N_IDX_PAIRS = N_IDX_HEADS * IDX_DIM // LANES
TOPK_MAX = 256
Q_BLOCK = 128
WIDTH_A = N_HEADS_A * HEAD_DIM

WIDTH_B = D_MODEL // 4
SSM_GROUP = 16
N_SSM_GROUPS = WIDTH_B // SSM_GROUP
SSM_STATE = 64
SSM_BUNDLE = 8
N_SSM_BUNDLES = N_SSM_GROUPS // SSM_BUNDLE
BUNDLE_CH = SSM_BUNDLE * SSM_GROUP
BUNDLE_ST = SSM_BUNDLE * SSM_STATE

WIDTH_C = D_MODEL // 4
POOL_WINDOWS = (2, 4, 8, 16)
POOL_GROUP = WIDTH_C // 4

D_FF = 5504
CONV_WIDTH = 3

LOGIT_SCALE = HEAD_DIM ** -0.5 * math.log2(math.e)
INT_MIN = -(2 ** 31)

COL_K = WIDTH_A
COL_V = COL_K + HEAD_DIM
COL_QI = COL_V + HEAD_DIM
COL_KW = COL_QI + N_IDX_HEADS * IDX_DIM
COL_U = COL_KW + LANES
COL_P = COL_U + WIDTH_B
D_IN_PAD = COL_P + WIDTH_C
KW_USED = IDX_DIM + N_IDX_HEADS

VMEM_LIMIT = 56 * 1024 * 1024

ROW_TILE = 512
MERGE_ROW_TILE = 1024
MERGE_COL_TILE = 256
FFN_COL_TILE = 512
ADA_COL_TILE = 1024
S5_TIME_BLOCK = 128


def _cparams(sem):
    return pltpu.CompilerParams(dimension_semantics=sem, vmem_limit_bytes=VMEM_LIMIT)


def _norm_mod(x, g, sc, sh):
    ms = jnp.mean(x * x, axis=-1, keepdims=True)
    y = x * lax.rsqrt(ms + EPS)
    return (y * g) * (1.0 + sc) + sh


def _rope(x, c, sa, sb, half):
    return x * c + pltpu.roll(x, LANES - half, 1) * sa + pltpu.roll(x, half, 1) * sb


def _ada_kernel(c_ref, w_ref, b_ref, o_ref):
    c = c_ref[...]
    ca = c * jax.nn.sigmoid(c)
    o_ref[0] = jnp.dot(ca.astype(BF16), w_ref[0].astype(BF16), preferred_element_type=F32) + b_ref[0]


def _ada(c, w_ada, b_ada):
    depth, d, n = w_ada.shape
    tn = ADA_COL_TILE
    b = c.shape[0]
    return pl.pallas_call(
        _ada_kernel,
        grid=(depth, n // tn),
        in_specs=[
            pl.BlockSpec((b, d), lambda l, j: (0, 0)),
            pl.BlockSpec((1, d, tn), lambda l, j: (l, 0, j)),
            pl.BlockSpec((1, 1, tn), lambda l, j: (l, 0, j)),
        ],
        out_specs=pl.BlockSpec((1, b, tn), lambda l, j: (l, 0, j)),
        out_shape=jax.ShapeDtypeStruct((depth, b, n), F32),
        compiler_params=_cparams(("arbitrary", "arbitrary")),
        name="ada",
    )(c, w_ada, b_ada.reshape(depth, 1, n))


def _proj_kernel(x_ref, mod_ref, g_ref, w_ref, gq_ref, gk_ref, cq_ref, saq_ref, sbq_ref,
                 ci_ref, sai_ref, sbi_ref,
                 h_ref, q_ref, k_ref, v_ref, qi_ref, ki2_ref, kw_ref, u_ref, p_ref):
    h = _norm_mod(x_ref[...], g_ref[...], mod_ref[0, 1:2, :], mod_ref[0, 0:1, :]).astype(BF16)
    h_ref[...] = h

    def mm(c0, width):
        return jnp.dot(h, w_ref[:, c0:c0 + width], preferred_element_type=F32)

    def qk_norm_rope(xh, g):
        ms = jnp.mean(xh * xh, axis=-1, keepdims=True)
        y = xh * lax.rsqrt(ms + EPS) * g
        return _rope(y, cq_ref[...], saq_ref[...], sbq_ref[...], ROT_DIM // 2)

    def store_stacked(ref, slab, idx, nslab):
        for qbl in range(slab.shape[0] // Q_BLOCK):
            r0 = (qbl * nslab + idx) * Q_BLOCK
            ref[r0:r0 + Q_BLOCK, :] = slab[qbl * Q_BLOCK:(qbl + 1) * Q_BLOCK]

    q_all = mm(0, WIDTH_A)
    for hd in range(N_HEADS_A):
        qh = (qk_norm_rope(q_all[:, hd * HEAD_DIM:(hd + 1) * HEAD_DIM], gq_ref[...]) * LOGIT_SCALE).astype(BF16)
        store_stacked(q_ref, qh, hd, N_HEADS_A)
    kv = mm(COL_K, 2 * HEAD_DIM)
    k_ref[...] = qk_norm_rope(kv[:, 0:HEAD_DIM], gk_ref[...]).T.astype(BF16)
    v_ref[...] = kv[:, HEAD_DIM:2 * HEAD_DIM].astype(BF16)
    qi_all = mm(COL_QI, N_IDX_PAIRS * LANES)
    for s in range(N_IDX_PAIRS):
        qi = qi_all[:, s * LANES:(s + 1) * LANES]
        qi = _rope(qi, ci_ref[...], sai_ref[...], sbi_ref[...], IDX_ROT_DIM // 2).astype(BF16)
        store_stacked(qi_ref, qi, s, N_IDX_PAIRS)
    rest = mm(COL_KW, LANES + WIDTH_B + WIDTH_C)
    kw = rest[:, 0:LANES]
    ki = _rope(kw, ci_ref[...], sai_ref[...], sbi_ref[...], IDX_ROT_DIM // 2)
    lane = lax.broadcasted_iota(I32, ki.shape, 1)
    ki_lo = jnp.where(lane < IDX_DIM, ki, 0.0)
    ki2_ref[0:LANES, :] = ki_lo.T.astype(BF16)
    ki2_ref[LANES:2 * LANES, :] = pltpu.roll(ki_lo, IDX_DIM, 1).T.astype(BF16)
    kw_ref[...] = kw * (N_IDX_HEADS ** -0.5 * IDX_DIM ** -0.5)
    u_ref[...] = rest[:, LANES:LANES + WIDTH_B]
    p_ref[...] = rest[:, LANES + WIDTH_B:]


def _proj(x2, mod, g1, w_in_pad, g_q, g_k, tabs_q, tabs_i, batch, seq, tm):
    n, d = x2.shape
    tpb = seq // tm
    row = lambda i: (i, 0)
    const = lambda i: (0, 0)
    tab_spec = pl.BlockSpec((tm, LANES), row)
    return pl.pallas_call(
        _proj_kernel,
        grid=(n // tm,),
        in_specs=[
            pl.BlockSpec((tm, d), row),
            pl.BlockSpec((1, 6, d), lambda i: (i // tpb, 0, 0)),
            pl.BlockSpec((1, d), const),
            pl.BlockSpec((d, D_IN_PAD), const),
            pl.BlockSpec((1, HEAD_DIM), const),
            pl.BlockSpec((1, HEAD_DIM), const),
            tab_spec, tab_spec, tab_spec, tab_spec, tab_spec, tab_spec,
        ],
        out_specs=[
            pl.BlockSpec((tm, d), row),
            pl.BlockSpec((tm * N_HEADS_A, HEAD_DIM), row),
            pl.BlockSpec((HEAD_DIM, tm), lambda i: (i // tpb, i % tpb)),
            pl.BlockSpec((tm, HEAD_DIM), row),
            pl.BlockSpec((tm * N_IDX_PAIRS, LANES), row),
            pl.BlockSpec((2 * LANES, tm), lambda i: (i // tpb, i % tpb)),
            pl.BlockSpec((tm, LANES), row),
            pl.BlockSpec((tm, WIDTH_B), lambda i: (i % tpb, i // tpb)),
            pl.BlockSpec((tm, WIDTH_C), row),
        ],
        out_shape=[
            jax.ShapeDtypeStruct((n, d), BF16),
            jax.ShapeDtypeStruct((n * N_HEADS_A, HEAD_DIM), BF16),
            jax.ShapeDtypeStruct((batch * HEAD_DIM, seq), BF16),
            jax.ShapeDtypeStruct((n, HEAD_DIM), BF16),
            jax.ShapeDtypeStruct((n * N_IDX_PAIRS, LANES), BF16),
            jax.ShapeDtypeStruct((batch * 2 * LANES, seq), BF16),
            jax.ShapeDtypeStruct((n, LANES), F32),
            jax.ShapeDtypeStruct((seq, batch * WIDTH_B), F32),
            jax.ShapeDtypeStruct((n, WIDTH_C), F32),
        ],
        compiler_params=_cparams(("arbitrary",)),
        name="proj",
    )(x2, mod, g1, w_in_pad, g_q, g_k, *tabs_q, *tabs_i)


KEY_SLABS = 4
KEY_BLOCK = KEY_SLABS * LANES


def _dsa_scores(qi_ref, ki2_ref, kw_ref, s_ref, nqb):
    row = lax.broadcasted_iota(I32, (Q_BLOCK, KEY_BLOCK), 0)
    lane = lax.broadcasted_iota(I32, (Q_BLOCK, KEY_BLOCK), 1)
    chunk_end = (row // CHUNK + 1) * CHUNK
    trans_b = (((1,), (1,)), ((), ()))

    def qb_body(qq, carry):
        r0 = pl.multiple_of(qq * Q_BLOCK, Q_BLOCK)
        w = kw_ref[pl.ds(r0, Q_BLOCK), :]
        wb = [jnp.broadcast_to(w[:, IDX_DIM + hd:IDX_DIM + hd + 1], (Q_BLOCK, LANES))
              for hd in range(N_IDX_HEADS)]
        limit_row = r0 + chunk_end

        def key_body(j, c2):
            k0 = pl.multiple_of(j * KEY_BLOCK, KEY_BLOCK)
            kia = ki2_ref[0:LANES, pl.ds(k0, KEY_BLOCK)]
            kib = ki2_ref[LANES:2 * LANES, pl.ds(k0, KEY_BLOCK)]
            qp = qi_ref[pl.ds(pl.multiple_of(r0 * N_IDX_PAIRS, Q_BLOCK), Q_BLOCK * N_IDX_PAIRS), :]
            sa = jnp.dot(qp, kia, preferred_element_type=F32)
            sb = jnp.dot(qp, kib, preferred_element_type=F32)
            slabs = []
            for sl in range(KEY_SLABS):
                cols = slice(sl * LANES, (sl + 1) * LANES)
                acc = jnp.zeros((Q_BLOCK, LANES), F32)
                for pr in range(N_IDX_PAIRS):
                    rows = slice(pr * Q_BLOCK, (pr + 1) * Q_BLOCK)
                    acc = (acc + jnp.maximum(sa[rows, cols], 0.0) * wb[2 * pr]
                           + jnp.maximum(sb[rows, cols], 0.0) * wb[2 * pr + 1])
                slabs.append(acc)
            sc = jnp.concatenate(slabs, axis=1)
            adm = ((k0 + lane) < limit_row) & (sc > NEG_INF * 0.5)
            bits = pltpu.bitcast(sc, I32)
            bits = jnp.where(bits == INT_MIN, 0, bits)
            skey = bits ^ ((bits >> 31) & 0x7FFFFFFF)
            skey = jnp.where(adm, skey, INT_MIN)
            for sl in range(KEY_SLABS):
                s_ref[j * KEY_SLABS + sl, pl.ds(r0, Q_BLOCK), :] = skey[:, sl * LANES:(sl + 1) * LANES]
            return c2

        nkb = (r0 + Q_BLOCK + KEY_BLOCK - 1) // KEY_BLOCK
        lax.fori_loop(0, nkb, key_body, 0)
        return carry

    lax.fori_loop(0, nqb, qb_body, 0)


def _dsa_threshold(s_ref, thr_ref, cnt_ref, nqb, topk):
    thr_ref[...] = jnp.full(thr_ref.shape, INT_MIN, I32)

    ones = jnp.ones((LANES, LANES), BF16)

    def count_ge(offset):
        for qq in range(nqb):
            rows = slice(qq * Q_BLOCK, (qq + 1) * Q_BLOCK)
            cand = thr_ref[rows, :] + offset
            acc = jnp.zeros((Q_BLOCK, LANES), F32)
            for s in range(qq + 1):
                acc = acc + jnp.where(s_ref[s, rows, :] >= cand, 1.0, 0.0)
            cnt_ref[rows, :] = acc.astype(BF16)
        return jnp.dot(cnt_ref[...], ones, preferred_element_type=F32)

    def bit_body(i, carry):
        bitval = jnp.left_shift(jnp.int32(1), 31 - i)
        cnt = count_ge(bitval)
        thr = thr_ref[...]
        thr_ref[...] = jnp.where(cnt >= float(topk), thr + bitval, thr)
        return carry

    lax.fori_loop(0, 32, bit_body, 0)

    surplus = jnp.where((count_ge(0) > float(topk)) & (thr_ref[...] > INT_MIN), 1.0, 0.0).astype(BF16)
    seq = thr_ref.shape[0]
    block_of_row = lax.broadcasted_iota(I32, (nqb, seq), 1) // Q_BLOCK
    in_block = jnp.where(block_of_row == lax.broadcasted_iota(I32, (nqb, seq), 0), 1.0, 0.0).astype(BF16)
    surplus_rows = jnp.dot(in_block, surplus, preferred_element_type=F32)
    li = lax.broadcasted_iota(I32, (LANES, LANES), 0)
    lj = lax.broadcasted_iota(I32, (LANES, LANES), 1)
    upper = jnp.where(li <= lj, 1.0, 0.0).astype(BF16)

    for qq in range(nqb):
        rows = slice(qq * Q_BLOCK, (qq + 1) * Q_BLOCK)

        @pl.when(surplus_rows[qq, 0] > 0.0)
        def _():
            thr = thr_ref[rows, :]

            n_gt = jnp.zeros((Q_BLOCK, LANES), F32)
            for s in range(qq + 1):
                n_gt = n_gt + jnp.where(s_ref[s, rows, :] > thr, 1.0, 0.0)
            keep = float(topk) - jnp.dot(n_gt.astype(BF16), ones, preferred_element_type=F32)
            seen = jnp.zeros((Q_BLOCK, LANES), F32)
            for s in range(qq + 1):
                sk = s_ref[s, rows, :]
                tie = sk == thr
                tie_b = jnp.where(tie, 1.0, 0.0).astype(BF16)
                rank = seen + jnp.dot(tie_b, upper, preferred_element_type=F32)
                drop = tie & (rank > keep) & (thr > INT_MIN)
                s_ref[s, rows, :] = jnp.where(drop, thr - 1, sk)
                seen = seen + jnp.dot(tie_b, ones, preferred_element_type=F32)


def _dsa_kernel(q_ref, k_ref, v_ref, qi_ref, ki2_ref, kw_ref, o_ref,
                s_ref, thr_ref, cnt_ref, bias_ref, lg_ref, p_ref, acc_ref, vx_ref, *m_refs,
                topk, nqb):
    qb = pl.program_id(1)

    @pl.when(qb == 0)
    def _():
        _dsa_scores(qi_ref, ki2_ref, kw_ref, s_ref, nqb)
        _dsa_threshold(s_ref, thr_ref, cnt_ref, nqb, topk)
        vx_ref[:, 0:HEAD_DIM] = v_ref[...]
        vx_ref[:, HEAD_DIM:] = jnp.ones((vx_ref.shape[0], LANES), BF16)

    r0 = pl.multiple_of(qb * Q_BLOCK, Q_BLOCK)
    nkb = (r0 + Q_BLOCK + KEY_BLOCK - 1) // KEY_BLOCK
    trans_b = (((1,), (1,)), ((), ()))
    thr = thr_ref[pl.ds(r0, Q_BLOCK), :]

    def bias_body(j, carry):
        for sl in range(KEY_SLABS):
            sk = s_ref[j * KEY_SLABS + sl, pl.ds(r0, Q_BLOCK), :]
            sel = (sk >= thr) & (sk > INT_MIN)
            bias_ref[j * KEY_SLABS + sl] = jnp.where(sel, 0.0, NEG_INF)
        return carry

    lax.fori_loop(0, nkb, bias_body, 0)

    for hd in range(N_HEADS_A):
        m_refs[hd][...] = jnp.full((Q_BLOCK, LANES), NEG_INF, F32)
    acc_ref[...] = jnp.zeros(acc_ref.shape, F32)

    def max_body(j, carry):
        k0 = pl.multiple_of(j * KEY_BLOCK, KEY_BLOCK)
        s = jnp.dot(q_ref[...], k_ref[:, pl.ds(k0, KEY_BLOCK)], preferred_element_type=F32)
        for hd in range(N_HEADS_A):
            rows = slice(hd * Q_BLOCK, (hd + 1) * Q_BLOCK)
            m = m_refs[hd][...]
            for sl in range(KEY_SLABS):
                sh = s[rows, sl * LANES:(sl + 1) * LANES] + bias_ref[j * KEY_SLABS + sl]
                lg_ref[j * KEY_SLABS + sl, rows, :] = sh
                m = jnp.maximum(m, sh)
            m_refs[hd][...] = m
        return carry

    lax.fori_loop(0, nkb, max_body, 0)
    for hd in range(N_HEADS_A):
        m_refs[hd][...] = jnp.broadcast_to(jnp.max(m_refs[hd][...], axis=1, keepdims=True), (Q_BLOCK, LANES))

    def sum_body(j, carry):
        k0 = pl.multiple_of(j * KEY_BLOCK, KEY_BLOCK)
        for hd in range(N_HEADS_A):
            rows = slice(hd * Q_BLOCK, (hd + 1) * Q_BLOCK)
            m = m_refs[hd][...]
            for sl in range(KEY_SLABS):
                p_s = jnp.exp2(lg_ref[j * KEY_SLABS + sl, rows, :] - m)
                p_ref[rows, sl * LANES:(sl + 1) * LANES] = p_s.astype(BF16)
        acc_ref[...] += jnp.dot(p_ref[...], vx_ref[pl.ds(k0, KEY_BLOCK), :], preferred_element_type=F32)
        return carry

    lax.fori_loop(0, nkb, sum_body, 0)
    for hd in range(N_HEADS_A):
        rows = slice(hd * Q_BLOCK, (hd + 1) * Q_BLOCK)
        o_ref[:, hd * HEAD_DIM:(hd + 1) * HEAD_DIM] = (
            acc_ref[rows, 0:HEAD_DIM] / acc_ref[rows, HEAD_DIM:]).astype(BF16)


def _dsa(q, k, v, qi, ki2, kw, batch, seq, topk):
    assert seq % KEY_BLOCK == 0
    n = v.shape[0]
    nqb = seq // Q_BLOCK
    qrow = lambda b, i: (b * nqb + i, 0)
    brow = lambda b, i: (b, 0)
    nsl = seq // LANES
    head_scratch = [pltpu.VMEM((Q_BLOCK, LANES), F32)] * N_HEADS_A
    return pl.pallas_call(
        functools.partial(_dsa_kernel, topk=topk, nqb=nqb),
        grid=(batch, nqb),
        in_specs=[
            pl.BlockSpec((Q_BLOCK * N_HEADS_A, HEAD_DIM), qrow),
            pl.BlockSpec((HEAD_DIM, seq), brow),
            pl.BlockSpec((seq, HEAD_DIM), brow),
            pl.BlockSpec((seq * N_IDX_PAIRS, LANES), brow),
            pl.BlockSpec((2 * LANES, seq), brow),
            pl.BlockSpec((seq, LANES), brow),
        ],
        out_specs=pl.BlockSpec((Q_BLOCK, WIDTH_A), qrow),
        out_shape=jax.ShapeDtypeStruct((n, WIDTH_A), BF16),
        scratch_shapes=[
            pltpu.VMEM((nsl, seq, LANES), I32),
            pltpu.VMEM((seq, LANES), I32),
            pltpu.VMEM((seq, LANES), BF16),
            pltpu.VMEM((nsl, Q_BLOCK, LANES), F32),
            pltpu.VMEM((nsl, Q_BLOCK * N_HEADS_A, LANES), F32),
            pltpu.VMEM((Q_BLOCK * N_HEADS_A, KEY_BLOCK), BF16),
            pltpu.VMEM((Q_BLOCK * N_HEADS_A, HEAD_DIM + LANES), F32),
            pltpu.VMEM((seq, HEAD_DIM + LANES), BF16),
        ] + head_scratch,
        compiler_params=_cparams(("arbitrary", "arbitrary")),
        name="dsa",
    )(q, k, v, qi, ki2, kw)


def _s5_kernel(u_ref, bm_ref, cm_ref, lre_ref, lim_ref, dsk_ref, wglu_ref, o_ref,
               bu_ref, st_ref, y_ref, *, batch, tl):
    @pl.when(pl.program_id(0) == 0)
    def _():
        st_ref[...] = jnp.zeros(st_ref.shape, F32)

    for gb in range(N_SSM_BUNDLES):
        ch = slice(gb * BUNDLE_CH, (gb + 1) * BUNDLE_CH)
        u_g = u_ref[:, ch]
        bu_ref[...] = jnp.dot(u_g.astype(BF16), bm_ref[gb], preferred_element_type=F32)
        lam_re = jnp.broadcast_to(lre_ref[gb], (batch, BUNDLE_ST))
        lam_im = jnp.broadcast_to(lim_ref[gb], (batch, BUNDLE_ST))

        def step(t, carry):
            x_re, x_im = carry
            r0 = pl.multiple_of(t * batch, batch)
            n_re = lam_re * x_re - lam_im * x_im + bu_ref[pl.ds(r0, batch), 0:BUNDLE_ST]
            n_im = lam_re * x_im + lam_im * x_re + bu_ref[pl.ds(r0, batch), BUNDLE_ST:2 * BUNDLE_ST]
            bu_ref[pl.ds(r0, batch), 0:BUNDLE_ST] = n_re
            bu_ref[pl.ds(r0, batch), BUNDLE_ST:2 * BUNDLE_ST] = n_im
            return n_re, n_im

        x_re, x_im = lax.fori_loop(
            0, tl, step, (st_ref[gb, :, 0:BUNDLE_ST], st_ref[gb, :, BUNDLE_ST:2 * BUNDLE_ST]), unroll=4)
        st_ref[gb, :, 0:BUNDLE_ST] = x_re
        st_ref[gb, :, BUNDLE_ST:2 * BUNDLE_ST] = x_im
        y = jnp.dot(bu_ref[...].astype(BF16), cm_ref[gb], preferred_element_type=F32)
        y_ref[:, ch] = y + dsk_ref[:, ch] * u_g

    y = jax.nn.gelu(y_ref[...])
    gl = jnp.dot(y.astype(BF16), wglu_ref[...], preferred_element_type=F32)
    o_ref[...] = (y * jax.nn.sigmoid(gl)).astype(BF16)


def _s5_params(a_re, a_im, b_re, b_im, c_re, c_im, log_dt):
    dt = jnp.exp(log_dt)[:, None]
    mag = jnp.exp(a_re * dt)
    lb_re = mag * jnp.cos(a_im * dt)
    lb_im = mag * jnp.sin(a_im * dt)
    nr, ni = lb_re - 1.0, lb_im
    den = a_re * a_re + a_im * a_im
    q_re = (nr * a_re + ni * a_im) / den
    q_im = (ni * a_re - nr * a_im) / den
    bb_re = q_re[..., None] * b_re - q_im[..., None] * b_im
    bb_im = q_re[..., None] * b_im + q_im[..., None] * b_re
    eye = jnp.eye(SSM_BUNDLE, dtype=F32)
    nb = N_SSM_BUNDLES

    def pack_b(m):
        m = m.reshape(nb, SSM_BUNDLE, SSM_STATE, SSM_GROUP)
        return jnp.einsum('bgpi,gh->bgihp', m, eye).reshape(nb, BUNDLE_CH, BUNDLE_ST)

    def pack_c(m):
        m = m.reshape(nb, SSM_BUNDLE, SSM_GROUP, SSM_STATE)
        return jnp.einsum('bgop,gh->bgpho', m, eye).reshape(nb, BUNDLE_ST, BUNDLE_CH)

    bm = jnp.concatenate([pack_b(bb_re), pack_b(bb_im)], axis=2).astype(BF16)
    cm = jnp.concatenate([pack_c(c_re), -pack_c(c_im)], axis=1).astype(BF16)
    return bm, cm, lb_re.reshape(nb, 1, BUNDLE_ST), lb_im.reshape(nb, 1, BUNDLE_ST)


def _s5(u_tm, bm, cm, lre, lim, d_skip, w_glu, batch, seq, tl):
    rows = tl * batch
    const3 = lambda t: (0, 0, 0)
    const2 = lambda t: (0, 0)
    return pl.pallas_call(
        functools.partial(_s5_kernel, batch=batch, tl=tl),
        grid=(seq // tl,),
        in_specs=[
            pl.BlockSpec((rows, WIDTH_B), lambda t: (t, 0)),
            pl.BlockSpec(bm.shape, const3),
            pl.BlockSpec(cm.shape, const3),
            pl.BlockSpec(lre.shape, const3),
            pl.BlockSpec(lim.shape, const3),
            pl.BlockSpec((1, WIDTH_B), const2),
            pl.BlockSpec((WIDTH_B, WIDTH_B), const2),
        ],
        out_specs=pl.BlockSpec((rows, WIDTH_B), lambda t: (t, 0)),
        out_shape=jax.ShapeDtypeStruct((seq * batch, WIDTH_B), BF16),
        scratch_shapes=[
            pltpu.VMEM((rows, 2 * BUNDLE_ST), F32),
            pltpu.VMEM((N_SSM_BUNDLES, batch, 2 * BUNDLE_ST), F32),
            pltpu.VMEM((rows, WIDTH_B), F32),
        ],
        compiler_params=_cparams(("arbitrary",)),
        name="s5",
    )(u_tm, bm, cm, lre, lim, d_skip, w_glu)


def _pool_kernel(p_ref, w_ref, sc_ref, o_ref):
    seq = p_ref.shape[0]
    t = lax.broadcasted_iota(I32, (seq, POOL_GROUP), 0)
    t1 = (t + 1).astype(F32)
    for g, win in enumerate(POOL_WINDOWS):
        ch = slice(g * POOL_GROUP, (g + 1) * POOL_GROUP)
        x = p_ref[:, ch]
        s = x
        sh = 1
        while sh < win:
            s = s + jnp.where(t >= sh, pltpu.roll(s, sh, 0), 0.0)
            sh *= 2
        pooled = s / jnp.minimum(t1, float(win)) - x
        y = jnp.dot(pooled.astype(BF16), w_ref[g], preferred_element_type=F32)
        o_ref[:, ch] = (y * sc_ref[:, ch]).astype(BF16)


def _pool(p, w_pool, pool_scale, batch, seq):
    return pl.pallas_call(
        _pool_kernel,
        grid=(batch,),
        in_specs=[
            pl.BlockSpec((seq, WIDTH_C), lambda b: (b, 0)),
            pl.BlockSpec(w_pool.shape, lambda b: (0, 0, 0)),
            pl.BlockSpec((1, WIDTH_C), lambda b: (0, 0)),
        ],
        out_specs=pl.BlockSpec((seq, WIDTH_C), lambda b: (b, 0)),
        out_shape=jax.ShapeDtypeStruct(p.shape, BF16),
        compiler_params=_cparams(("arbitrary",)),
        name="pool",
    )(p, w_pool, pool_scale)


def _col_tiles(w, tn):
    *lead, k, n = w.shape
    nl = len(lead)
    return w.reshape(*lead, k, n // tn, tn).transpose(*range(nl), nl + 1, nl, nl + 2)


def _merge_kernel(h_ref, xc_ref, oa_ref, ob_ref, oc_ref, wg_ref, bg_ref,
                  pa_ref, pb_ref, pc_ref, wo_ref, gt_ref, o_ref, mg_ref, *, ntiles, tn):
    i = pl.program_id(0)
    j = pl.program_id(1)
    slot = i % 2

    @pl.when(i < ntiles)
    def _():
        h = h_ref[...]
        branches = ((oa_ref, pa_ref), (ob_ref, pb_ref), (oc_ref, pc_ref))
        merged = None
        for b, (o_r, p_r) in enumerate(branches):
            gate = jax.nn.sigmoid(jnp.dot(h, wg_ref[b, 0], preferred_element_type=F32) + bg_ref[0, b:b + 1, :])
            term = gate * jnp.dot(o_r[...], p_r[0], preferred_element_type=F32)
            merged = term if merged is None else merged + term
        c0 = pl.multiple_of(j * tn, tn)
        mg_ref[slot, :, pl.ds(c0, tn)] = merged.astype(BF16)

    @pl.when(i >= 1)
    def _():
        mix = jnp.dot(mg_ref[1 - slot], wo_ref[0], preferred_element_type=F32)
        o_ref[...] = xc_ref[...] + gt_ref[0] * mix


def _merge(x2, h, gt, o_a, o_b_tm, o_c, wg, bg, p_a, p_b, p_c, w_out, seq, tm, tn):
    n, d = x2.shape
    nj = d // tn
    ntiles = n // tm
    tpb = seq // tm
    cur = lambda i: jnp.minimum(i, ntiles - 1)
    prev = lambda i: jnp.maximum(i - 1, 0)
    row = lambda i, j: (cur(i), 0)
    wt = lambda i, j: (j, 0, 0)
    return pl.pallas_call(
        functools.partial(_merge_kernel, ntiles=ntiles, tn=tn),
        grid=(ntiles + 1, nj),
        in_specs=[
            pl.BlockSpec((tm, d), row),
            pl.BlockSpec((tm, tn), lambda i, j: (prev(i), j)),
            pl.BlockSpec((tm, WIDTH_A), row),
            pl.BlockSpec((tm, WIDTH_B), lambda i, j: (cur(i) % tpb, cur(i) // tpb)),
            pl.BlockSpec((tm, WIDTH_C), row),
            pl.BlockSpec((3, 1, d, tn), lambda i, j: (0, j, 0, 0)),
            pl.BlockSpec((1, 3, tn), wt),
            pl.BlockSpec((1, WIDTH_A, tn), wt),
            pl.BlockSpec((1, WIDTH_B, tn), wt),
            pl.BlockSpec((1, WIDTH_C, tn), wt),
            pl.BlockSpec((1, d, tn), wt),
            pl.BlockSpec((1, 1, tn), lambda i, j: (prev(i) // tpb, 0, j)),
        ],
        out_specs=pl.BlockSpec((tm, tn), lambda i, j: (prev(i), jnp.where(i == 0, 0, j))),
        out_shape=jax.ShapeDtypeStruct((n, d), F32),
        scratch_shapes=[pltpu.VMEM((2, tm, d), BF16)],
        compiler_params=_cparams(("arbitrary", "arbitrary")),
        name="merge",
    )(h, x2, o_a, o_b_tm, o_c, _col_tiles(wg, tn), _col_tiles(bg, tn), _col_tiles(p_a, tn),
      _col_tiles(p_b, tn), _col_tiles(p_c, tn), _col_tiles(w_out, tn), gt)


FFN_HALO = 16


def _ffn_kernel(x_ref, xh_ref, mod_ref, g_ref, wa_ref, wb_ref, cw_ref, cb_ref, wd_ref, o_ref,
                h_ref, acc_ref, *, tm, tpb):
    i = pl.program_id(0)
    j = pl.program_id(1)
    nj = pl.num_programs(1)

    @pl.when(j == 0)
    def _():
        g, sc, sh = g_ref[...], mod_ref[0, 4:5, :], mod_ref[0, 3:4, :]
        halo = _norm_mod(xh_ref[...], g, sc, sh)
        h_ref[0:FFN_HALO, :] = jnp.where(i % tpb == 0, 0.0, halo).astype(BF16)
        h_ref[FFN_HALO:, :] = _norm_mod(x_ref[...], g, sc, sh).astype(BF16)
        acc_ref[...] = jnp.zeros(acc_ref.shape, F32)

    a = jnp.dot(h_ref[...], wa_ref[0], preferred_element_type=F32)
    b = jnp.dot(h_ref[FFN_HALO:, :], wb_ref[0], preferred_element_type=F32)
    a_conv = cb_ref[...] + a[FFN_HALO - 2:FFN_HALO - 2 + tm] * cw_ref[0:1, :]
    a_conv = a_conv + a[FFN_HALO - 1:FFN_HALO - 1 + tm] * cw_ref[1:2, :]
    a_conv = a_conv + a[FFN_HALO:] * cw_ref[2:3, :]
    act = (a_conv * jax.nn.sigmoid(a_conv)) * b
    acc_ref[...] += jnp.dot(act.astype(BF16), wd_ref[...], preferred_element_type=F32)

    @pl.when(j == nj - 1)
    def _():
        o_ref[...] = x_ref[...] + mod_ref[0, 5:6, :] * acc_ref[...]


def _ffn(x2, mod, g2, w_a, w_b, conv_w, conv_b, w_down, seq, tm, tn):
    n, d = x2.shape
    dff = w_a.shape[1]
    tpb = seq // tm
    hb = tm // FFN_HALO
    return pl.pallas_call(
        functools.partial(_ffn_kernel, tm=tm, tpb=tpb),
        grid=(n // tm, dff // tn),
        in_specs=[
            pl.BlockSpec((tm, d), lambda i, j: (i, 0)),
            pl.BlockSpec((FFN_HALO, d), lambda i, j: (jnp.maximum(i * hb - 1, 0), 0)),
            pl.BlockSpec((1, 6, d), lambda i, j: (i // tpb, 0, 0)),
            pl.BlockSpec((1, d), lambda i, j: (0, 0)),
            pl.BlockSpec((1, d, tn), lambda i, j: (j, 0, 0)),
            pl.BlockSpec((1, d, tn), lambda i, j: (j, 0, 0)),
            pl.BlockSpec((CONV_WIDTH, tn), lambda i, j: (0, j)),
            pl.BlockSpec((1, tn), lambda i, j: (0, j)),
            pl.BlockSpec((tn, d), lambda i, j: (j, 0)),
        ],
        out_specs=pl.BlockSpec((tm, d), lambda i, j: (i, 0)),
        out_shape=jax.ShapeDtypeStruct((n, d), F32),
        scratch_shapes=[pltpu.VMEM((tm + FFN_HALO, d), BF16), pltpu.VMEM((tm, d), F32)],
        compiler_params=_cparams(("arbitrary", "arbitrary")),
        name="ffn",
    )(x2, x2, mod, g2, _col_tiles(w_a, tn), _col_tiles(w_b, tn), conv_w, conv_b, w_down)


def _rope_tables(positions, rot_dim, period):
    half = rot_dim // 2
    inv_freq = ROPE_THETA ** (-jnp.arange(half, dtype=F32) * (2.0 / rot_dim))
    ang = positions.astype(F32)[..., None] * inv_freq
    cos, sin = jnp.cos(ang), jnp.sin(ang)
    rest = period - 2 * half
    ones = jnp.ones(cos.shape[:-1] + (rest,), F32)
    zh = jnp.zeros_like(sin)
    zr = jnp.zeros_like(ones)
    reps = LANES // period
    out = []
    for parts in ((cos, cos, ones), (-sin, zh, zr), (zh, sin, zr)):
        t = jnp.concatenate(parts, axis=-1)
        out.append(jnp.tile(t, (1, 1, reps)).reshape(-1, LANES))
    return out


def _pad_cols(w, new):
    return jnp.pad(w, ((0, 0), (0, new - w.shape[1])))


def kernel(x, c, positions, w_ada, b_ada, g_norm1, g_norm2, w_in, g_q, g_k, a_re, a_im, b_re, b_im,
           c_re, c_im, d_skip, log_dt, w_glu, w_pool, pool_scale, p_a, p_b, p_c, w_gate, b_gate,
           w_out, w_up, conv_w, conv_b, w_down):
    batch, seq, d = x.shape
    depth = w_ada.shape[0]
    n = batch * seq
    tm, tm_merge, tl = min(ROW_TILE, seq), min(MERGE_ROW_TILE, seq), min(S5_TIME_BLOCK, seq)
    tn_merge, tn_ffn = MERGE_COL_TILE, FFN_COL_TILE
    topk = min(TOPK_MAX, seq // 4)
    dff_pad = -(-D_FF // tn_ffn) * tn_ffn

    mod_all = _ada(c, w_ada, b_ada).reshape(depth, batch, 6, d)
    tabs_q = _rope_tables(positions, ROT_DIM, HEAD_DIM)
    tabs_i = _rope_tables(positions, IDX_ROT_DIM, IDX_DIM)

    x2 = x.reshape(n, d)
    for l in range(depth):
        mod = mod_all[l]
        gt1 = mod[:, 2:3, :]
        w_in_pad = jnp.concatenate(
            [w_in[l][:, :COL_KW + KW_USED],
             jnp.zeros((d, LANES - KW_USED), F32),
             w_in[l][:, COL_KW + KW_USED:]], axis=1).astype(BF16)
        h, q, k, v, qi, ki2, kw, u_tm, p = _proj(
            x2, mod, g_norm1[l][None], w_in_pad, g_q[l][None], g_k[l][None], tabs_q, tabs_i,
            batch, seq, tm)
        o_a = _dsa(q, k, v, qi, ki2, kw, batch, seq, topk)
        bm, cm, lre, lim = _s5_params(a_re[l], a_im[l], b_re[l], b_im[l], c_re[l], c_im[l], log_dt[l])
        o_b = _s5(u_tm.reshape(seq * batch, WIDTH_B), bm, cm, lre, lim, d_skip[l][None],
                  w_glu[l].astype(BF16), batch, seq, tl)
        o_c = _pool(p, w_pool[l].astype(BF16), pool_scale[l][None], batch, seq)
        x2 = _merge(x2, h, gt1, o_a, o_b.reshape(seq, batch * WIDTH_B), o_c,
                    w_gate[l].astype(BF16), b_gate[l], p_a[l].astype(BF16), p_b[l].astype(BF16),
                    p_c[l].astype(BF16), w_out[l].astype(BF16), seq, tm_merge, tn_merge)
        w_a = _pad_cols(w_up[l][:, :D_FF], dff_pad).astype(BF16)
        w_b = _pad_cols(w_up[l][:, D_FF:], dff_pad).astype(BF16)
        w_d = jnp.pad(w_down[l], ((0, dff_pad - D_FF), (0, 0))).astype(BF16)
        x2 = _ffn(x2, mod, g_norm2[l][None], w_a, w_b, _pad_cols(conv_w[l], dff_pad),
                  _pad_cols(conv_b[l][None], dff_pad), w_d, seq, tm, tn_ffn)
    return x2.reshape(batch, seq, d)
```

```python
import functools
import math

import jax
import jax.numpy as jnp
from jax import lax
from jax.experimental import pallas as pl
from jax.experimental.pallas import tpu as pltpu

F32 = jnp.float32
BF16 = jnp.bfloat16
I32 = jnp.int32

D_MODEL = 2048
DEPTH = 2
CHUNK = 64
EPS = 1e-6
NEG_INF = -1e30
ROPE_THETA = 500000.0

N_HEADS_A = 8
HEAD_DIM = 128
ROT_DIM = HEAD_DIM // 4
N_IDX_HEADS = 8
IDX_DIM = 64
IDX_ROT_DIM = IDX_DIM // 4
LANES = 128
N_IDX_PAIRS = N_IDX_HEADS * IDX_DIM // LANES
TOPK_MAX = 256
Q_BLOCK = 128
WIDTH_A = N_HEADS_A * HEAD_DIM

WIDTH_B = D_MODEL // 4
SSM_GROUP = 16
N_SSM_GROUPS = WIDTH_B // SSM_GROUP
SSM_STATE = 64
SSM_BUNDLE = 8
N_SSM_BUNDLES = N_SSM_GROUPS // SSM_BUNDLE
BUNDLE_CH = SSM_BUNDLE * SSM_GROUP
BUNDLE_ST = SSM_BUNDLE * SSM_STATE

WIDTH_C = D_MODEL // 4
POOL_WINDOWS = (2, 4, 8, 16)
POOL_GROUP = WIDTH_C // 4

D_FF = 5504
CONV_WIDTH = 3

LOGIT_SCALE = HEAD_DIM ** -0.5 * math.log2(math.e)
INT_MIN = -(2 ** 31)

COL_K = WIDTH_A
COL_V = COL_K + HEAD_DIM
COL_QI = COL_V + HEAD_DIM
COL_KW = COL_QI + N_IDX_HEADS * IDX_DIM
COL_U = COL_KW + LANES
COL_P = COL_U + WIDTH_B
D_IN_PAD = COL_P + WIDTH_C
KW_USED = IDX_DIM + N_IDX_HEADS

VMEM_LIMIT = 56 * 1024 * 1024

ROW_TILE = 512
MERGE_ROW_TILE = 1024
MERGE_COL_TILE = 256
FFN_COL_TILE = 512
ADA_COL_TILE = 1024
S5_TIME_BLOCK = 128


def _cparams(sem):
    return pltpu.CompilerParams(dimension_semantics=sem, vmem_limit_bytes=VMEM_LIMIT)


def _norm_mod(x, g, sc, sh):
    ms = jnp.mean(x * x, axis=-1, keepdims=True)
    y = x * lax.rsqrt(ms + EPS)
    return (y * g) * (1.0 + sc) + sh


def _rope(x, c, sa, sb, half):
    return x * c + pltpu.roll(x, LANES - half, 1) * sa + pltpu.roll(x, half, 1) * sb


def _ada_kernel(c_ref, w_ref, b_ref, o_ref):
    c = c_ref[...]
    ca = c * jax.nn.sigmoid(c)
    o_ref[0] = jnp.dot(ca.astype(BF16), w_ref[0].astype(BF16), preferred_element_type=F32) + b_ref[0]


def _ada(c, w_ada, b_ada):
    depth, d, n = w_ada.shape
    tn = ADA_COL_TILE
    b = c.shape[0]
    return pl.pallas_call(
        _ada_kernel,
        grid=(depth, n // tn),
        in_specs=[
            pl.BlockSpec((b, d), lambda l, j: (0, 0)),
            pl.BlockSpec((1, d, tn), lambda l, j: (l, 0, j)),
            pl.BlockSpec((1, 1, tn), lambda l, j: (l, 0, j)),
        ],
        out_specs=pl.BlockSpec((1, b, tn), lambda l, j: (l, 0, j)),
        out_shape=jax.ShapeDtypeStruct((depth, b, n), F32),
        compiler_params=_cparams(("arbitrary", "arbitrary")),
        name="ada",
    )(c, w_ada, b_ada.reshape(depth, 1, n))


def _proj_kernel(x_ref, mod_ref, g_ref, w_ref, gq_ref, gk_ref, cq_ref, saq_ref, sbq_ref,
                 ci_ref, sai_ref, sbi_ref,
                 h_ref, q_ref, k_ref, v_ref, qi_ref, ki2_ref, kw_ref, u_ref, p_ref):
    h = _norm_mod(x_ref[...], g_ref[...], mod_ref[0, 1:2, :], mod_ref[0, 0:1, :]).astype(BF16)
    h_ref[...] = h

    def mm(c0, width):
        return jnp.dot(h, w_ref[:, c0:c0 + width], preferred_element_type=F32)

    def qk_norm_rope(xh, g):
        ms = jnp.mean(xh * xh, axis=-1, keepdims=True)
        y = xh * lax.rsqrt(ms + EPS) * g
        return _rope(y, cq_ref[...], saq_ref[...], sbq_ref[...], ROT_DIM // 2)

    def store_stacked(ref, slab, idx, nslab):
        for qbl in range(slab.shape[0] // Q_BLOCK):
            r0 = (qbl * nslab + idx) * Q_BLOCK
            ref[r0:r0 + Q_BLOCK, :] = slab[qbl * Q_BLOCK:(qbl + 1) * Q_BLOCK]

    q_all = mm(0, WIDTH_A)
    for hd in range(N_HEADS_A):
        qh = (qk_norm_rope(q_all[:, hd * HEAD_DIM:(hd + 1) * HEAD_DIM], gq_ref[...]) * LOGIT_SCALE).astype(BF16)
        store_stacked(q_ref, qh, hd, N_HEADS_A)
    kv = mm(COL_K, 2 * HEAD_DIM)
    k_ref[...] = qk_norm_rope(kv[:, 0:HEAD_DIM], gk_ref[...]).T.astype(BF16)
    v_ref[...] = kv[:, HEAD_DIM:2 * HEAD_DIM].astype(BF16)
    qi_all = mm(COL_QI, N_IDX_PAIRS * LANES)
    for s in range(N_IDX_PAIRS):
        qi = qi_all[:, s * LANES:(s + 1) * LANES]
        qi = _rope(qi, ci_ref[...], sai_ref[...], sbi_ref[...], IDX_ROT_DIM // 2).astype(BF16)
        store_stacked(qi_ref, qi, s, N_IDX_PAIRS)
    rest = mm(COL_KW, LANES + WIDTH_B + WIDTH_C)
    kw = rest[:, 0:LANES]
    ki = _rope(kw, ci_ref[...], sai_ref[...], sbi_ref[...], IDX_ROT_DIM // 2)
    lane = lax.broadcasted_iota(I32, ki.shape, 1)
    ki_lo = jnp.where(lane < IDX_DIM, ki, 0.0)
    ki2_ref[0:LANES, :] = ki_lo.T.astype(BF16)
    ki2_ref[LANES:2 * LANES, :] = pltpu.roll(ki_lo, IDX_DIM, 1).T.astype(BF16)
    kw_ref[...] = kw * (N_IDX_HEADS ** -0.5 * IDX_DIM ** -0.5)
    u_ref[...] = rest[:, LANES:LANES + WIDTH_B]
    p_ref[...] = rest[:, LANES + WIDTH_B:]


def _proj(x2, mod, g1, w_in_pad, g_q, g_k, tabs_q, tabs_i, batch, seq, tm):
    n, d = x2.shape
    tpb = seq // tm
    row = lambda i: (i, 0)
    const = lambda i: (0, 0)
    tab_spec = pl.BlockSpec((tm, LANES), row)
    return pl.pallas_call(
        _proj_kernel,
        grid=(n // tm,),
        in_specs=[
            pl.BlockSpec((tm, d), row),
            pl.BlockSpec((1, 6, d), lambda i: (i // tpb, 0, 0)),
            pl.BlockSpec((1, d), const),
            pl.BlockSpec((d, D_IN_PAD), const),
            pl.BlockSpec((1, HEAD_DIM), const),
            pl.BlockSpec((1, HEAD_DIM), const),
            tab_spec, tab_spec, tab_spec, tab_spec, tab_spec, tab_spec,
        ],
        out_specs=[
            pl.BlockSpec((tm, d), row),
            pl.BlockSpec((tm * N_HEADS_A, HEAD_DIM), row),
            pl.BlockSpec((HEAD_DIM, tm), lambda i: (i // tpb, i % tpb)),
            pl.BlockSpec((tm, HEAD_DIM), row),
            pl.BlockSpec((tm * N_IDX_PAIRS, LANES), row),
            pl.BlockSpec((2 * LANES, tm), lambda i: (i // tpb, i % tpb)),
            pl.BlockSpec((tm, LANES), row),
            pl.BlockSpec((tm, WIDTH_B), lambda i: (i % tpb, i // tpb)),
            pl.BlockSpec((tm, WIDTH_C), row),
        ],
        out_shape=[
            jax.ShapeDtypeStruct((n, d), BF16),
            jax.ShapeDtypeStruct((n * N_HEADS_A, HEAD_DIM), BF16),
            jax.ShapeDtypeStruct((batch * HEAD_DIM, seq), BF16),
            jax.ShapeDtypeStruct((n, HEAD_DIM), BF16),
            jax.ShapeDtypeStruct((n * N_IDX_PAIRS, LANES), BF16),
            jax.ShapeDtypeStruct((batch * 2 * LANES, seq), BF16),
            jax.ShapeDtypeStruct((n, LANES), F32),
            jax.ShapeDtypeStruct((seq, batch * WIDTH_B), F32),
            jax.ShapeDtypeStruct((n, WIDTH_C), F32),
        ],
        compiler_params=_cparams(("arbitrary",)),
        name="proj",
    )(x2, mod, g1, w_in_pad, g_q, g_k, *tabs_q, *tabs_i)


KEY_SLABS = 4
KEY_BLOCK = KEY_SLABS * LANES


def _dsa_scores(qi_ref, ki2_ref, kw_ref, s_ref, nqb):
    row = lax.broadcasted_iota(I32, (Q_BLOCK, KEY_BLOCK), 0)
    lane = lax.broadcasted_iota(I32, (Q_BLOCK, KEY_BLOCK), 1)
    chunk_end = (row // CHUNK + 1) * CHUNK
    trans_b = (((1,), (1,)), ((), ()))

    def qb_body(qq, carry):
        r0 = pl.multiple_of(qq * Q_BLOCK, Q_BLOCK)
        w = kw_ref[pl.ds(r0, Q_BLOCK), :]
        wb = [jnp.broadcast_to(w[:, IDX_DIM + hd:IDX_DIM + hd + 1], (Q_BLOCK, LANES))
              for hd in range(N_IDX_HEADS)]
        limit_row = r0 + chunk_end

        def key_body(j, c2):
            k0 = pl.multiple_of(j * KEY_BLOCK, KEY_BLOCK)
            kia = ki2_ref[0:LANES, pl.ds(k0, KEY_BLOCK)]
            kib = ki2_ref[LANES:2 * LANES, pl.ds(k0, KEY_BLOCK)]
            qp = qi_ref[pl.ds(pl.multiple_of(r0 * N_IDX_PAIRS, Q_BLOCK), Q_BLOCK * N_IDX_PAIRS), :]
            sa = jnp.dot(qp, kia, preferred_element_type=F32)
            sb = jnp.dot(qp, kib, preferred_element_type=F32)
            slabs = []
            for sl in range(KEY_SLABS):
                cols = slice(sl * LANES, (sl + 1) * LANES)
                acc = jnp.zeros((Q_BLOCK, LANES), F32)
                for pr in range(N_IDX_PAIRS):
                    rows = slice(pr * Q_BLOCK, (pr + 1) * Q_BLOCK)
                    acc = (acc + jnp.maximum(sa[rows, cols], 0.0) * wb[2 * pr]
                           + jnp.maximum(sb[rows, cols], 0.0) * wb[2 * pr + 1])
                slabs.append(acc)
            sc = jnp.concatenate(slabs, axis=1)
            adm = ((k0 + lane) < limit_row) & (sc > NEG_INF * 0.5)
            bits = pltpu.bitcast(sc, I32)
            bits = jnp.where(bits == INT_MIN, 0, bits)
            skey = bits ^ ((bits >> 31) & 0x7FFFFFFF)
            skey = jnp.where(adm, skey, INT_MIN)
            for sl in range(KEY_SLABS):
                s_ref[j * KEY_SLABS + sl, pl.ds(r0, Q_BLOCK), :] = skey[:, sl * LANES:(sl + 1) * LANES]
            return c2

        nkb = (r0 + Q_BLOCK + KEY_BLOCK - 1) // KEY_BLOCK
        lax.fori_loop(0, nkb, key_body, 0)
        return carry

    lax.fori_loop(0, nqb, qb_body, 0)


def _dsa_threshold(s_ref, thr_ref, cnt_ref, nqb, topk):
    thr_ref[...] = jnp.full(thr_ref.shape, INT_MIN, I32)

    ones = jnp.ones((LANES, LANES), BF16)
    first_searched = min(topk // Q_BLOCK, nqb)
    if first_searched:
        cnt_ref[0:first_searched * Q_BLOCK, :] = jnp.zeros((first_searched * Q_BLOCK, LANES), BF16)

    def count_ge(offset):
        for qq in range(first_searched, nqb):
            rows = slice(qq * Q_BLOCK, (qq + 1) * Q_BLOCK)
            cand = thr_ref[rows, :] + offset
            acc = jnp.zeros((Q_BLOCK, LANES), F32)
            for s in range(qq + 1):
                acc = acc + jnp.where(s_ref[s, rows, :] >= cand, 1.0, 0.0)
            cnt_ref[rows, :] = acc.astype(BF16)
        return jnp.dot(cnt_ref[...], ones, preferred_element_type=F32)

    def bit_body(i, carry):
        bitval = jnp.left_shift(jnp.int32(1), 31 - i)
        cnt = count_ge(bitval)
        thr = thr_ref[...]
        thr_ref[...] = jnp.where(cnt >= float(topk), thr + bitval, thr)
        return carry

    lax.fori_loop(0, 32, bit_body, 0)

    surplus = jnp.where((count_ge(0) > float(topk)) & (thr_ref[...] > INT_MIN), 1.0, 0.0).astype(BF16)
    seq = thr_ref.shape[0]
    block_of_row = lax.broadcasted_iota(I32, (nqb, seq), 1) // Q_BLOCK
    in_block = jnp.where(block_of_row == lax.broadcasted_iota(I32, (nqb, seq), 0), 1.0, 0.0).astype(BF16)
    surplus_rows = jnp.dot(in_block, surplus, preferred_element_type=F32)
    li = lax.broadcasted_iota(I32, (LANES, LANES), 0)
    lj = lax.broadcasted_iota(I32, (LANES, LANES), 1)
    upper = jnp.where(li <= lj, 1.0, 0.0).astype(BF16)

    for qq in range(nqb):
        rows = slice(qq * Q_BLOCK, (qq + 1) * Q_BLOCK)

        @pl.when(surplus_rows[qq, 0] > 0.0)
        def _():
            thr = thr_ref[rows, :]

            n_gt = jnp.zeros((Q_BLOCK, LANES), F32)
            for s in range(qq + 1):
                n_gt = n_gt + jnp.where(s_ref[s, rows, :] > thr, 1.0, 0.0)
            keep = float(topk) - jnp.dot(n_gt.astype(BF16), ones, preferred_element_type=F32)
            seen = jnp.zeros((Q_BLOCK, LANES), F32)
            for s in range(qq + 1):
                sk = s_ref[s, rows, :]
                tie = sk == thr
                tie_b = jnp.where(tie, 1.0, 0.0).astype(BF16)
                rank = seen + jnp.dot(tie_b, upper, preferred_element_type=F32)
                drop = tie & (rank > keep) & (thr > INT_MIN)
                s_ref[s, rows, :] = jnp.where(drop, thr - 1, sk)
                seen = seen + jnp.dot(tie_b, ones, preferred_element_type=F32)


def _dsa_kernel(q_ref, k_ref, v_ref, qi_ref, ki2_ref, kw_ref, o_ref,
                s_ref, thr_ref, cnt_ref, bias_ref, lg_ref, p_ref, acc_ref, vx_ref, *m_refs,
                topk, nqb):
    qb = pl.program_id(1)

    @pl.when(qb == 0)
    def _():
        _dsa_scores(qi_ref, ki2_ref, kw_ref, s_ref, nqb)
        _dsa_threshold(s_ref, thr_ref, cnt_ref, nqb, topk)
        vx_ref[:, 0:HEAD_DIM] = v_ref[...]
        vx_ref[:, HEAD_DIM:] = jnp.ones((vx_ref.shape[0], LANES), BF16)

    r0 = pl.multiple_of(qb * Q_BLOCK, Q_BLOCK)
    nkb = (r0 + Q_BLOCK + KEY_BLOCK - 1) // KEY_BLOCK
    trans_b = (((1,), (1,)), ((), ()))
    thr = thr_ref[pl.ds(r0, Q_BLOCK), :]

    def bias_body(j, carry):
        for sl in range(KEY_SLABS):
            sk = s_ref[j * KEY_SLABS + sl, pl.ds(r0, Q_BLOCK), :]
            sel = (sk >= thr) & (sk > INT_MIN)
            bias_ref[j * KEY_SLABS + sl] = jnp.where(sel, 0.0, NEG_INF)
        return carry

    lax.fori_loop(0, nkb, bias_body, 0)

    for hd in range(N_HEADS_A):
        m_refs[hd][...] = jnp.full((Q_BLOCK, LANES), NEG_INF, F32)
    acc_ref[...] = jnp.zeros(acc_ref.shape, F32)

    def max_body(j, carry):
        k0 = pl.multiple_of(j * KEY_BLOCK, KEY_BLOCK)
        s = jnp.dot(q_ref[...], k_ref[:, pl.ds(k0, KEY_BLOCK)], preferred_element_type=F32)
        for hd in range(N_HEADS_A):
            rows = slice(hd * Q_BLOCK, (hd + 1) * Q_BLOCK)
            m = m_refs[hd][...]
            for sl in range(KEY_SLABS):
                sh = s[rows, sl * LANES:(sl + 1) * LANES] + bias_ref[j * KEY_SLABS + sl]
                lg_ref[j * KEY_SLABS + sl, rows, :] = sh
                m = jnp.maximum(m, sh)
            m_refs[hd][...] = m
        return carry

    lax.fori_loop(0, nkb, max_body, 0)
    for hd in range(N_HEADS_A):
        m_refs[hd][...] = jnp.broadcast_to(jnp.max(m_refs[hd][...], axis=1, keepdims=True), (Q_BLOCK, LANES))

    def sum_body(j, carry):
        k0 = pl.multiple_of(j * KEY_BLOCK, KEY_BLOCK)
        for hd in range(N_HEADS_A):
            rows = slice(hd * Q_BLOCK, (hd + 1) * Q_BLOCK)
            m = m_refs[hd][...]
            for sl in range(KEY_SLABS):
                p_s = jnp.exp2(lg_ref[j * KEY_SLABS + sl, rows, :] - m)
                p_ref[rows, sl * LANES:(sl + 1) * LANES] = p_s.astype(BF16)
        acc_ref[...] += jnp.dot(p_ref[...], vx_ref[pl.ds(k0, KEY_BLOCK), :], preferred_element_type=F32)
        return carry

    lax.fori_loop(0, nkb, sum_body, 0)
    for hd in range(N_HEADS_A):
        rows = slice(hd * Q_BLOCK, (hd + 1) * Q_BLOCK)
        o_ref[:, hd * HEAD_DIM:(hd + 1) * HEAD_DIM] = (
            acc_ref[rows, 0:HEAD_DIM] / acc_ref[rows, HEAD_DIM:]).astype(BF16)


def _dsa(q, k, v, qi, ki2, kw, batch, seq, topk):
    assert seq % KEY_BLOCK == 0
    n = v.shape[0]
    nqb = seq // Q_BLOCK
    qrow = lambda b, i: (b * nqb + i, 0)
    brow = lambda b, i: (b, 0)
    nsl = seq // LANES
    head_scratch = [pltpu.VMEM((Q_BLOCK, LANES), F32)] * N_HEADS_A
    return pl.pallas_call(
        functools.partial(_dsa_kernel, topk=topk, nqb=nqb),
        grid=(batch, nqb),
        in_specs=[
            pl.BlockSpec((Q_BLOCK * N_HEADS_A, HEAD_DIM), qrow),
            pl.BlockSpec((HEAD_DIM, seq), brow),
            pl.BlockSpec((seq, HEAD_DIM), brow),
            pl.BlockSpec((seq * N_IDX_PAIRS, LANES), brow),
            pl.BlockSpec((2 * LANES, seq), brow),
            pl.BlockSpec((seq, LANES), brow),
        ],
        out_specs=pl.BlockSpec((Q_BLOCK, WIDTH_A), qrow),
        out_shape=jax.ShapeDtypeStruct((n, WIDTH_A), BF16),
        scratch_shapes=[
            pltpu.VMEM((nsl, seq, LANES), I32),
            pltpu.VMEM((seq, LANES), I32),
            pltpu.VMEM((seq, LANES), BF16),
            pltpu.VMEM((nsl, Q_BLOCK, LANES), F32),
            pltpu.VMEM((nsl, Q_BLOCK * N_HEADS_A, LANES), F32),
            pltpu.VMEM((Q_BLOCK * N_HEADS_A, KEY_BLOCK), BF16),
            pltpu.VMEM((Q_BLOCK * N_HEADS_A, HEAD_DIM + LANES), F32),
            pltpu.VMEM((seq, HEAD_DIM + LANES), BF16),
        ] + head_scratch,
        compiler_params=_cparams(("arbitrary", "arbitrary")),
        name="dsa",
    )(q, k, v, qi, ki2, kw)


def _s5_kernel(u_ref, bm_ref, cm_ref, lre_ref, lim_ref, dsk_ref, wglu_ref, o_ref,
               bu_ref, st_ref, y_ref, *, batch, tl):
    @pl.when(pl.program_id(0) == 0)
    def _():
        st_ref[...] = jnp.zeros(st_ref.shape, F32)

    for gb in range(N_SSM_BUNDLES):
        ch = slice(gb * BUNDLE_CH, (gb + 1) * BUNDLE_CH)
        u_g = u_ref[:, ch]
        bu_ref[...] = jnp.dot(u_g.astype(BF16), bm_ref[gb], preferred_element_type=F32)
        lam_re = jnp.broadcast_to(lre_ref[gb], (batch, BUNDLE_ST))
        lam_im = jnp.broadcast_to(lim_ref[gb], (batch, BUNDLE_ST))

        def step(t, carry):
            x_re, x_im = carry
            r0 = pl.multiple_of(t * batch, batch)
            n_re = lam_re * x_re - lam_im * x_im + bu_ref[pl.ds(r0, batch), 0:BUNDLE_ST]
            n_im = lam_re * x_im + lam_im * x_re + bu_ref[pl.ds(r0, batch), BUNDLE_ST:2 * BUNDLE_ST]
            bu_ref[pl.ds(r0, batch), 0:BUNDLE_ST] = n_re
            bu_ref[pl.ds(r0, batch), BUNDLE_ST:2 * BUNDLE_ST] = n_im
            return n_re, n_im

        x_re, x_im = lax.fori_loop(
            0, tl, step, (st_ref[gb, :, 0:BUNDLE_ST], st_ref[gb, :, BUNDLE_ST:2 * BUNDLE_ST]), unroll=4)
        st_ref[gb, :, 0:BUNDLE_ST] = x_re
        st_ref[gb, :, BUNDLE_ST:2 * BUNDLE_ST] = x_im
        y = jnp.dot(bu_ref[...].astype(BF16), cm_ref[gb], preferred_element_type=F32)
        y_ref[:, ch] = y + dsk_ref[:, ch] * u_g

    y = jax.nn.gelu(y_ref[...])
    gl = jnp.dot(y.astype(BF16), wglu_ref[...], preferred_element_type=F32)
    o_ref[...] = (y * jax.nn.sigmoid(gl)).astype(BF16)


def _s5_params(a_re, a_im, b_re, b_im, c_re, c_im, log_dt):
    dt = jnp.exp(log_dt)[:, None]
    mag = jnp.exp(a_re * dt)
    lb_re = mag * jnp.cos(a_im * dt)
    lb_im = mag * jnp.sin(a_im * dt)
    nr, ni = lb_re - 1.0, lb_im
    den = a_re * a_re + a_im * a_im
    q_re = (nr * a_re + ni * a_im) / den
    q_im = (ni * a_re - nr * a_im) / den
    bb_re = q_re[..., None] * b_re - q_im[..., None] * b_im
    bb_im = q_re[..., None] * b_im + q_im[..., None] * b_re
    eye = jnp.eye(SSM_BUNDLE, dtype=F32)
    nb = N_SSM_BUNDLES

    def pack_b(m):
        m = m.reshape(nb, SSM_BUNDLE, SSM_STATE, SSM_GROUP)
        return jnp.einsum('bgpi,gh->bgihp', m, eye).reshape(nb, BUNDLE_CH, BUNDLE_ST)

    def pack_c(m):
        m = m.reshape(nb, SSM_BUNDLE, SSM_GROUP, SSM_STATE)
        return jnp.einsum('bgop,gh->bgpho', m, eye).reshape(nb, BUNDLE_ST, BUNDLE_CH)

    bm = jnp.concatenate([pack_b(bb_re), pack_b(bb_im)], axis=2).astype(BF16)
    cm = jnp.concatenate([pack_c(c_re), -pack_c(c_im)], axis=1).astype(BF16)
    return bm, cm, lb_re.reshape(nb, 1, BUNDLE_ST), lb_im.reshape(nb, 1, BUNDLE_ST)


def _s5(u_tm, bm, cm, lre, lim, d_skip, w_glu, batch, seq, tl):
    rows = tl * batch
    const3 = lambda t: (0, 0, 0)
    const2 = lambda t: (0, 0)
    return pl.pallas_call(
        functools.partial(_s5_kernel, batch=batch, tl=tl),
        grid=(seq // tl,),
        in_specs=[
            pl.BlockSpec((rows, WIDTH_B), lambda t: (t, 0)),
            pl.BlockSpec(bm.shape, const3),
            pl.BlockSpec(cm.shape, const3),
            pl.BlockSpec(lre.shape, const3),
            pl.BlockSpec(lim.shape, const3),
            pl.BlockSpec((1, WIDTH_B), const2),
            pl.BlockSpec((WIDTH_B, WIDTH_B), const2),
        ],
        out_specs=pl.BlockSpec((rows, WIDTH_B), lambda t: (t, 0)),
        out_shape=jax.ShapeDtypeStruct((seq * batch, WIDTH_B), BF16),
        scratch_shapes=[
            pltpu.VMEM((rows, 2 * BUNDLE_ST), F32),
            pltpu.VMEM((N_SSM_BUNDLES, batch, 2 * BUNDLE_ST), F32),
            pltpu.VMEM((rows, WIDTH_B), F32),
        ],
        compiler_params=_cparams(("arbitrary",)),
        name="s5",
    )(u_tm, bm, cm, lre, lim, d_skip, w_glu)


def _pool_kernel(p_ref, w_ref, sc_ref, o_ref):
    seq = p_ref.shape[0]
    t = lax.broadcasted_iota(I32, (seq, POOL_GROUP), 0)
    t1 = (t + 1).astype(F32)
    for g, win in enumerate(POOL_WINDOWS):
        ch = slice(g * POOL_GROUP, (g + 1) * POOL_GROUP)
        x = p_ref[:, ch]
        s = x
        sh = 1
        while sh < win:
            s = s + jnp.where(t >= sh, pltpu.roll(s, sh, 0), 0.0)
            sh *= 2
        pooled = s / jnp.minimum(t1, float(win)) - x
        y = jnp.dot(pooled.astype(BF16), w_ref[g], preferred_element_type=F32)
        o_ref[:, ch] = (y * sc_ref[:, ch]).astype(BF16)


def _pool(p, w_pool, pool_scale, batch, seq):
    return pl.pallas_call(
        _pool_kernel,
        grid=(batch,),
        in_specs=[
            pl.BlockSpec((seq, WIDTH_C), lambda b: (b, 0)),
            pl.BlockSpec(w_pool.shape, lambda b: (0, 0, 0)),
            pl.BlockSpec((1, WIDTH_C), lambda b: (0, 0)),
        ],
        out_specs=pl.BlockSpec((seq, WIDTH_C), lambda b: (b, 0)),
        out_shape=jax.ShapeDtypeStruct(p.shape, BF16),
        compiler_params=_cparams(("arbitrary",)),
        name="pool",
    )(p, w_pool, pool_scale)


def _col_tiles(w, tn):
    *lead, k, n = w.shape
    nl = len(lead)
    return w.reshape(*lead, k, n // tn, tn).transpose(*range(nl), nl + 1, nl, nl + 2)


def _merge_kernel(h_ref, xc_ref, oa_ref, ob_ref, oc_ref, wg_ref, bg_ref,
                  pa_ref, pb_ref, pc_ref, wo_ref, gt_ref, o_ref, mg_ref, *, ntiles, tn):
    i = pl.program_id(0)
    j = pl.program_id(1)
    slot = i % 2

    @pl.when(i < ntiles)
    def _():
        h = h_ref[...]
        branches = ((oa_ref, pa_ref), (ob_ref, pb_ref), (oc_ref, pc_ref))
        merged = None
        for b, (o_r, p_r) in enumerate(branches):
            gate = jax.nn.sigmoid(jnp.dot(h, wg_ref[b, 0], preferred_element_type=F32) + bg_ref[0, b:b + 1, :])
            term = gate * jnp.dot(o_r[...], p_r[0], preferred_element_type=F32)
            merged = term if merged is None else merged + term
        c0 = pl.multiple_of(j * tn, tn)
        mg_ref[slot, :, pl.ds(c0, tn)] = merged.astype(BF16)

    @pl.when(i >= 1)
    def _():
        mix = jnp.dot(mg_ref[1 - slot], wo_ref[0], preferred_element_type=F32)
        o_ref[...] = xc_ref[...] + gt_ref[0] * mix


def _merge(x2, h, gt, o_a, o_b_tm, o_c, wg, bg, p_a, p_b, p_c, w_out, seq, tm, tn):
    n, d = x2.shape
    nj = d // tn
    ntiles = n // tm
    tpb = seq // tm
    cur = lambda i: jnp.minimum(i, ntiles - 1)
    prev = lambda i: jnp.maximum(i - 1, 0)
    row = lambda i, j: (cur(i), 0)
    wt = lambda i, j: (j, 0, 0)
    return pl.pallas_call(
        functools.partial(_merge_kernel, ntiles=ntiles, tn=tn),
        grid=(ntiles + 1, nj),
        in_specs=[
            pl.BlockSpec((tm, d), row),
            pl.BlockSpec((tm, tn), lambda i, j: (prev(i), j)),
            pl.BlockSpec((tm, WIDTH_A), row),
            pl.BlockSpec((tm, WIDTH_B), lambda i, j: (cur(i) % tpb, cur(i) // tpb)),
            pl.BlockSpec((tm, WIDTH_C), row),
            pl.BlockSpec((3, 1, d, tn), lambda i, j: (0, j, 0, 0)),
            pl.BlockSpec((1, 3, tn), wt),
            pl.BlockSpec((1, WIDTH_A, tn), wt),
            pl.BlockSpec((1, WIDTH_B, tn), wt),
            pl.BlockSpec((1, WIDTH_C, tn), wt),
            pl.BlockSpec((1, d, tn), wt),
            pl.BlockSpec((1, 1, tn), lambda i, j: (prev(i) // tpb, 0, j)),
        ],
        out_specs=pl.BlockSpec((tm, tn), lambda i, j: (prev(i), jnp.where(i == 0, 0, j))),
        out_shape=jax.ShapeDtypeStruct((n, d), F32),
        scratch_shapes=[pltpu.VMEM((2, tm, d), BF16)],
        compiler_params=_cparams(("arbitrary", "arbitrary")),
        name="merge",
    )(h, x2, o_a, o_b_tm, o_c, _col_tiles(wg, tn), _col_tiles(bg, tn), _col_tiles(p_a, tn),
      _col_tiles(p_b, tn), _col_tiles(p_c, tn), _col_tiles(w_out, tn), gt)


FFN_HALO = 16


def _ffn_kernel(x_ref, xh_ref, mod_ref, g_ref, wa_ref, wb_ref, cw_ref, cb_ref, wd_ref, o_ref,
                h_ref, acc_ref, *, tm, tpb):
    i = pl.program_id(0)
    j = pl.program_id(1)
    nj = pl.num_programs(1)

    @pl.when(j == 0)
    def _():
        g, sc, sh = g_ref[...], mod_ref[0, 4:5, :], mod_ref[0, 3:4, :]
        halo = _norm_mod(xh_ref[...], g, sc, sh)
        h_ref[0:FFN_HALO, :] = jnp.where(i % tpb == 0, 0.0, halo).astype(BF16)
        h_ref[FFN_HALO:, :] = _norm_mod(x_ref[...], g, sc, sh).astype(BF16)
        acc_ref[...] = jnp.zeros(acc_ref.shape, F32)

    a = jnp.dot(h_ref[...], wa_ref[0], preferred_element_type=F32)
    b = jnp.dot(h_ref[FFN_HALO:, :], wb_ref[0], preferred_element_type=F32)
    a_conv = cb_ref[...] + a[FFN_HALO - 2:FFN_HALO - 2 + tm] * cw_ref[0:1, :]
    a_conv = a_conv + a[FFN_HALO - 1:FFN_HALO - 1 + tm] * cw_ref[1:2, :]
    a_conv = a_conv + a[FFN_HALO:] * cw_ref[2:3, :]
    act = (a_conv * jax.nn.sigmoid(a_conv)) * b
    acc_ref[...] += jnp.dot(act.astype(BF16), wd_ref[...], preferred_element_type=F32)

    @pl.when(j == nj - 1)
    def _():
        o_ref[...] = x_ref[...] + mod_ref[0, 5:6, :] * acc_ref[...]


def _ffn(x2, mod, g2, w_a, w_b, conv_w, conv_b, w_down, seq, tm, tn):
    n, d = x2.shape
    dff = w_a.shape[1]
    tpb = seq // tm
    hb = tm // FFN_HALO
    return pl.pallas_call(
        functools.partial(_ffn_kernel, tm=tm, tpb=tpb),
        grid=(n // tm, dff // tn),
        in_specs=[
            pl.BlockSpec((tm, d), lambda i, j: (i, 0)),
            pl.BlockSpec((FFN_HALO, d), lambda i, j: (jnp.maximum(i * hb - 1, 0), 0)),
            pl.BlockSpec((1, 6, d), lambda i, j: (i // tpb, 0, 0)),
            pl.BlockSpec((1, d), lambda i, j: (0, 0)),
            pl.BlockSpec((1, d, tn), lambda i, j: (j, 0, 0)),
            pl.BlockSpec((1, d, tn), lambda i, j: (j, 0, 0)),
            pl.BlockSpec((CONV_WIDTH, tn), lambda i, j: (0, j)),
            pl.BlockSpec((1, tn), lambda i, j: (0, j)),
            pl.BlockSpec((tn, d), lambda i, j: (j, 0)),
        ],
        out_specs=pl.BlockSpec((tm, d), lambda i, j: (i, 0)),
        out_shape=jax.ShapeDtypeStruct((n, d), F32),
        scratch_shapes=[pltpu.VMEM((tm + FFN_HALO, d), BF16), pltpu.VMEM((tm, d), F32)],
        compiler_params=_cparams(("arbitrary", "arbitrary")),
        name="ffn",
    )(x2, x2, mod, g2, _col_tiles(w_a, tn), _col_tiles(w_b, tn), conv_w, conv_b, w_down)


def _rope_tables(positions, rot_dim, period):
    half = rot_dim // 2
    inv_freq = ROPE_THETA ** (-jnp.arange(half, dtype=F32) * (2.0 / rot_dim))
    ang = positions.astype(F32)[..., None] * inv_freq
    cos, sin = jnp.cos(ang), jnp.sin(ang)
    rest = period - 2 * half
    ones = jnp.ones(cos.shape[:-1] + (rest,), F32)
    zh = jnp.zeros_like(sin)
    zr = jnp.zeros_like(ones)
    reps = LANES // period
    out = []
    for parts in ((cos, cos, ones), (-sin, zh, zr), (zh, sin, zr)):
        t = jnp.concatenate(parts, axis=-1)
        out.append(jnp.tile(t, (1, 1, reps)).reshape(-1, LANES))
    return out


def _pad_cols(w, new):
    return jnp.pad(w, ((0, 0), (0, new - w.shape[1])))


def kernel(x, c, positions, w_ada, b_ada, g_norm1, g_norm2, w_in, g_q, g_k, a_re, a_im, b_re, b_im,
           c_re, c_im, d_skip, log_dt, w_glu, w_pool, pool_scale, p_a, p_b, p_c, w_gate, b_gate,
           w_out, w_up, conv_w, conv_b, w_down):
    batch, seq, d = x.shape
    depth = w_ada.shape[0]
    n = batch * seq
    tm, tm_merge, tl = min(ROW_TILE, seq), min(MERGE_ROW_TILE, seq), min(S5_TIME_BLOCK, seq)
    tn_merge, tn_ffn = MERGE_COL_TILE, FFN_COL_TILE
    topk = min(TOPK_MAX, seq // 4)
    dff_pad = -(-D_FF // tn_ffn) * tn_ffn

    mod_all = _ada(c, w_ada, b_ada).reshape(depth, batch, 6, d)
    tabs_q = _rope_tables(positions, ROT_DIM, HEAD_DIM)
    tabs_i = _rope_tables(positions, IDX_ROT_DIM, IDX_DIM)

    x2 = x.reshape(n, d)
    for l in range(depth):
        mod = mod_all[l]
        gt1 = mod[:, 2:3, :]
        w_in_pad = jnp.concatenate(
            [w_in[l][:, :COL_KW + KW_USED],
             jnp.zeros((d, LANES - KW_USED), F32),
             w_in[l][:, COL_KW + KW_USED:]], axis=1).astype(BF16)
        h, q, k, v, qi, ki2, kw, u_tm, p = _proj(
            x2, mod, g_norm1[l][None], w_in_pad, g_q[l][None], g_k[l][None], tabs_q, tabs_i,
            batch, seq, tm)
        o_a = _dsa(q, k, v, qi, ki2, kw, batch, seq, topk)
        bm, cm, lre, lim = _s5_params(a_re[l], a_im[l], b_re[l], b_im[l], c_re[l], c_im[l], log_dt[l])
        o_b = _s5(u_tm.reshape(seq * batch, WIDTH_B), bm, cm, lre, lim, d_skip[l][None],
                  w_glu[l].astype(BF16), batch, seq, tl)
        o_c = _pool(p, w_pool[l].astype(BF16), pool_scale[l][None], batch, seq)
        x2 = _merge(x2, h, gt1, o_a, o_b.reshape(seq, batch * WIDTH_B), o_c,
                    w_gate[l].astype(BF16), b_gate[l], p_a[l].astype(BF16), p_b[l].astype(BF16),
                    p_c[l].astype(BF16), w_out[l].astype(BF16), seq, tm_merge, tn_merge)
        w_a = _pad_cols(w_up[l][:, :D_FF], dff_pad).astype(BF16)
        w_b = _pad_cols(w_up[l][:, D_FF:], dff_pad).astype(BF16)
        w_d = jnp.pad(w_down[l], ((0, dff_pad - D_FF), (0, 0))).astype(BF16)
        x2 = _ffn(x2, mod, g_norm2[l][None], w_a, w_b, _pad_cols(conv_w[l], dff_pad),
                  _pad_cols(conv_b[l][None], dff_pad), w_d, seq, tm, tn_ffn)
    return x2.reshape(batch, seq, d)
```
